```python
import math, functools
import jax, jax.numpy as jnp
from jax import lax
import numpy as np

D_MODEL = 1024
BATCH = 16
SEQ = 256
DEPTH = 2
DEC_BATCH = 2
DEC_SEQ = 4096
PAST_LEN = 512

GRID_W = 64
Q_BLOCK = 128
ROPE_BASE = 10000.0
NORM_EPS = 1e-6
HA = 4
DA = 64
HB = 4
NOPE = 64
ROPE_B = 32
VB = 128
Q_LORA = 192
KV_LORA = 128
HC = 8
NC = 64
W_LORA = 64
A_LORA = 64
G_LORA = 128
WKV_LN_EPS = 64e-5
WKV_DECAY_SCALE = 0.606531
HD = 8
PD = 64
G_SSM = 2
NS = 64
CHUNK = 128
D_INNER = HD * PD
CONV_CH = D_INNER + 2 * G_SSM * NS
F_FF = 2816
A_COLS = 3 * HA * 2 * DA
B_COLS = Q_LORA + KV_LORA + ROPE_B
C_COLS = 3 * HC * NC + 2 * W_LORA + 2 * A_LORA + G_LORA
D_COLS = D_INNER + CONV_CH + HD
MIX_WIDTH = HA * 2 * DA + HB * VB

kernel_name = 'hybrid_diffusion_trunk_step'


def rmsnorm(x, g, eps=NORM_EPS):
    xf = x.astype(jnp.float32)
    y = xf * lax.rsqrt(jnp.mean(xf * xf, axis=-1, keepdims=True) + eps)
    return (y * g).astype(x.dtype)


def head_layernorm(y, w, b, eps):
    yf = y.astype(jnp.float32)
    mu = jnp.mean(yf, axis=-1, keepdims=True)
    var = jnp.mean(jnp.square(yf - mu), axis=-1, keepdims=True)
    return ((yf - mu) * lax.rsqrt(var + eps) * w + b).astype(y.dtype)


def dwconv3(x, w, b):
    xp = jnp.pad(x, ((0, 0), (1, 1), (0, 0)))
    return xp[:, :-2] * w[0] + xp[:, 1:-1] * w[1] + xp[:, 2:] * w[2] + b


def centred_shift(x):
    xp = jnp.pad(x, ((0, 0), (1, 1), (0, 0)))
    return 0.5 * (xp[:, :-2] + xp[:, 2:])


def axial_rope_tables(length, dim):
    rows = length // GRID_W
    quarter = dim // 4
    inv = ROPE_BASE ** (-jnp.arange(quarter, dtype=jnp.float32) / quarter)
    pos = jnp.arange(rows * GRID_W)
    row = (pos // GRID_W).astype(jnp.float32)
    col = (pos % GRID_W).astype(jnp.float32)
    ang_r = row[:, None] * inv
    ang_c = col[:, None] * inv
    return jnp.cos(ang_r), jnp.sin(ang_r), jnp.cos(ang_c), jnp.sin(ang_c)


def rope_2d(x, tables):
    cr, sr, cc, sc = tables

    def rot(xh, c, s):
        x1, x2 = jnp.split(xh, 2, axis=-1)
        return jnp.concatenate([x1 * c - x2 * s, x1 * s + x2 * c], axis=-1)

    xr, xc = jnp.split(x, 2, axis=-1)
    return jnp.concatenate([rot(xr, cr, sr), rot(xc, cc, sc)], axis=-1).astype(x.dtype)


def attend(q, k, v, scale):
    b, h, lq, dq = q.shape
    nblk = lq // Q_BLOCK
    qb = q.reshape(b, h, nblk, Q_BLOCK, dq).transpose(2, 0, 1, 3, 4)

    def one(qblk):
        s = jnp.einsum('bhqd,bhkd->bhqk', qblk, k).astype(jnp.float32) * scale
        p = jax.nn.softmax(s, axis=-1).astype(v.dtype)
        return jnp.einsum('bhqk,bhkd->bhqd', p, v)

    o = lax.map(one, qb)
    return o.transpose(1, 2, 0, 3, 4).reshape(b, h, lq, v.shape[-1])


def wkv_scan(r, w, kk, a, k, v, S0, reverse):
    f32 = jnp.float32
    tm = lambda t: jnp.moveaxis(t.astype(f32), 1, 0)
    xs = (tm(r), tm(w), tm(-kk), tm(kk * a), tm(k), tm(v))

    def step(S, inp):
        r_t, w_t, a_t, b_t, k_t, v_t = inp
        sa = jnp.einsum('bhvk,bhk->bhv', S, a_t)
        S = S * w_t[:, :, None, :] + sa[..., None] * b_t[:, :, None, :] + v_t[..., None] * k_t[:, :, None, :]
        return S, jnp.einsum('bhvk,bhk->bhv', S, r_t)

    S, ys = lax.scan(step, S0.astype(f32), xs, reverse=reverse)
    return jnp.moveaxis(ys, 0, 1).astype(r.dtype), S


def ssd_scan(x, dt, A, Bm, Cm, S0):
    f32 = jnp.float32
    b, L, H, P = x.shape
    nc = L // CHUNK
    rep = H // Bm.shape[2]
    Bh = jnp.repeat(Bm.astype(f32), rep, axis=2).reshape(b, nc, CHUNK, H, -1)
    Ch = jnp.repeat(Cm.astype(f32), rep, axis=2).reshape(b, nc, CHUNK, H, -1)
    dtf = dt.astype(f32)
    xdt = (x.astype(f32) * dtf[..., None]).reshape(b, nc, CHUNK, H, P)
    acum = jnp.cumsum((dtf * A.astype(f32)).reshape(b, nc, CHUNK, H), axis=2)
    causal = jnp.tril(jnp.ones((CHUNK, CHUNK), dtype=bool))[None, None, :, :, None]
    seg = acum[:, :, :, None, :] - acum[:, :, None, :, :]
    decay_in = jnp.where(causal, jnp.exp(jnp.where(causal, seg, 0.0)), 0.0)
    scores = jnp.einsum('bcihn,bcjhn->bcijh', Ch, Bh) * decay_in
    y_intra = jnp.einsum('bcijh,bcjhp->bcihp', scores, xdt)
    decay_end = jnp.exp(acum[:, :, -1:, :] - acum)
    chunk_states = jnp.einsum('bcjhn,bcjhp->bchpn', Bh * decay_end[..., None], xdt)
    chunk_decay = jnp.exp(acum[:, :, -1, :])

    def step(S, inp):
        dec, st = inp
        return S * dec[:, :, None, None] + st, S

    S_fin, S_start = lax.scan(step, S0.astype(f32),
                              (jnp.moveaxis(chunk_decay, 1, 0), jnp.moveaxis(chunk_states, 1, 0)))
    S_start = jnp.moveaxis(S_start, 0, 1)
    y_inter = jnp.einsum('bcihn,bchpn->bcihp', Ch * jnp.exp(acum)[..., None], S_start)
    return (y_intra + y_inter).reshape(b, L, H, P).astype(x.dtype), S_fin


def even_mixer(hn, p, ctx, layer_idx):
    b, L, _ = hn.shape
    latent = ctx is not None
    proj = hn @ p['w_in']
    qa, ka, va, q_down, ckv, kpe = jnp.split(
        proj, [HA * 2 * DA, 2 * HA * 2 * DA, A_COLS, A_COLS + Q_LORA, A_COLS + Q_LORA + KV_LORA], axis=-1)
    heads = lambda t, n: t.reshape(t.shape[0], t.shape[1], n, -1).transpose(0, 2, 1, 3)
    qa, ka, va = heads(qa, HA), heads(ka, HA), heads(va, HA)
    q1, q2 = jnp.split(qa, 2, axis=-1)
    k1, k2 = jnp.split(ka, 2, axis=-1)
    qb = heads(rmsnorm(q_down, p['q_norm']) @ p['w_uq'], HB)
    q_nope, q_pe = jnp.split(qb, [NOPE], axis=-1)
    ckv = rmsnorm(ckv, p['kv_norm'])
    if latent:
        ta = axial_rope_tables(L, DA)
        tb = axial_rope_tables(L, ROPE_B)
        q1, q2, k1, k2 = rope_2d(q1, ta), rope_2d(q2, ta), rope_2d(k1, ta), rope_2d(k2, ta)
        q_pe, kpe = rope_2d(q_pe, tb), rope_2d(kpe, tb)
    own = (jnp.concatenate([k1, k2], axis=-1), va, ckv, kpe)
    if latent:
        k_all, v_all, ckv_all, kpe_all = (jnp.concatenate([cc.astype(oo.dtype), oo], axis=-2)
                                          for cc, oo in zip(ctx, own))
    else:
        k_all, v_all, ckv_all, kpe_all = own
    lambda_init = 0.8 - 0.6 * math.exp(-0.3 * layer_idx)
    f32 = jnp.float32
    lam = (jnp.exp(jnp.sum(p['lambda_q1'].astype(f32) * p['lambda_k1'].astype(f32)))
           - jnp.exp(jnp.sum(p['lambda_q2'].astype(f32) * p['lambda_k2'].astype(f32))) + lambda_init)
    k1a, k2a = jnp.split(k_all, 2, axis=-1)
    sa = DA ** -0.5
    oa = attend(q1, k1a, v_all, sa) - lam.astype(hn.dtype) * attend(q2, k2a, v_all, sa)
    oa = rmsnorm(oa, p['subln']) * (1.0 - lambda_init)
    Lk = ckv_all.shape[1]
    kv = (ckv_all @ p['w_ukv']).reshape(b, Lk, HB, NOPE + VB).transpose(0, 2, 1, 3)
    k_nope, vb = jnp.split(kv, [NOPE], axis=-1)
    kpe_h = jnp.broadcast_to(kpe_all[:, None], (b, HB, Lk, ROPE_B))
    ob = attend(jnp.concatenate([q_nope, q_pe], axis=-1), jnp.concatenate([k_nope, kpe_h], axis=-1),
                vb, (NOPE + ROPE_B) ** -0.5)
    merged = jnp.concatenate([oa.transpose(0, 2, 1, 3).reshape(b, L, -1),
                              ob.transpose(0, 2, 1, 3).reshape(b, L, -1)], axis=-1)
    return merged @ p['w_out'], own


def odd_mixer(hn, p, ctx):
    b, L, _ = hn.shape
    f32 = jnp.float32
    proj = hn @ p['w_in']
    zc, zd = jnp.split(proj, [C_COLS], axis=-1)
    zc = zc + p['mu'] * (centred_shift(zc) - zc)
    r, k, v, wd, ad, gd = jnp.split(
        zc, [HC * NC, 2 * HC * NC, 3 * HC * NC, 3 * HC * NC + 2 * W_LORA, 3 * HC * NC + 2 * W_LORA + 2 * A_LORA],
        axis=-1)
    hsplit = lambda t: t.reshape(b, L, HC, NC)
    r, k, v = hsplit(r), hsplit(k), hsplit(v)
    kkf = (k * p['k_k'].reshape(HC, NC)).astype(f32)
    kk = (kkf * lax.rsqrt(jnp.sum(kkf * kkf, axis=-1, keepdims=True) + 1e-12)).astype(k.dtype)
    g = jax.nn.sigmoid(gd) @ p['g2']
    wds = jnp.split(wd, 2, axis=-1)
    ads = jnp.split(ad, 2, axis=-1)
    k_a = p['k_a'].reshape(HC, NC)
    if ctx is None:
        wkv0 = (jnp.zeros((b, HC, NC, NC), f32), jnp.zeros((b, HC, NC, NC), f32))
        ssm0 = (jnp.zeros((b, HD, PD, NS), f32), jnp.zeros((b, HD, PD, NS), f32))
    else:
        wkv0 = (ctx[0], ctx[1])
        ssm0 = (ctx[2], ctx[3])

    def wkv_direction(i):
        logw = -WKV_DECAY_SCALE * jax.nn.sigmoid(p['w0'][i] + jnp.tanh(wds[i]) @ p['w2'][i])
        a = hsplit(jax.nn.sigmoid(p['a0'][i] + ads[i] @ p['a2'][i]))
        k_i = k * (1.0 + (a - 1.0) * k_a)
        return wkv_scan(r, jnp.exp(hsplit(logw)), kk, a, k_i, v, wkv0[i], reverse=(i == 1))

    y_f, S_f = wkv_direction(0)
    y_b, S_b = wkv_direction(1)
    y = head_layernorm(y_f + y_b, p['ln_w'].reshape(HC, NC), p['ln_b'].reshape(HC, NC), WKV_LN_EPS)
    y = y + jnp.sum(r * k * p['r_k'], axis=-1, keepdims=True) * v
    oc = y.reshape(b, L, HC * NC) * g
    zg, xbc, dt_raw = jnp.split(zd, [D_INNER, D_INNER + CONV_CH], axis=-1)
    xbc = jax.nn.silu(dwconv3(xbc, p['conv_w'], p['conv_b']))
    xs, Bs, Cs = jnp.split(xbc, [D_INNER, D_INNER + G_SSM * NS], axis=-1)
    xs = xs.reshape(b, L, HD, PD)
    Bs = Bs.reshape(b, L, G_SSM, NS)
    Cs = Cs.reshape(b, L, G_SSM, NS)

    def ssd_direction(i):
        dt = jax.nn.softplus(dt_raw + p['dt_bias'][i])
        A = -jnp.exp(p['A_log'][i].astype(f32))
        if i == 1:
            fl = lambda t: jnp.flip(t, axis=1)
            yy, S = ssd_scan(fl(xs), fl(dt), A, fl(Bs), fl(Cs), ssm0[i])
            yy = fl(yy)
        else:
            yy, S = ssd_scan(xs, dt, A, Bs, Cs, ssm0[i])
        return yy + p['D'][i][:, None] * xs, S

    yd_f, H_f = ssd_direction(0)
    yd_b, H_b = ssd_direction(1)
    od = rmsnorm((yd_f + yd_b).reshape(b, L, D_INNER) * jax.nn.silu(zg), p['gnorm'])
    merged = jnp.concatenate([oc, od], axis=-1)
    return merged @ p['w_out'], (S_f, S_b, H_f, H_b)


def conv_ffn(x, p):
    u = dwconv3(x @ p['ffn_up'], p['ffn_conv_w'], p['ffn_conv_b'])
    gate, val = jnp.split(u, 2, axis=-1)
    return (jax.nn.silu(gate) * val) @ p['ffn_down']


def trunk_layer(h, cond, p, mixer, ctx):
    sh1, sc1, g1, sh2, sc2, g2 = jnp.split(jax.nn.silu(cond) @ p['ada_w'] + p['ada_b'], 6, axis=-1)
    hn = rmsnorm(h, p['norm1']) * (1.0 + sc1) + sh1
    m, ctx_out = mixer(hn, p['mix'], ctx)
    h = h + g1 * m
    hn = rmsnorm(h, p['norm2']) * (1.0 + sc2) + sh2
    h = h + g2 * conv_ffn(hn, p)
    return h, ctx_out


def setup_inputs(seed: int = 0) -> dict:
    key = jax.random.key(seed)
    ks = iter(jax.random.split(key, 96))
    nrm = lambda shape, s=1.0: jax.random.normal(next(ks), shape, jnp.float32) * s
    gain = lambda n: 1.0 + 0.02 * jax.random.normal(next(ks), (n,), jnp.float32)
    inp = {}
    inp['x_prompt'] = nrm((BATCH, SEQ, D_MODEL))
    inp['x_sample'] = nrm((DEC_BATCH, DEC_SEQ, D_MODEL))
    inp['cache_l0_k'] = nrm((DEC_BATCH, HA, PAST_LEN, 2 * DA))
    inp['cache_l0_v'] = nrm((DEC_BATCH, HA, PAST_LEN, 2 * DA))
    inp['cache_l0_ckv'] = nrm((DEC_BATCH, PAST_LEN, KV_LORA))
    inp['cache_l0_kpe'] = nrm((DEC_BATCH, PAST_LEN, ROPE_B))
    inp['state_l1_wkv_fwd'] = nrm((DEC_BATCH, HC, NC, NC), 0.5)
    inp['state_l1_wkv_bwd'] = nrm((DEC_BATCH, HC, NC, NC), 0.5)
    inp['state_l1_ssm_fwd'] = nrm((DEC_BATCH, HD, PD, NS), 0.5)
    inp['state_l1_ssm_bwd'] = nrm((DEC_BATCH, HD, PD, NS), 0.5)
    inp['c'] = nrm((DEC_BATCH, D_MODEL))
    inp['c_ctx'] = nrm((D_MODEL,))

    def common(i):
        inp['ada_w_%d' % i] = nrm((D_MODEL, 6 * D_MODEL), D_MODEL ** -0.5)
        inp['ada_b_%d' % i] = nrm((6 * D_MODEL,), 0.02)
        inp['norm1_%d' % i] = gain(D_MODEL)
        inp['norm2_%d' % i] = gain(D_MODEL)
        inp['ffn_up_%d' % i] = nrm((D_MODEL, 2 * F_FF), D_MODEL ** -0.5)
        inp['ffn_conv_w_%d' % i] = nrm((3, 2 * F_FF), 3 ** -0.5)
        inp['ffn_conv_b_%d' % i] = nrm((2 * F_FF,), 0.02)
        inp['ffn_down_%d' % i] = nrm((F_FF, D_MODEL), F_FF ** -0.5)

    common(0)
    inp['l0_w_in'] = nrm((D_MODEL, A_COLS + B_COLS), D_MODEL ** -0.5)
    inp['l0_lambda_q1'] = nrm((DA,), 0.1)
    inp['l0_lambda_k1'] = nrm((DA,), 0.1)
    inp['l0_lambda_q2'] = nrm((DA,), 0.1)
    inp['l0_lambda_k2'] = nrm((DA,), 0.1)
    inp['l0_subln'] = gain(2 * DA)
    inp['l0_q_norm'] = gain(Q_LORA)
    inp['l0_w_uq'] = nrm((Q_LORA, HB * (NOPE + ROPE_B)), Q_LORA ** -0.5)
    inp['l0_kv_norm'] = gain(KV_LORA)
    inp['l0_w_ukv'] = nrm((KV_LORA, HB * (NOPE + VB)), KV_LORA ** -0.5)
    inp['l0_w_out'] = nrm((MIX_WIDTH, D_MODEL), MIX_WIDTH ** -0.5)
    common(1)
    inp['l1_w_in'] = nrm((D_MODEL, C_COLS + D_COLS), D_MODEL ** -0.5)
    inp['l1_mu'] = jax.random.uniform(next(ks), (C_COLS,), jnp.float32)
    inp['l1_w0'] = nrm((2, HC * NC), 0.5)
    inp['l1_w2'] = nrm((2, W_LORA, HC * NC), W_LORA ** -0.5)
    inp['l1_a0'] = nrm((2, HC * NC), 0.5)
    inp['l1_a2'] = nrm((2, A_LORA, HC * NC), A_LORA ** -0.5)
    inp['l1_g2'] = nrm((G_LORA, HC * NC), G_LORA ** -0.5)
    inp['l1_k_k'] = 0.85 + nrm((HC * NC,), 0.05)
    inp['l1_k_a'] = 1.0 + nrm((HC * NC,), 0.05)
    inp['l1_r_k'] = nrm((HC, NC), 0.1)
    inp['l1_ln_w'] = gain(HC * NC)
    inp['l1_ln_b'] = nrm((HC * NC,), 0.02)
    inp['l1_conv_w'] = nrm((3, CONV_CH), 3 ** -0.5)
    inp['l1_conv_b'] = nrm((CONV_CH,), 0.02)
    inp['l1_A_log'] = jnp.log(jax.random.uniform(next(ks), (2, HD), jnp.float32, 1.0, 16.0))
    dt0 = jnp.exp(jax.random.uniform(next(ks), (2, HD), jnp.float32, math.log(1e-3), math.log(1e-1)))
    inp['l1_dt_bias'] = dt0 + jnp.log(-jnp.expm1(-dt0))
    inp['l1_D'] = 1.0 + nrm((2, HD), 0.1)
    inp['l1_gnorm'] = gain(D_INNER)
    inp['l1_w_out'] = nrm((MIX_WIDTH, D_MODEL), MIX_WIDTH ** -0.5)
    inp['norm_f'] = gain(D_MODEL)
    return inp


def reference(x_prompt, x_sample, cache_l0_k, cache_l0_v, cache_l0_ckv, cache_l0_kpe,
              state_l1_wkv_fwd, state_l1_wkv_bwd, state_l1_ssm_fwd, state_l1_ssm_bwd,
              c, c_ctx,
              ada_w_0, ada_b_0, norm1_0, norm2_0, ffn_up_0, ffn_conv_w_0, ffn_conv_b_0, ffn_down_0,
              l0_w_in, l0_lambda_q1, l0_lambda_k1, l0_lambda_q2, l0_lambda_k2, l0_subln,
              l0_q_norm, l0_w_uq, l0_kv_norm, l0_w_ukv, l0_w_out,
              ada_w_1, ada_b_1, norm1_1, norm2_1, ffn_up_1, ffn_conv_w_1, ffn_conv_b_1, ffn_down_1,
              l1_w_in, l1_mu, l1_w0, l1_w2, l1_a0, l1_a2, l1_g2, l1_k_k, l1_k_a, l1_r_k,
              l1_ln_w, l1_ln_b, l1_conv_w, l1_conv_b, l1_A_log, l1_dt_bias, l1_D, l1_gnorm, l1_w_out,
              norm_f):
    layers = [
        dict(ada_w=ada_w_0, ada_b=ada_b_0, norm1=norm1_0, norm2=norm2_0, ffn_up=ffn_up_0,
             ffn_conv_w=ffn_conv_w_0, ffn_conv_b=ffn_conv_b_0, ffn_down=ffn_down_0,
             mix=dict(w_in=l0_w_in, lambda_q1=l0_lambda_q1, lambda_k1=l0_lambda_k1,
                      lambda_q2=l0_lambda_q2, lambda_k2=l0_lambda_k2, subln=l0_subln,
                      q_norm=l0_q_norm, w_uq=l0_w_uq, kv_norm=l0_kv_norm, w_ukv=l0_w_ukv, w_out=l0_w_out)),
        dict(ada_w=ada_w_1, ada_b=ada_b_1, norm1=norm1_1, norm2=norm2_1, ffn_up=ffn_up_1,
             ffn_conv_w=ffn_conv_w_1, ffn_conv_b=ffn_conv_b_1, ffn_down=ffn_down_1,
             mix=dict(w_in=l1_w_in, mu=l1_mu, w0=l1_w0, w2=l1_w2, a0=l1_a0, a2=l1_a2, g2=l1_g2,
                      k_k=l1_k_k, k_a=l1_k_a, r_k=l1_r_k, ln_w=l1_ln_w, ln_b=l1_ln_b,
                      conv_w=l1_conv_w, conv_b=l1_conv_b, A_log=l1_A_log, dt_bias=l1_dt_bias,
                      D=l1_D, gnorm=l1_gnorm, w_out=l1_w_out)),
    ]
    caches = [(cache_l0_k, cache_l0_v, cache_l0_ckv, cache_l0_kpe),
              (state_l1_wkv_fwd, state_l1_wkv_bwd, state_l1_ssm_fwd, state_l1_ssm_bwd)]
    cond_ctx = c_ctx[None, None, :]
    cond_lat = c[:, None, :]
    h_ctx, h_lat = x_prompt, x_sample
    new_state = []
    for i in range(DEPTH):
        p = layers[i]
        mixer = functools.partial(even_mixer, layer_idx=i) if i % 2 == 0 else odd_mixer
        h_ctx, ctx_tensors = trunk_layer(h_ctx, cond_ctx, p, mixer, None)
        h_lat, _ = trunk_layer(h_lat, cond_lat, p, mixer, caches[i])
        new_state.append(ctx_tensors)
    y_prompt = rmsnorm(h_ctx, norm_f)
    y_sample = rmsnorm(h_lat, norm_f)
    new_l0_k, new_l0_v, new_l0_ckv, new_l0_kpe = new_state[0]
    new_l1_wkv_fwd, new_l1_wkv_bwd, new_l1_ssm_fwd, new_l1_ssm_bwd = new_state[1]
    return (y_prompt, y_sample, new_l0_k, new_l0_v, new_l0_ckv, new_l0_kpe,
            new_l1_wkv_fwd, new_l1_wkv_bwd, new_l1_ssm_fwd, new_l1_ssm_bwd)
```

```python
import functools
import math

import jax
import jax.numpy as jnp
from jax import lax
from jax.experimental import pallas as pl
from jax.experimental.pallas import tpu as pltpu

F32 = jnp.float32
BF16 = jnp.bfloat16
HIGHEST = lax.Precision.HIGHEST

D_MODEL = 1024
GRID_W = 64
ROPE_BASE = 10000.0
NORM_EPS = 1e-6
HA, DA = 4, 64
HB, NOPE, ROPE_B, VB = 4, 64, 32, 128
Q_LORA, KV_LORA = 192, 128
HC, NC = 8, 64
WKV_LN_EPS = 64e-5
WKV_DECAY_SCALE = 0.606531
HD, PD, G_SSM, NS = 8, 64, 2, 64
D_INNER = HD * PD
F_FF = 2816
LANE = 128
HALO = 8
WKV_CHUNK = 64
SSD_CHUNK = 128
VMEM_LIMIT = 56 * 1024 * 1024


def _cparams(sem):
    return pltpu.CompilerParams(dimension_semantics=sem, vmem_limit_bytes=VMEM_LIMIT)


def _mm(a, b):
    return jnp.dot(a.astype(BF16), b.astype(BF16), preferred_element_type=F32)


def _mm_nt(a, b):
    return lax.dot_general(a.astype(BF16), b.astype(BF16), (((1,), (1,)), ((), ())),
                           preferred_element_type=F32)


def _mm_tn(a, b):
    return lax.dot_general(a.astype(BF16), b.astype(BF16), (((0,), (0,)), ((), ())),
                           preferred_element_type=F32)


def _mm_f32(a, b):
    return jnp.dot(a, b, precision=HIGHEST, preferred_element_type=F32)


def _mm_split(x, m):
    hi = x.astype(BF16)
    lo = (x - hi.astype(F32)).astype(BF16)
    return (jnp.dot(hi, m, preferred_element_type=F32) + jnp.dot(lo, m, preferred_element_type=F32))


def _rms(x, g, n):
    ms = jnp.sum(x * x, axis=-1, keepdims=True) * (1.0 / n)
    return x * lax.rsqrt(ms + NORM_EPS) * g


def _silu(x):
    return x * jax.nn.sigmoid(x)


def _lane(shape):
    return lax.broadcasted_iota(jnp.int32, shape, len(shape) - 1)


def _row(shape):
    return lax.broadcasted_iota(jnp.int32, shape, len(shape) - 2)


def _const_spec(shape):
    nd = len(shape)
    return pl.BlockSpec(shape, lambda *_: (0,) * nd)


def _ada_kernel(c_ref, w_ref, b_ref, o_ref):
    o_ref[...] = _mm(_silu(c_ref[...]), w_ref[...]) + b_ref[...]


def _ada(cond8, w, b):
    n = w.shape[1]
    tn = 1536
    return pl.pallas_call(
        _ada_kernel,
        grid=(n // tn,),
        in_specs=[_const_spec((8, D_MODEL)),
                  pl.BlockSpec((D_MODEL, tn), lambda j: (0, j)),
                  pl.BlockSpec((1, tn), lambda j: (0, j))],
        out_specs=pl.BlockSpec((8, tn), lambda j: (0, j)),
        out_shape=jax.ShapeDtypeStruct((8, n), F32),
        compiler_params=_cparams(("arbitrary",)),
        name="ada",
    )(cond8, w, b.reshape(1, n))


def _rope(x, c, s, half):
    w = x.shape[-1]
    up = pltpu.roll(x, w - half, 1)
    dn = pltpu.roll(x, half, 1)
    first = (_lane(x.shape) & (2 * half - 1)) < half
    return x * c + jnp.where(first, up, dn) * s


def _l0_proj_kernel(*refs, latent):
    if latent:
        (x_ref, mod_ref, n1_ref, w_ref, qn_ref, wuq_ref, kvn_ref, ca_ref, sa_ref, cb_ref, sb_ref,
         qa_o, ka_o, va_o, qb_o, ckv_o, kpe_o) = refs
    else:
        (x_ref, mod_ref, n1_ref, w_ref, qn_ref, wuq_ref, kvn_ref,
         qa_o, ka_o, va_o, qb_o, ckv_o, kpe_o) = refs
    mod = mod_ref[0]
    hn = _rms(x_ref[0], n1_ref[...], D_MODEL) * (1.0 + mod[1:2]) + mod[0:1]
    proj = _mm(hn, w_ref[...])
    qd = _rms(proj[:, 1536:1792], qn_ref[...], Q_LORA)
    qb = _mm(qd, wuq_ref[...])
    ckv_o[0] = _rms(proj[:, 1792:1920], kvn_ref[...], KV_LORA)
    kpe = proj[:, 1920:2048]
    if latent:
        ca, sa, cb, sb = ca_ref[...], sa_ref[...], cb_ref[...], sb_ref[...]
        kpe = _rope(kpe, cb, sb, ROPE_B // 4)
    kpe_o[0] = kpe
    for h in range(HA):
        sl = slice(h * LANE, (h + 1) * LANE)
        q = proj[:, sl]
        k = proj[:, 512 + h * LANE:512 + (h + 1) * LANE]
        qbh = qb[:, sl]
        if latent:
            q = _rope(q, ca, sa, DA // 4)
            k = _rope(k, ca, sa, DA // 4)
            qbh = _rope(qbh, cb, sb, ROPE_B // 4)
        qa_o[0, h] = (q * (DA ** -0.5)).astype(qa_o.dtype)
        ka_o[0, h] = k.astype(ka_o.dtype)
        va_o[0, h] = proj[:, 1024 + h * LANE:1024 + (h + 1) * LANE].astype(va_o.dtype)
        qb_o[0, h] = (qbh * ((NOPE + ROPE_B) ** -0.5)).astype(qb_o.dtype)


def _l0_proj(x, mod, per_batch, n1, w, qn, wuq, kvn, tables, tm):
    b, l, _ = x.shape
    latent = tables is not None
    kv_dtype = BF16 if latent else F32
    mod_map = (lambda bi, i: (bi, 0, 0)) if per_batch else (lambda bi, i: (0, 0, 0))
    in_specs = [pl.BlockSpec((1, tm, D_MODEL), lambda bi, i: (bi, i, 0)),
                pl.BlockSpec((1, 6, D_MODEL), mod_map),
                _const_spec((1, D_MODEL)), _const_spec(w.shape), _const_spec(qn.shape),
                _const_spec(wuq.shape), _const_spec(kvn.shape)]
    args = [x, mod, n1, w, qn, wuq, kvn]
    if latent:
        in_specs += [pl.BlockSpec((tm, LANE), lambda bi, i: (i, 0))] * 4
        args += list(tables)
    head_spec = pl.BlockSpec((1, HA, tm, LANE), lambda bi, i: (bi, 0, i, 0))
    row_spec = pl.BlockSpec((1, tm, LANE), lambda bi, i: (bi, i, 0))
    return pl.pallas_call(
        functools.partial(_l0_proj_kernel, latent=latent),
        grid=(b, l // tm),
        in_specs=in_specs,
        out_specs=[head_spec, head_spec, head_spec, head_spec, row_spec, row_spec],
        out_shape=[jax.ShapeDtypeStruct((b, HA, l, LANE), BF16),
                   jax.ShapeDtypeStruct((b, HA, l, LANE), kv_dtype),
                   jax.ShapeDtypeStruct((b, HA, l, LANE), kv_dtype),
                   jax.ShapeDtypeStruct((b, HB, l, LANE), BF16),
                   jax.ShapeDtypeStruct((b, l, LANE), F32),
                   jax.ShapeDtypeStruct((b, l, LANE), F32)],
        compiler_params=_cparams(("parallel", "parallel")),
        name="l0_proj",
    )(*args)


def _kvup_kernel(ckv_ref, kpe_ref, w_ref, kb_o, vb_o):
    kv = _mm(ckv_ref[0], w_ref[...])
    kpe = kpe_ref[0]
    for h in range(HB):
        kb_o[0, h] = (kv[:, h * LANE:(h + 1) * LANE] + kpe).astype(BF16)
        vb_o[0, h] = kv[:, 512 + h * LANE:512 + (h + 1) * LANE].astype(BF16)


def _kvup(ckv, kpe, w, tk):
    b, lk, _ = ckv.shape
    row_spec = pl.BlockSpec((1, tk, LANE), lambda bi, i: (bi, i, 0))
    head_spec = pl.BlockSpec((1, HB, tk, LANE), lambda bi, i: (bi, 0, i, 0))
    return pl.pallas_call(
        _kvup_kernel,
        grid=(b, lk // tk),
        in_specs=[row_spec, row_spec, _const_spec(w.shape)],
        out_specs=[head_spec, head_spec],
        out_shape=[jax.ShapeDtypeStruct((b, HB, lk, LANE), BF16)] * 2,
        compiler_params=_cparams(("parallel", "parallel")),
        name="kvup",
    )(ckv, kpe, w)


def _attn_kernel(*refs, diff, lambda_init):
    if diff:
        q_ref, k_ref, v_ref, lam_ref, g_ref, o_ref = refs
    else:
        q_ref, k_ref, v_ref, o_ref = refs
    q = q_ref[0, 0]
    k = k_ref[0, 0]
    v = v_ref[0, 0]
    tq = q.shape[0]
    if diff:
        lo = _lane(q.shape) < DA
        zero = jnp.zeros_like(q)
        qq = jnp.concatenate([jnp.where(lo, q, zero), jnp.where(lo, zero, q)], axis=0)
    else:
        qq = q
    s = _mm_nt(qq, k)
    p = jnp.exp(s - jnp.max(s, axis=-1, keepdims=True))
    inv = 1.0 / jnp.sum(p, axis=-1, keepdims=True)
    if diff:
        lv = lam_ref[...]
        lam = (jnp.exp(jnp.sum(lv[0:1] * lv[1:2], axis=-1, keepdims=True))
               - jnp.exp(jnp.sum(lv[2:3] * lv[3:4], axis=-1, keepdims=True)) + lambda_init)
        pd = p[:tq] * inv[:tq] - p[tq:] * (inv[tq:] * lam)
        o = _mm(pd, v)
        o = _rms(o, g_ref[...], 2 * DA) * (1.0 - lambda_init)
    else:
        o = _mm(p * inv, v)
    o_ref[0] = o.astype(o_ref.dtype)


def _attention(q, k, v, tq, diff, lam_vecs=None, subln=None, lambda_init=0.0):
    b, h, l, _ = q.shape
    lk = k.shape[2]
    in_specs = [pl.BlockSpec((1, 1, tq, LANE), lambda bi, hi, i: (bi, hi, i, 0)),
                pl.BlockSpec((1, 1, lk, LANE), lambda bi, hi, i: (bi, hi, 0, 0)),
                pl.BlockSpec((1, 1, lk, LANE), lambda bi, hi, i: (bi, hi, 0, 0))]
    args = [q, k, v]
    if diff:
        in_specs += [_const_spec(lam_vecs.shape), _const_spec(subln.shape)]
        args += [lam_vecs, subln]
    return pl.pallas_call(
        functools.partial(_attn_kernel, diff=diff, lambda_init=lambda_init),
        grid=(b, h, l // tq),
        in_specs=in_specs,
        out_specs=pl.BlockSpec((1, tq, LANE), lambda bi, hi, i: (bi, i, hi)),
        out_shape=jax.ShapeDtypeStruct((b, l, h * LANE), BF16),
        compiler_params=_cparams(("parallel", "parallel", "parallel")),
        name="attn_diff" if diff else "attn_mla",
    )(*args)


def _oproj_kernel(h_ref, mod_ref, a_ref, b_ref, w_ref, o_ref):
    half = a_ref.shape[-1]
    m = _mm(a_ref[0], w_ref[0:half, :]) + _mm(b_ref[0], w_ref[half:, :])
    o_ref[0] = h_ref[0] + mod_ref[0][2:3] * m


def _oproj(h, mod, per_batch, oa, ob, w, tm):
    b, l, _ = h.shape
    mod_map = (lambda bi, i: (bi, 0, 0)) if per_batch else (lambda bi, i: (0, 0, 0))
    x_spec = pl.BlockSpec((1, tm, D_MODEL), lambda bi, i: (bi, i, 0))
    half_spec = pl.BlockSpec((1, tm, oa.shape[-1]), lambda bi, i: (bi, i, 0))
    return pl.pallas_call(
        _oproj_kernel,
        grid=(b, l // tm),
        in_specs=[x_spec, pl.BlockSpec((1, 6, D_MODEL), mod_map), half_spec, half_spec,
                  _const_spec(w.shape)],
        out_specs=x_spec,
        out_shape=jax.ShapeDtypeStruct(h.shape, F32),
        compiler_params=_cparams(("parallel", "parallel")),
        name="oproj",
    )(h, mod, oa, ob, w)


def _halo_rows(x_ref, xp_ref, xn_ref):
    i = pl.program_id(1)
    last = pl.num_programs(1) - 1
    xc = jnp.concatenate([xp_ref[0], x_ref[0], xn_ref[0]], axis=0)
    tm = x_ref.shape[1]
    r = _row((tm + 2 * HALO, 1))
    valid = jnp.logical_and(jnp.logical_or(r >= HALO, i > 0),
                            jnp.logical_or(r < tm + HALO, i < last))
    return xc, valid


def _shift_rows(u, tm):
    n = u.shape[0]
    up = pltpu.roll(u, 1, 0)[HALO:HALO + tm]
    dn = pltpu.roll(u, n - 1, 0)[HALO:HALO + tm]
    return up, u[HALO:HALO + tm], dn


def _ffn_kernel(x_ref, xp_ref, xn_ref, mod_ref, n2_ref, wu_ref, cw_ref, cb_ref, wd_ref, *rest,
                final, cw):
    if final:
        nf_ref, o_ref = rest
    else:
        (o_ref,) = rest
    tm = x_ref.shape[1]
    mod = mod_ref[0]
    xc, valid = _halo_rows(x_ref, xp_ref, xn_ref)
    hn = _rms(xc, n2_ref[...], D_MODEL) * (1.0 + mod[4:5]) + mod[3:4]
    hn = jnp.where(valid, hn, 0.0).astype(BF16)
    acc = jnp.zeros((tm, D_MODEL), F32)
    for c in range(F_FF // cw):
        halves = []
        for off in (c * cw, F_FF + c * cw):
            u = jnp.dot(hn, wu_ref[:, off:off + cw], preferred_element_type=F32)
            up, mid, dn = _shift_rows(u, tm)
            w3 = cw_ref[:, off:off + cw]
            halves.append(up * w3[0:1] + mid * w3[1:2] + dn * w3[2:3] + cb_ref[:, off:off + cw])
        act = (_silu(halves[0]) * halves[1]).astype(BF16)
        acc = acc + jnp.dot(act, wd_ref[c * cw:(c + 1) * cw, :], preferred_element_type=F32)
    out = x_ref[0] + mod[5:6] * acc
    if final:
        out = _rms(out, nf_ref[...], D_MODEL)
    o_ref[0] = out


def _ffn(h, mod, per_batch, n2, wu, cw3, cb, wd, nf, tm):
    b, l, _ = h.shape
    nblk = l // tm
    nb = tm // HALO
    mod_map = (lambda bi, i: (bi, 0, 0)) if per_batch else (lambda bi, i: (0, 0, 0))
    x_spec = pl.BlockSpec((1, tm, D_MODEL), lambda bi, i: (bi, i, 0))
    in_specs = [x_spec,
                pl.BlockSpec((1, HALO, D_MODEL), lambda bi, i: (bi, jnp.maximum(i * nb - 1, 0), 0)),
                pl.BlockSpec((1, HALO, D_MODEL),
                             lambda bi, i: (bi, jnp.minimum((i + 1) * nb, nblk * nb - 1), 0)),
                pl.BlockSpec((1, 6, D_MODEL), mod_map),
                _const_spec(n2.shape), _const_spec(wu.shape), _const_spec(cw3.shape),
                _const_spec(cb.shape), _const_spec(wd.shape)]
    args = [h, h, h, mod, n2, wu, cw3, cb, wd]
    final = nf is not None
    if final:
        in_specs.append(_const_spec(nf.shape))
        args.append(nf)
    return pl.pallas_call(
        functools.partial(_ffn_kernel, final=final, cw=256),
        grid=(b, nblk),
        in_specs=in_specs,
        out_specs=x_spec,
        out_shape=jax.ShapeDtypeStruct(h.shape, F32),
        compiler_params=_cparams(("parallel", "parallel")),
        name="ffn",
    )(*args)


def _softplus(x):
    return jnp.maximum(x, 0.0) + jnp.log1p(jnp.exp(-jnp.abs(x)))


def _l1_proj_kernel(x_ref, xp_ref, xn_ref, mod_ref, n1_ref, w_ref, mu_ref, kk_ref, ka_ref,
                    w0_ref, w2_ref, a0_ref, a2_ref, g2_ref, cw_ref, cb_ref, dtb_ref, bd_ref,
                    r_o, k_o, v_o, kk_o, g_o, lw0_o, ai0_o, kx0_o, lw1_o, ai1_o, kx1_o,
                    zg_o, xs_o, bc_o, dt_o):
    tm = x_ref.shape[1]
    mod = mod_ref[0]
    xc, valid = _halo_rows(x_ref, xp_ref, xn_ref)
    hn = _rms(xc, n1_ref[...], D_MODEL) * (1.0 + mod[1:2]) + mod[0:1]
    hn = jnp.where(valid, hn, 0.0).astype(BF16)
    zc = jnp.dot(hn, w_ref[:, 0:1920], preferred_element_type=F32)
    up, mid, dn = _shift_rows(zc, tm)
    zc = mid + mu_ref[...] * (0.5 * (up + dn) - mid)
    r, k, v = zc[:, 0:512], zc[:, 512:1024], zc[:, 1024:1536]
    wd, ad, gd = zc[:, 1536:1664], zc[:, 1664:1792], zc[:, 1792:1920]
    r_o[0], k_o[0], v_o[0] = r, k, v
    kkf = k * kk_ref[...]
    kk_o[0] = kkf * lax.rsqrt(_mm_split(kkf * kkf, bd_ref[...]) + 1e-12)
    g_o[0] = _mm(jax.nn.sigmoid(gd), g2_ref[...])
    twd = jnp.tanh(wd)
    for i, (lw_o, ai_o, kx_o) in enumerate(((lw0_o, ai0_o, kx0_o), (lw1_o, ai1_o, kx1_o))):
        lw_o[0] = -WKV_DECAY_SCALE * jax.nn.sigmoid(w0_ref[i:i + 1] + _mm(twd, w2_ref[i]))
        a = jax.nn.sigmoid(a0_ref[i:i + 1] + _mm(ad, a2_ref[i]))
        ai_o[0] = a
        kx_o[0] = k * (1.0 + (a - 1.0) * ka_ref[...])
    zd = jnp.dot(hn, w_ref[:, 1920:3328], preferred_element_type=F32)
    zg_o[0] = zd[HALO:HALO + tm, 0:512]
    xbc = zd[:, 512:1280]
    up, mid, dn = _shift_rows(xbc, tm)
    cw3 = cw_ref[...]
    xbc = _silu(up * cw3[0:1] + mid * cw3[1:2] + dn * cw3[2:3] + cb_ref[...])
    xs_o[0] = xbc[:, 0:512]
    bc_o[0] = xbc[:, 512:768]
    dt_o[0] = _softplus(zd[HALO:HALO + tm, 1280:1408] + dtb_ref[...])


def _l1_proj(x, mod, per_batch, p, tm):
    b, l, _ = x.shape
    nblk = l // tm
    nb = tm // HALO
    mod_map = (lambda bi, i: (bi, 0, 0)) if per_batch else (lambda bi, i: (0, 0, 0))
    x_spec = pl.BlockSpec((1, tm, D_MODEL), lambda bi, i: (bi, i, 0))
    consts = [p["norm1"], p["w_in"], p["mu"], p["k_k"], p["k_a"], p["w0"], p["w2"], p["a0"],
              p["a2"], p["g2"], p["conv_w"], p["conv_b"], p["dt_bias"], p["bd"]]
    in_specs = [x_spec,
                pl.BlockSpec((1, HALO, D_MODEL), lambda bi, i: (bi, jnp.maximum(i * nb - 1, 0), 0)),
                pl.BlockSpec((1, HALO, D_MODEL),
                             lambda bi, i: (bi, jnp.minimum((i + 1) * nb, nblk * nb - 1), 0)),
                pl.BlockSpec((1, 6, D_MODEL), mod_map)] + [_const_spec(c.shape) for c in consts]

    def o_spec(w):
        return pl.BlockSpec((1, tm, w), lambda bi, i: (bi, i, 0))

    widths = [512] * 11 + [512, 512, 256, 128]
    return pl.pallas_call(
        _l1_proj_kernel,
        grid=(b, nblk),
        in_specs=in_specs,
        out_specs=[o_spec(w) for w in widths],
        out_shape=[jax.ShapeDtypeStruct((b, l, w), F32) for w in widths],
        compiler_params=_cparams(("parallel", "parallel")),
        name="l1_proj",
    )(x, x, x, mod, *consts)


def _tri(n, upper, strict=False):
    r, c = _row((n, n)), _lane((n, n))
    if upper:
        return (c > r) if strict else (c >= r)
    return (c < r) if strict else (c <= r)


def _wkv_chunk(r, kk, v, lw, a, kx, st_ref, d, bwd):
    c = r.shape[0]
    cum = _mm_f32(_tri(c, bwd).astype(F32), lw)
    last = cum[0:1] if bwd else cum[c - 1:c]
    e_in, e_out = jnp.exp(cum - lw), jnp.exp(cum)
    e_neg, e_end = jnp.exp(-cum), jnp.exp(last - cum)
    at, rt = -kk * e_in, r * e_out
    bb = kk * a
    bt, kt = bb * e_neg, kx * e_neg
    bh, kh = bb * e_end, kx * e_end
    ones = jnp.ones((c, LANE), F32)
    lane2 = _lane((2 * c, LANE))
    row2 = _row((2 * c, LANE))
    lo2 = lane2 < NC
    top2 = row2 < c
    bd2 = lo2 == top2
    tmask = row2 - jnp.where(top2, 0, c)
    imask = lane2 - jnp.where(lo2, 0, NC)
    strict = (imask > tmask) if bwd else (imask < tmask)
    incl = (imask >= tmask) if bwd else (imask <= tmask)
    blk16 = jnp.right_shift(tmask, 4) == jnp.right_shift(imask, 4)
    eye = jnp.logical_and(bd2, tmask == imask).astype(F32)
    lo1 = _lane((c, LANE)) < NC
    ys = []
    for p in range(HC // 2):
        sl = slice(p * LANE, (p + 1) * LANE)
        st = st_ref[d, p]
        l2 = jnp.concatenate([at[:, sl], rt[:, sl]], axis=0)
        base = _mm(l2, st)
        zero = jnp.zeros_like(l2)
        g_a = _mm_nt(jnp.where(lo2, l2, zero), jnp.concatenate([bt[:, sl], kt[:, sl]], axis=0))
        g_b = _mm_nt(jnp.where(lo2, zero, l2), jnp.concatenate([kt[:, sl], bt[:, sl]], axis=0))
        ga = jnp.where(strict, jnp.concatenate([g_a[:c], g_b[:c]], axis=0), 0.0)
        gr_a = jnp.where(incl[:c], g_a[c:], 0.0)
        gr_b = jnp.where(incl[c:], g_b[c:], 0.0)
        n_p = jnp.where(bd2, ga, 0.0)
        ak = jnp.where(bd2, 0.0, ga)
        nd = jnp.where(blk16, n_p, 0.0)
        no = n_p - nd
        n2 = _mm(nd, nd)
        n4 = _mm(n2, n2)
        n8 = _mm(n4, n4)
        dinv = eye + nd
        dinv = dinv + _mm(dinv, n2)
        dinv = dinv + _mm(dinv, n4)
        dinv = dinv + _mm(dinv, n8)
        m1 = _mm(dinv, no)
        m2 = _mm(m1, m1)
        t_p = eye + m1
        t_p = t_p + _mm(t_p, m2)
        t_p = _mm(t_p, dinv)
        vp = v[:, sl]
        rhs = jnp.concatenate([base[:c], base[:c]], axis=0) + _mm(ak, jnp.concatenate([vp, vp], axis=0))
        us = _mm(t_p, rhs)
        u = jnp.where(lo1, us[:c], us[c:])
        uv = jnp.concatenate([u, vp], axis=0)
        vu = jnp.concatenate([vp, u], axis=0)
        ys.append(base[c:] + jnp.where(lo1, _mm(gr_a, uv), _mm(gr_b, vu)))
        upd = _mm_tn(jnp.concatenate([bh[:, sl], kh[:, sl]], axis=0), uv)
        wc = jnp.exp(lax.dot_general(lw[:, sl], ones, (((0,), (0,)), ((), ())),
                                     precision=HIGHEST, preferred_element_type=F32))
        st_ref[d, p] = st * wc + jnp.where(bd2, upd, 0.0)
    return jnp.concatenate(ys, axis=1)


def _wkv_kernel(rf, kkf, vf, lw0, a0, kx0, rb, kkb, vb, lw1, a1, kx1, s0_ref,
                yf_o, yb_o, sfin_o, st_ref):
    j = pl.program_id(1)

    @pl.when(j == 0)
    def _():
        st_ref[...] = s0_ref[0]

    yf_o[0] = _wkv_chunk(rf[0], kkf[0], vf[0], lw0[0], a0[0], kx0[0], st_ref, 0, False)
    yb_o[0] = _wkv_chunk(rb[0], kkb[0], vb[0], lw1[0], a1[0], kx1[0], st_ref, 1, True)

    @pl.when(j == pl.num_programs(1) - 1)
    def _():
        sfin_o[0] = st_ref[...]


def _wkv(r, kk, v, lw0, a0, kx0, lw1, a1, kx1, s0):
    b, l, w = r.shape
    c = WKV_CHUNK
    nc = l // c
    fwd = pl.BlockSpec((1, c, w), lambda bi, j: (bi, j, 0))
    bwd = pl.BlockSpec((1, c, w), lambda bi, j: (bi, nc - 1 - j, 0))
    st_spec = pl.BlockSpec((1,) + s0.shape[1:], lambda bi, j: (bi, 0, 0, 0, 0))
    return pl.pallas_call(
        _wkv_kernel,
        grid=(b, nc),
        in_specs=[fwd] * 6 + [bwd] * 6 + [st_spec],
        out_specs=[fwd, bwd, st_spec],
        out_shape=[jax.ShapeDtypeStruct((b, l, w), F32), jax.ShapeDtypeStruct((b, l, w), F32),
                   jax.ShapeDtypeStruct(s0.shape, F32)],
        scratch_shapes=[pltpu.VMEM(s0.shape[1:], F32)],
        compiler_params=_cparams(("parallel", "arbitrary")),
        name="wkv",
    )(r, kk, v, lw0, a0, kx0, r, kk, v, lw1, a1, kx1, s0)


def _ssd_chunk(xs, bc, dt, arow, st_ref, d, bwd):
    c = xs.shape[0]
    o = HD * d
    acum = _mm_f32(_tri(c, bwd).astype(F32), dt * arow)
    acum_t = acum.T
    last = acum[0:1] if bwd else acum[c - 1:c]
    ea = jnp.exp(acum)
    de = jnp.exp(last - acum)
    cd = jnp.exp(last)
    bfull, cfull = bc[:, 0:LANE], bc[:, LANE:2 * LANE]
    lane = _lane((c, LANE))
    lo = lane < PD
    causal = _tri(c, bwd)
    zero = jnp.zeros_like(cfull)
    cb = [_mm_nt(jnp.where(lo, cfull, zero), bfull), _mm_nt(jnp.where(lo, zero, cfull), bfull)]
    grp_rows = _row((c, LANE)) < NS

    def colsel(m, ha, hb):
        return jnp.where(lo, jnp.broadcast_to(m[:, o + ha:o + ha + 1], (c, LANE)),
                         jnp.broadcast_to(m[:, o + hb:o + hb + 1], (c, LANE)))

    ys = []
    for p in range(HD // 2):
        g = p // 2
        ha, hb = 2 * p, 2 * p + 1
        xdt = xs[:, p * LANE:(p + 1) * LANE] * colsel(dt, ha, hb)
        yi = []
        for h in (ha, hb):
            seg = jnp.broadcast_to(acum[:, o + h:o + h + 1], (c, c)) - jnp.broadcast_to(acum_t[o + h:o + h + 1, :], (c, c))
            dec = jnp.where(causal, jnp.exp(jnp.where(causal, seg, 0.0)), 0.0)
            yi.append(_mm(cb[g] * dec, xdt))
        st = st_ref[d, p]
        y = jnp.where(lo, yi[0], yi[1]) + _mm(cfull, st) * colsel(ea, ha, hb)
        ys.append(y)
        cs = _mm_tn(bfull, xdt * colsel(de, ha, hb))
        cs = jnp.where(grp_rows if g == 0 else jnp.logical_not(grp_rows), cs, 0.0)
        cdp = jnp.where(lo[0:1], jnp.broadcast_to(cd[:, o + ha:o + ha + 1], (1, LANE)),
                        jnp.broadcast_to(cd[:, o + hb:o + hb + 1], (1, LANE)))
        st_ref[d, p] = st * cdp + cs
    return jnp.concatenate(ys, axis=1)


def _ssd_kernel(xf, bcf, dtf, xb, bcb, dtb, arow_ref, s0_ref, yf_o, yb_o, sfin_o, st_ref):
    j = pl.program_id(1)

    @pl.when(j == 0)
    def _():
        st_ref[...] = s0_ref[0]

    arow = arow_ref[...]
    yf_o[0] = _ssd_chunk(xf[0], bcf[0], dtf[0], arow, st_ref, 0, False)
    yb_o[0] = _ssd_chunk(xb[0], bcb[0], dtb[0], arow, st_ref, 1, True)

    @pl.when(j == pl.num_programs(1) - 1)
    def _():
        sfin_o[0] = st_ref[...]


def _ssd(xs, bc, dt, arow, s0):
    b, l, _ = xs.shape
    c = SSD_CHUNK
    nc = l // c

    def spec(w, rev):
        if rev:
            return pl.BlockSpec((1, c, w), lambda bi, j: (bi, nc - 1 - j, 0))
        return pl.BlockSpec((1, c, w), lambda bi, j: (bi, j, 0))

    st_spec = pl.BlockSpec((1,) + s0.shape[1:], lambda bi, j: (bi, 0, 0, 0, 0))
    return pl.pallas_call(
        _ssd_kernel,
        grid=(b, nc),
        in_specs=[spec(512, False), spec(256, False), spec(128, False),
                  spec(512, True), spec(256, True), spec(128, True),
                  _const_spec(arow.shape), st_spec],
        out_specs=[spec(512, False), spec(512, True), st_spec],
        out_shape=[jax.ShapeDtypeStruct((b, l, 512), F32), jax.ShapeDtypeStruct((b, l, 512), F32),
                   jax.ShapeDtypeStruct(s0.shape, F32)],
        scratch_shapes=[pltpu.VMEM(s0.shape[1:], F32)],
        compiler_params=_cparams(("parallel", "arbitrary")),
        name="ssd",
    )(xs, bc, dt, xs, bc, dt, arow, s0)


def _l1_post_kernel(h_ref, mod_ref, yf, yb, r, k, v, g, ydf, ydb, xs, zg, lnw, lnb, rk, dsum, gn,
                    w_ref, bd_ref, o_ref):
    bd = bd_ref[...]
    y = yf[0] + yb[0]
    mu = _mm_split(y, bd) * (1.0 / NC)
    dl = y - mu
    var = _mm_split(dl * dl, bd) * (1.0 / NC)
    y = dl * lax.rsqrt(var + WKV_LN_EPS) * lnw[...] + lnb[...]
    y = y + _mm_split(r[0] * k[0] * rk[...], bd) * v[0]
    oc = y * g[0]
    x = xs[0]
    yd = (ydf[0] + dsum[0:1] * x) + (ydb[0] + dsum[1:2] * x)
    od = _rms(yd * _silu(zg[0]), gn[...], D_INNER)
    m = _mm(oc, w_ref[0:512, :]) + _mm(od, w_ref[512:1024, :])
    o_ref[0] = h_ref[0] + mod_ref[0][2:3] * m


def _l1_post(h, mod, per_batch, acts, consts, tm):
    b, l, _ = h.shape
    mod_map = (lambda bi, i: (bi, 0, 0)) if per_batch else (lambda bi, i: (0, 0, 0))
    x_spec = pl.BlockSpec((1, tm, D_MODEL), lambda bi, i: (bi, i, 0))
    a_spec = pl.BlockSpec((1, tm, 512), lambda bi, i: (bi, i, 0))
    return pl.pallas_call(
        _l1_post_kernel,
        grid=(b, l // tm),
        in_specs=[x_spec, pl.BlockSpec((1, 6, D_MODEL), mod_map)] + [a_spec] * len(acts)
        + [_const_spec(c.shape) for c in consts],
        out_specs=x_spec,
        out_shape=jax.ShapeDtypeStruct(h.shape, F32),
        compiler_params=_cparams(("parallel", "parallel")),
        name="l1_post",
    )(h, mod, *acts, *consts)


def _rope_tables(length, dim, offset, width):
    quarter = dim // 4
    inv = ROPE_BASE ** (-jnp.arange(quarter, dtype=F32) / quarter)
    pos = jnp.arange(length)
    row = (pos // GRID_W).astype(F32)
    col = (pos % GRID_W).astype(F32)
    ar, ac = row[:, None] * inv, col[:, None] * inv
    c = jnp.concatenate([jnp.cos(ar), jnp.cos(ar), jnp.cos(ac), jnp.cos(ac)], axis=-1)
    s = jnp.concatenate([-jnp.sin(ar), jnp.sin(ar), -jnp.sin(ac), jnp.sin(ac)], axis=-1)
    cw = jnp.ones((length, width), F32).at[:, offset:offset + dim].set(c)
    sw = jnp.zeros((length, width), F32).at[:, offset:offset + dim].set(s)
    return cw, sw


def _pad_cols(w, n):
    return jnp.pad(w, ((0, 0), (0, n - w.shape[1])))


def _prep_l0(w_in, q_norm, w_uq, kv_norm, w_ukv, w_out):
    a_cols = 3 * HA * 2 * DA
    wqd = _pad_cols(w_in[:, a_cols:a_cols + Q_LORA], 256)
    wckv = w_in[:, a_cols + Q_LORA:a_cols + Q_LORA + KV_LORA]
    wkpe = jnp.pad(w_in[:, a_cols + Q_LORA + KV_LORA:], ((0, 0), (NOPE, LANE - NOPE - ROPE_B)))
    w = jnp.concatenate([w_in[:, :a_cols], wqd, wckv, wkpe], axis=1).astype(BF16)
    qn = _pad_cols(q_norm.reshape(1, -1), 256)
    wuq = jnp.pad(w_uq.reshape(Q_LORA, HB, NOPE + ROPE_B),
                  ((0, 256 - Q_LORA), (0, 0), (0, LANE - NOPE - ROPE_B))).reshape(256, HB * LANE)
    wukv = w_ukv.reshape(KV_LORA, HB, NOPE + VB)
    wk = jnp.pad(wukv[:, :, :NOPE], ((0, 0), (0, 0), (0, LANE - NOPE))).reshape(KV_LORA, HB * LANE)
    wv = wukv[:, :, NOPE:].reshape(KV_LORA, HB * VB)
    return dict(w=w, qn=qn, wuq=wuq.astype(BF16), kvn=kv_norm.reshape(1, -1),
                wukv=jnp.concatenate([wk, wv], axis=1).astype(BF16), w_out=w_out.astype(BF16))


def _prep_l1(w_in, mu, w0, w2, a0, a2, g2, k_k, k_a, r_k, ln_w, ln_b, conv_w, conv_b, A_log,
             dt_bias, D, gnorm, w_out, norm1):
    c_cols = 3 * HC * NC + 4 * 64 + 128
    conv_ch = D_INNER + 2 * G_SSM * NS
    wdt = w_in[:, c_cols + D_INNER + conv_ch:]
    wdt = _pad_cols(jnp.concatenate([wdt, wdt], axis=1), LANE)
    w = jnp.concatenate([w_in[:, :c_cols + D_INNER + conv_ch], wdt], axis=1).astype(BF16)
    z = jnp.zeros((64, HC * NC), F32)
    w2p = jnp.stack([jnp.concatenate([w2[0], z], 0), jnp.concatenate([z, w2[1]], 0)]).astype(BF16)
    a2p = jnp.stack([jnp.concatenate([a2[0], z], 0), jnp.concatenate([z, a2[1]], 0)]).astype(BF16)
    hid = jnp.arange(HC * NC) // NC
    bd = (hid[:, None] == hid[None, :]).astype(BF16)
    dtb = _pad_cols(dt_bias.reshape(1, 2 * HD), LANE)
    arow = _pad_cols((-jnp.exp(A_log.astype(F32))).reshape(1, 2 * HD), LANE)
    return dict(norm1=norm1.reshape(1, -1), w_in=w, mu=mu.reshape(1, -1), k_k=k_k.reshape(1, -1),
                k_a=k_a.reshape(1, -1), w0=w0, w2=w2p, a0=a0, a2=a2p, g2=g2.astype(BF16),
                conv_w=conv_w, conv_b=conv_b.reshape(1, -1), dt_bias=dtb, bd=bd, arow=arow,
                ln_w=ln_w.reshape(1, -1), ln_b=ln_b.reshape(1, -1), r_k=r_k.reshape(1, -1),
                dsum=jnp.repeat(D, PD, axis=1), gnorm=gnorm.reshape(1, -1),
                w_out=w_out.astype(BF16))


def _wkv_state_in(s):
    b = s.shape[0]
    st = jnp.swapaxes(s, -1, -2).reshape(b, HC // 2, 2, NC, NC)
    z = jnp.zeros_like(st[:, :, 0])
    top = jnp.concatenate([st[:, :, 0], z], axis=-1)
    bot = jnp.concatenate([z, st[:, :, 1]], axis=-1)
    return jnp.concatenate([top, bot], axis=-2)


def _wkv_state_out(s):
    a = s[:, :, :NC, :NC]
    b_ = s[:, :, NC:, NC:]
    st = jnp.stack([a, b_], axis=2).reshape(s.shape[0], HC, NC, NC)
    return jnp.swapaxes(st, -1, -2)


def _ssd_state_in(s):
    b = s.shape[0]
    st = jnp.swapaxes(s, -1, -2).reshape(b, HD // 2, 2, NS, PD)
    pair = jnp.concatenate([st[:, :, 0], st[:, :, 1]], axis=-1)
    z = jnp.zeros_like(pair)
    g0 = jnp.concatenate([pair, z], axis=-2)
    g1 = jnp.concatenate([z, pair], axis=-2)
    is_g0 = (jnp.arange(HD // 2) < HD // 4)[None, :, None, None]
    return jnp.where(is_g0, g0, g1)


def _ssd_state_out(s):
    is_g0 = (jnp.arange(HD // 2) < HD // 4)[None, :, None, None]
    pair = jnp.where(is_g0, s[:, :, :NS, :], s[:, :, NS:, :])
    st = jnp.stack([pair[..., :PD], pair[..., PD:]], axis=2).reshape(s.shape[0], HD, NS, PD)
    return jnp.swapaxes(st, -1, -2)


def _layer0(h, mod, per_batch, p, norm1, ctx, tables, lam_vecs, subln, lambda_init, tm, tq):
    qa, ka, va, qb, ckv, kpe = _l0_proj(h, mod, per_batch, norm1, p["w"], p["qn"], p["wuq"],
                                        p["kvn"], tables, tm)
    own = (ka, va, ckv, kpe)
    if ctx is not None:
        ck, cv, cckv, ckpe = ctx
        ka = jnp.concatenate([ck.astype(BF16), ka], axis=2)
        va = jnp.concatenate([cv.astype(BF16), va], axis=2)
        ckv = jnp.concatenate([cckv, ckv], axis=1)
        kpe = jnp.concatenate([jnp.pad(ckpe, ((0, 0), (0, 0), (NOPE, LANE - NOPE - ROPE_B))), kpe],
                              axis=1)
    kb, vb = _kvup(ckv, kpe, p["wukv"], min(512, ckv.shape[1]))
    oa = _attention(qa, ka, va, tq, True, lam_vecs, subln, lambda_init)
    ob = _attention(qb, kb, vb, tq, False)
    return _oproj(h, mod, per_batch, oa, ob, p["w_out"], tm), own


def _layer1(h, mod, per_batch, p, wkv0, ssm0, tm):
    (r, k, v, kk, g, lw0, a0, kx0, lw1, a1, kx1, zg, xs, bc, dt) = _l1_proj(h, mod, per_batch, p, tm)
    yf, yb, s_wkv = _wkv(r, kk, v, lw0, a0, kx0, lw1, a1, kx1, wkv0)
    ydf, ydb, s_ssm = _ssd(xs, bc, dt, p["arow"], ssm0)
    consts = [p["ln_w"], p["ln_b"], p["r_k"], p["dsum"], p["gnorm"], p["w_out"], p["bd"]]
    h = _l1_post(h, mod, per_batch, [yf, yb, r, k, v, g, ydf, ydb, xs, zg], consts, tm)
    return h, s_wkv, s_ssm


def kernel(x_prompt, x_sample, cache_l0_k, cache_l0_v, cache_l0_ckv, cache_l0_kpe, state_l1_wkv_fwd, state_l1_wkv_bwd, state_l1_ssm_fwd, state_l1_ssm_bwd, c, c_ctx, ada_w_0, ada_b_0, norm1_0, norm2_0, ffn_up_0, ffn_conv_w_0, ffn_conv_b_0, ffn_down_0, l0_w_in, l0_lambda_q1, l0_lambda_k1, l0_lambda_q2, l0_lambda_k2, l0_subln, l0_q_norm, l0_w_uq, l0_kv_norm, l0_w_ukv, l0_w_out, ada_w_1, ada_b_1, norm1_1, norm2_1, ffn_up_1, ffn_conv_w_1, ffn_conv_b_1, ffn_down_1, l1_w_in, l1_mu, l1_w0, l1_w2, l1_a0, l1_a2, l1_g2, l1_k_k, l1_k_a, l1_r_k, l1_ln_w, l1_ln_b, l1_conv_w, l1_conv_b, l1_A_log, l1_dt_bias, l1_D, l1_gnorm, l1_w_out, norm_f):
    bc_, lc, _ = x_prompt.shape
    bl, ll, _ = x_sample.shape
    tm_c, tm_l = min(256, lc), min(256, ll)
    cond = jnp.concatenate([c_ctx[None, :], c], axis=0)
    cond8 = jnp.pad(cond, ((0, 8 - cond.shape[0]), (0, 0)))
    mods = [_ada(cond8, w, b).reshape(8, 6, D_MODEL) for w, b in ((ada_w_0, ada_b_0), (ada_w_1, ada_b_1))]
    p0 = _prep_l0(l0_w_in, l0_q_norm, l0_w_uq, l0_kv_norm, l0_w_ukv, l0_w_out)
    tables = _rope_tables(ll, DA, 0, DA)
    tables = tuple(jnp.concatenate([t, t], axis=1) for t in tables) + _rope_tables(ll, ROPE_B, NOPE, LANE)
    lam_vecs = jnp.pad(jnp.stack([l0_lambda_q1, l0_lambda_k1, l0_lambda_q2, l0_lambda_k2]),
                       ((0, 4), (0, LANE - DA)))
    subln = l0_subln.reshape(1, -1)
    lambda_init = 0.8 - 0.6 * math.exp(-0.3 * 0)
    n1 = norm1_0.reshape(1, -1)
    h_ctx, own = _layer0(x_prompt, mods[0][0:1], False, p0, n1, None, None, lam_vecs, subln,
                         lambda_init, tm_c, min(256, lc))
    h_lat, _ = _layer0(x_sample, mods[0][1:1 + bl], True, p0, n1,
                       (cache_l0_k, cache_l0_v, cache_l0_ckv, cache_l0_kpe), tables, lam_vecs,
                       subln, lambda_init, tm_l, min(128, ll))
    ffn0 = (norm2_0.reshape(1, -1), ffn_up_0.astype(BF16), ffn_conv_w_0, ffn_conv_b_0.reshape(1, -1),
            ffn_down_0.astype(BF16))
    h_ctx = _ffn(h_ctx, mods[0][0:1], False, *ffn0, None, tm_c)
    h_lat = _ffn(h_lat, mods[0][1:1 + bl], True, *ffn0, None, tm_l)
    p1 = _prep_l1(l1_w_in, l1_mu, l1_w0, l1_w2, l1_a0, l1_a2, l1_g2, l1_k_k, l1_k_a, l1_r_k,
                  l1_ln_w, l1_ln_b, l1_conv_w, l1_conv_b, l1_A_log, l1_dt_bias, l1_D, l1_gnorm,
                  l1_w_out, norm1_1)
    zero_state = jnp.zeros((bc_, 2, HC // 2, LANE, LANE), F32)
    h_ctx, s_wkv, s_ssm = _layer1(h_ctx, mods[1][0:1], False, p1, zero_state, zero_state, tm_c)
    wkv0 = jnp.stack([_wkv_state_in(state_l1_wkv_fwd), _wkv_state_in(state_l1_wkv_bwd)], axis=1)
    ssm0 = jnp.stack([_ssd_state_in(state_l1_ssm_fwd), _ssd_state_in(state_l1_ssm_bwd)], axis=1)
    h_lat, _, _ = _layer1(h_lat, mods[1][1:1 + bl], True, p1, wkv0, ssm0, tm_l)
    ffn1 = (norm2_1.reshape(1, -1), ffn_up_1.astype(BF16), ffn_conv_w_1, ffn_conv_b_1.reshape(1, -1),
            ffn_down_1.astype(BF16))
    nf = norm_f.reshape(1, -1)
    y_prompt = _ffn(h_ctx, mods[1][0:1], False, *ffn1, nf, tm_c)
    y_sample = _ffn(h_lat, mods[1][1:1 + bl], True, *ffn1, nf, tm_l)
    ka, va, ckv, kpe = own
    return (y_prompt, y_sample, ka, va, ckv, kpe[:, :, NOPE:NOPE + ROPE_B],
            _wkv_state_out(s_wkv[:, 0]), _wkv_state_out(s_wkv[:, 1]),
            _ssd_state_out(s_ssm[:, 0]), _ssd_state_out(s_ssm[:, 1]))
```

```python
import functools
import math

import jax
import jax.numpy as jnp
from jax import lax
from jax.experimental import pallas as pl
from jax.experimental.pallas import tpu as pltpu

F32 = jnp.float32
BF16 = jnp.bfloat16
HIGHEST = lax.Precision.HIGHEST

D_MODEL = 1024
GRID_W = 64
ROPE_BASE = 10000.0
NORM_EPS = 1e-6
HA, DA = 4, 64
HB, NOPE, ROPE_B, VB = 4, 64, 32, 128
Q_LORA, KV_LORA = 192, 128
HC, NC = 8, 64
WKV_LN_EPS = 64e-5
WKV_DECAY_SCALE = 0.606531
HD, PD, G_SSM, NS = 8, 64, 2, 64
D_INNER = HD * PD
F_FF = 2816
LANE = 128
HALO = 8
WKV_CHUNK = 64
SSD_CHUNK = 128
VMEM_LIMIT = 56 * 1024 * 1024


def _cparams(sem):
    return pltpu.CompilerParams(dimension_semantics=sem, vmem_limit_bytes=VMEM_LIMIT)


def _mm(a, b):
    return jnp.dot(a.astype(BF16), b.astype(BF16), preferred_element_type=F32)


def _mm_nt(a, b):
    return lax.dot_general(a.astype(BF16), b.astype(BF16), (((1,), (1,)), ((), ())),
                           preferred_element_type=F32)


def _mm_tn(a, b):
    return lax.dot_general(a.astype(BF16), b.astype(BF16), (((0,), (0,)), ((), ())),
                           preferred_element_type=F32)


def _mm_f32(a, b):
    return jnp.dot(a, b, precision=HIGHEST, preferred_element_type=F32)


def _mm_split(x, m):
    hi = x.astype(BF16)
    lo = (x - hi.astype(F32)).astype(BF16)
    return (jnp.dot(hi, m, preferred_element_type=F32) + jnp.dot(lo, m, preferred_element_type=F32))


def _rms(x, g, n):
    ms = jnp.sum(x * x, axis=-1, keepdims=True) * (1.0 / n)
    return x * lax.rsqrt(ms + NORM_EPS) * g


def _silu(x):
    return x * jax.nn.sigmoid(x)


def _lane(shape):
    return lax.broadcasted_iota(jnp.int32, shape, len(shape) - 1)


def _row(shape):
    return lax.broadcasted_iota(jnp.int32, shape, len(shape) - 2)


def _const_spec(shape):
    nd = len(shape)
    return pl.BlockSpec(shape, lambda *_: (0,) * nd)


def _ada_kernel(c_ref, w_ref, b_ref, o_ref):
    o_ref[...] = _mm(_silu(c_ref[...]), w_ref[...]) + b_ref[...]


def _ada(cond8, w, b):
    n = w.shape[1]
    tn = 1536
    return pl.pallas_call(
        _ada_kernel,
        grid=(n // tn,),
        in_specs=[_const_spec((8, D_MODEL)),
                  pl.BlockSpec((D_MODEL, tn), lambda j: (0, j)),
                  pl.BlockSpec((1, tn), lambda j: (0, j))],
        out_specs=pl.BlockSpec((8, tn), lambda j: (0, j)),
        out_shape=jax.ShapeDtypeStruct((8, n), F32),
        compiler_params=_cparams(("arbitrary",)),
        name="ada",
    )(cond8, w, b.reshape(1, n))


def _rope(x, c, s, half):
    w = x.shape[-1]
    up = pltpu.roll(x, w - half, 1)
    dn = pltpu.roll(x, half, 1)
    first = (_lane(x.shape) & (2 * half - 1)) < half
    return x * c + jnp.where(first, up, dn) * s


def _l0_proj_kernel(*refs, latent):
    if latent:
        (x_ref, mod_ref, n1_ref, w_ref, qn_ref, wuq_ref, kvn_ref, ca_ref, sa_ref, cb_ref, sb_ref,
         qa_o, ka_o, va_o, qb_o, ckv_o, kpe_o) = refs
    else:
        (x_ref, mod_ref, n1_ref, w_ref, qn_ref, wuq_ref, kvn_ref,
         qa_o, ka_o, va_o, qb_o, ckv_o, kpe_o) = refs
    mod = mod_ref[0]
    hn = _rms(x_ref[0], n1_ref[...], D_MODEL) * (1.0 + mod[1:2]) + mod[0:1]
    proj = _mm(hn, w_ref[...])
    qd = _rms(proj[:, 1536:1792], qn_ref[...], Q_LORA)
    qb = _mm(qd, wuq_ref[...])
    ckv_o[0] = _rms(proj[:, 1792:1920], kvn_ref[...], KV_LORA)
    kpe = proj[:, 1920:2048]
    if latent:
        ca, sa, cb, sb = ca_ref[...], sa_ref[...], cb_ref[...], sb_ref[...]
        kpe = _rope(kpe, cb, sb, ROPE_B // 4)
    kpe_o[0] = kpe
    for h in range(HA):
        sl = slice(h * LANE, (h + 1) * LANE)
        q = proj[:, sl]
        k = proj[:, 512 + h * LANE:512 + (h + 1) * LANE]
        qbh = qb[:, sl]
        if latent:
            q = _rope(q, ca, sa, DA // 4)
            k = _rope(k, ca, sa, DA // 4)
            qbh = _rope(qbh, cb, sb, ROPE_B // 4)
        qa_o[0, h] = (q * (DA ** -0.5)).astype(qa_o.dtype)
        ka_o[0, h] = k.astype(ka_o.dtype)
        va_o[0, h] = proj[:, 1024 + h * LANE:1024 + (h + 1) * LANE].astype(va_o.dtype)
        qb_o[0, h] = (qbh * ((NOPE + ROPE_B) ** -0.5)).astype(qb_o.dtype)


def _l0_proj(x, mod, per_batch, n1, w, qn, wuq, kvn, tables, tm):
    b, l, _ = x.shape
    latent = tables is not None
    kv_dtype = BF16 if latent else F32
    mod_map = (lambda bi, i: (bi, 0, 0)) if per_batch else (lambda bi, i: (0, 0, 0))
    in_specs = [pl.BlockSpec((1, tm, D_MODEL), lambda bi, i: (bi, i, 0)),
                pl.BlockSpec((1, 6, D_MODEL), mod_map),
                _const_spec((1, D_MODEL)), _const_spec(w.shape), _const_spec(qn.shape),
                _const_spec(wuq.shape), _const_spec(kvn.shape)]
    args = [x, mod, n1, w, qn, wuq, kvn]
    if latent:
        in_specs += [pl.BlockSpec((tm, LANE), lambda bi, i: (i, 0))] * 4
        args += list(tables)
    head_spec = pl.BlockSpec((1, HA, tm, LANE), lambda bi, i: (bi, 0, i, 0))
    row_spec = pl.BlockSpec((1, tm, LANE), lambda bi, i: (bi, i, 0))
    return pl.pallas_call(
        functools.partial(_l0_proj_kernel, latent=latent),
        grid=(b, l // tm),
        in_specs=in_specs,
        out_specs=[head_spec, head_spec, head_spec, head_spec, row_spec, row_spec],
        out_shape=[jax.ShapeDtypeStruct((b, HA, l, LANE), BF16),
                   jax.ShapeDtypeStruct((b, HA, l, LANE), kv_dtype),
                   jax.ShapeDtypeStruct((b, HA, l, LANE), kv_dtype),
                   jax.ShapeDtypeStruct((b, HB, l, LANE), BF16),
                   jax.ShapeDtypeStruct((b, l, LANE), F32),
                   jax.ShapeDtypeStruct((b, l, LANE), F32)],
        compiler_params=_cparams(("parallel", "parallel")),
        name="l0_proj",
    )(*args)


def _kvup_kernel(ckv_ref, kpe_ref, w_ref, kb_o, vb_o):
    kv = _mm(ckv_ref[0], w_ref[...])
    kpe = kpe_ref[0]
    for h in range(HB):
        kb_o[0, h] = (kv[:, h * LANE:(h + 1) * LANE] + kpe).astype(BF16)
        vb_o[0, h] = kv[:, 512 + h * LANE:512 + (h + 1) * LANE].astype(BF16)


def _kvup(ckv, kpe, w, tk):
    b, lk, _ = ckv.shape
    row_spec = pl.BlockSpec((1, tk, LANE), lambda bi, i: (bi, i, 0))
    head_spec = pl.BlockSpec((1, HB, tk, LANE), lambda bi, i: (bi, 0, i, 0))
    return pl.pallas_call(
        _kvup_kernel,
        grid=(b, lk // tk),
        in_specs=[row_spec, row_spec, _const_spec(w.shape)],
        out_specs=[head_spec, head_spec],
        out_shape=[jax.ShapeDtypeStruct((b, HB, lk, LANE), BF16)] * 2,
        compiler_params=_cparams(("parallel", "parallel")),
        name="kvup",
    )(ckv, kpe, w)


def _attn_kernel(*refs, diff, lambda_init):
    if diff:
        q_ref, k_ref, v_ref, lam_ref, g_ref, o_ref = refs
    else:
        q_ref, k_ref, v_ref, o_ref = refs
    q = q_ref[0, 0]
    k = k_ref[0, 0]
    v = v_ref[0, 0]
    tq = q.shape[0]
    if diff:
        lo = _lane(q.shape) < DA
        zero = jnp.zeros_like(q)
        qq = jnp.concatenate([jnp.where(lo, q, zero), jnp.where(lo, zero, q)], axis=0)
    else:
        qq = q
    s = _mm_nt(qq, k)
    p = jnp.exp(s - jnp.max(s, axis=-1, keepdims=True))
    inv = 1.0 / jnp.sum(p, axis=-1, keepdims=True)
    if diff:
        lv = lam_ref[...]
        lam = (jnp.exp(jnp.sum(lv[0:1] * lv[1:2], axis=-1, keepdims=True))
               - jnp.exp(jnp.sum(lv[2:3] * lv[3:4], axis=-1, keepdims=True)) + lambda_init)
        pd = p[:tq] * inv[:tq] - p[tq:] * (inv[tq:] * lam)
        o = _mm(pd, v)
        o = _rms(o, g_ref[...], 2 * DA) * (1.0 - lambda_init)
    else:
        o = _mm(p * inv, v)
    o_ref[0] = o.astype(o_ref.dtype)


def _attention(q, k, v, tq, diff, lam_vecs=None, subln=None, lambda_init=0.0):
    b, h, l, _ = q.shape
    lk = k.shape[2]
    in_specs = [pl.BlockSpec((1, 1, tq, LANE), lambda bi, hi, i: (bi, hi, i, 0)),
                pl.BlockSpec((1, 1, lk, LANE), lambda bi, hi, i: (bi, hi, 0, 0)),
                pl.BlockSpec((1, 1, lk, LANE), lambda bi, hi, i: (bi, hi, 0, 0))]
    args = [q, k, v]
    if diff:
        in_specs += [_const_spec(lam_vecs.shape), _const_spec(subln.shape)]
        args += [lam_vecs, subln]
    return pl.pallas_call(
        functools.partial(_attn_kernel, diff=diff, lambda_init=lambda_init),
        grid=(b, h, l // tq),
        in_specs=in_specs,
        out_specs=pl.BlockSpec((1, tq, LANE), lambda bi, hi, i: (bi, i, hi)),
        out_shape=jax.ShapeDtypeStruct((b, l, h * LANE), BF16),
        compiler_params=_cparams(("parallel", "parallel", "parallel")),
        name="attn_diff" if diff else "attn_mla",
    )(*args)


def _oproj_kernel(h_ref, mod_ref, a_ref, b_ref, w_ref, o_ref):
    half = a_ref.shape[-1]
    m = _mm(a_ref[0], w_ref[0:half, :]) + _mm(b_ref[0], w_ref[half:, :])
    o_ref[0] = h_ref[0] + mod_ref[0][2:3] * m


def _oproj(h, mod, per_batch, oa, ob, w, tm):
    b, l, _ = h.shape
    mod_map = (lambda bi, i: (bi, 0, 0)) if per_batch else (lambda bi, i: (0, 0, 0))
    x_spec = pl.BlockSpec((1, tm, D_MODEL), lambda bi, i: (bi, i, 0))
    half_spec = pl.BlockSpec((1, tm, oa.shape[-1]), lambda bi, i: (bi, i, 0))
    return pl.pallas_call(
        _oproj_kernel,
        grid=(b, l // tm),
        in_specs=[x_spec, pl.BlockSpec((1, 6, D_MODEL), mod_map), half_spec, half_spec,
                  _const_spec(w.shape)],
        out_specs=x_spec,
        out_shape=jax.ShapeDtypeStruct(h.shape, F32),
        compiler_params=_cparams(("parallel", "parallel")),
        name="oproj",
    )(h, mod, oa, ob, w)


def _halo_rows(x_ref, xp_ref, xn_ref):
    i = pl.program_id(1)
    last = pl.num_programs(1) - 1
    xc = jnp.concatenate([xp_ref[0], x_ref[0], xn_ref[0]], axis=0)
    tm = x_ref.shape[1]
    r = _row((tm + 2 * HALO, 1))
    valid = jnp.logical_and(jnp.logical_or(r >= HALO, i > 0),
                            jnp.logical_or(r < tm + HALO, i < last))
    return xc, valid


def _shift_rows(u, tm):
    n = u.shape[0]
    up = pltpu.roll(u, 1, 0)[HALO:HALO + tm]
    dn = pltpu.roll(u, n - 1, 0)[HALO:HALO + tm]
    return up, u[HALO:HALO + tm], dn


def _ffn_kernel(x_ref, xp_ref, xn_ref, mod_ref, n2_ref, wu_ref, cw_ref, cb_ref, wd_ref, *rest,
                final, cw):
    if final:
        nf_ref, o_ref = rest
    else:
        (o_ref,) = rest
    tm = x_ref.shape[1]
    mod = mod_ref[0]
    xc, valid = _halo_rows(x_ref, xp_ref, xn_ref)
    hn = _rms(xc, n2_ref[...], D_MODEL) * (1.0 + mod[4:5]) + mod[3:4]
    hn = jnp.where(valid, hn, 0.0).astype(BF16)
    acc = jnp.zeros((tm, D_MODEL), F32)
    for c in range(F_FF // cw):
        halves = []
        for off in (c * cw, F_FF + c * cw):
            u = jnp.dot(hn, wu_ref[:, off:off + cw], preferred_element_type=F32)
            up, mid, dn = _shift_rows(u, tm)
            w3 = cw_ref[:, off:off + cw]
            halves.append(up * w3[0:1] + mid * w3[1:2] + dn * w3[2:3] + cb_ref[:, off:off + cw])
        act = (_silu(halves[0]) * halves[1]).astype(BF16)
        acc = acc + jnp.dot(act, wd_ref[c * cw:(c + 1) * cw, :], preferred_element_type=F32)
    out = x_ref[0] + mod[5:6] * acc
    if final:
        out = _rms(out, nf_ref[...], D_MODEL)
    o_ref[0] = out


def _ffn(h, mod, per_batch, n2, wu, cw3, cb, wd, nf, tm):
    b, l, _ = h.shape
    nblk = l // tm
    nb = tm // HALO
    mod_map = (lambda bi, i: (bi, 0, 0)) if per_batch else (lambda bi, i: (0, 0, 0))
    x_spec = pl.BlockSpec((1, tm, D_MODEL), lambda bi, i: (bi, i, 0))
    in_specs = [x_spec,
                pl.BlockSpec((1, HALO, D_MODEL), lambda bi, i: (bi, jnp.maximum(i * nb - 1, 0), 0)),
                pl.BlockSpec((1, HALO, D_MODEL),
                             lambda bi, i: (bi, jnp.minimum((i + 1) * nb, nblk * nb - 1), 0)),
                pl.BlockSpec((1, 6, D_MODEL), mod_map),
                _const_spec(n2.shape), _const_spec(wu.shape), _const_spec(cw3.shape),
                _const_spec(cb.shape), _const_spec(wd.shape)]
    args = [h, h, h, mod, n2, wu, cw3, cb, wd]
    final = nf is not None
    if final:
        in_specs.append(_const_spec(nf.shape))
        args.append(nf)
    return pl.pallas_call(
        functools.partial(_ffn_kernel, final=final, cw=256),
        grid=(b, nblk),
        in_specs=in_specs,
        out_specs=x_spec,
        out_shape=jax.ShapeDtypeStruct(h.shape, F32),
        compiler_params=_cparams(("parallel", "parallel")),
        name="ffn",
    )(*args)


def _softplus(x):
    return jnp.maximum(x, 0.0) + jnp.log1p(jnp.exp(-jnp.abs(x)))


def _l1_proj_kernel(x_ref, xp_ref, xn_ref, mod_ref, n1_ref, w_ref, mu_ref, kk_ref, ka_ref,
                    w0_ref, w2_ref, a0_ref, a2_ref, g2_ref, cw_ref, cb_ref, dtb_ref, bd_ref,
                    r_o, k_o, v_o, kk_o, g_o, lw0_o, ai0_o, kx0_o, lw1_o, ai1_o, kx1_o,
                    zg_o, xs_o, bc_o, dt_o):
    tm = x_ref.shape[1]
    mod = mod_ref[0]
    xc, valid = _halo_rows(x_ref, xp_ref, xn_ref)
    hn = _rms(xc, n1_ref[...], D_MODEL) * (1.0 + mod[1:2]) + mod[0:1]
    hn = jnp.where(valid, hn, 0.0).astype(BF16)
    zc = jnp.dot(hn, w_ref[:, 0:1920], preferred_element_type=F32)
    up, mid, dn = _shift_rows(zc, tm)
    zc = mid + mu_ref[...] * (0.5 * (up + dn) - mid)
    r, k, v = zc[:, 0:512], zc[:, 512:1024], zc[:, 1024:1536]
    wd, ad, gd = zc[:, 1536:1664], zc[:, 1664:1792], zc[:, 1792:1920]
    r_o[0], k_o[0], v_o[0] = r, k, v
    kkf = k * kk_ref[...]
    kk_o[0] = kkf * lax.rsqrt(_mm_split(kkf * kkf, bd_ref[...]) + 1e-12)
    g_o[0] = _mm(jax.nn.sigmoid(gd), g2_ref[...])
    twd = jnp.tanh(wd)
    for i, (lw_o, ai_o, kx_o) in enumerate(((lw0_o, ai0_o, kx0_o), (lw1_o, ai1_o, kx1_o))):
        lw_o[0] = -WKV_DECAY_SCALE * jax.nn.sigmoid(w0_ref[i:i + 1] + _mm(twd, w2_ref[i]))
        a = jax.nn.sigmoid(a0_ref[i:i + 1] + _mm(ad, a2_ref[i]))
        ai_o[0] = a
        kx_o[0] = k * (1.0 + (a - 1.0) * ka_ref[...])
    zd = jnp.dot(hn, w_ref[:, 1920:3328], preferred_element_type=F32)
    zg_o[0] = zd[HALO:HALO + tm, 0:512]
    xbc = zd[:, 512:1280]
    up, mid, dn = _shift_rows(xbc, tm)
    cw3 = cw_ref[...]
    xbc = _silu(up * cw3[0:1] + mid * cw3[1:2] + dn * cw3[2:3] + cb_ref[...])
    xs_o[0] = xbc[:, 0:512]
    bc_o[0] = xbc[:, 512:768]
    dt_o[0] = _softplus(zd[HALO:HALO + tm, 1280:1408] + dtb_ref[...])


def _l1_proj(x, mod, per_batch, p, tm):
    b, l, _ = x.shape
    nblk = l // tm
    nb = tm // HALO
    mod_map = (lambda bi, i: (bi, 0, 0)) if per_batch else (lambda bi, i: (0, 0, 0))
    x_spec = pl.BlockSpec((1, tm, D_MODEL), lambda bi, i: (bi, i, 0))
    consts = [p["norm1"], p["w_in"], p["mu"], p["k_k"], p["k_a"], p["w0"], p["w2"], p["a0"],
              p["a2"], p["g2"], p["conv_w"], p["conv_b"], p["dt_bias"], p["bd"]]
    in_specs = [x_spec,
                pl.BlockSpec((1, HALO, D_MODEL), lambda bi, i: (bi, jnp.maximum(i * nb - 1, 0), 0)),
                pl.BlockSpec((1, HALO, D_MODEL),
                             lambda bi, i: (bi, jnp.minimum((i + 1) * nb, nblk * nb - 1), 0)),
                pl.BlockSpec((1, 6, D_MODEL), mod_map)] + [_const_spec(c.shape) for c in consts]

    def o_spec(w):
        return pl.BlockSpec((1, tm, w), lambda bi, i: (bi, i, 0))

    widths = [512] * 11 + [512, 512, 256, 128]
    return pl.pallas_call(
        _l1_proj_kernel,
        grid=(b, nblk),
        in_specs=in_specs,
        out_specs=[o_spec(w) for w in widths],
        out_shape=[jax.ShapeDtypeStruct((b, l, w), F32) for w in widths],
        compiler_params=_cparams(("parallel", "parallel")),
        name="l1_proj",
    )(x, x, x, mod, *consts)


def _tri(n, upper, strict=False):
    r, c = _row((n, n)), _lane((n, n))
    if upper:
        return (c > r) if strict else (c >= r)
    return (c < r) if strict else (c <= r)


def _wkv_step(dirs, st_ref):
    c = WKV_CHUNK
    npair = HC // 2
    ones = jnp.ones((c, LANE), F32)
    lane2 = _lane((2 * c, LANE))
    row2 = _row((2 * c, LANE))
    lo2 = lane2 < NC
    top2 = row2 < c
    bd2 = lo2 == top2
    tmask = row2 - jnp.where(top2, 0, c)
    imask = lane2 - jnp.where(lo2, 0, NC)
    blk16 = jnp.right_shift(tmask, 4) == jnp.right_shift(imask, 4)
    eye = jnp.logical_and(bd2, tmask == imask).astype(F32)
    lo1 = _lane((c, LANE)) < NC
    units = []
    for d, (r, kk, v, lw, a, kx, bwd) in enumerate(dirs):
        cum = _mm_f32(_tri(c, bwd).astype(F32), lw)
        last = cum[0:1] if bwd else cum[c - 1:c]
        e_in, e_out = jnp.exp(cum - lw), jnp.exp(cum)
        e_neg, e_end = jnp.exp(-cum), jnp.exp(last - cum)
        at, rt = -kk * e_in, r * e_out
        bb = kk * a
        bt, kt = bb * e_neg, kx * e_neg
        bh, kh = bb * e_end, kx * e_end
        wc = jnp.exp(lax.dot_general(lw, ones, (((0,), (0,)), ((), ())),
                                     precision=HIGHEST, preferred_element_type=F32))
        strict = (imask > tmask) if bwd else (imask < tmask)
        incl = ((imask >= tmask) if bwd else (imask <= tmask))[:c]
        for p in range(npair):
            sl = slice(p * LANE, (p + 1) * LANE)
            units.append(dict(
                d=d, p=p, strict=strict, incl=incl, v=v[:, sl], wc=wc[sl],
                l2=jnp.concatenate([at[:, sl], rt[:, sl]], axis=0).astype(BF16),
                bk=jnp.concatenate([bt[:, sl], kt[:, sl]], axis=0).astype(BF16),
                kb=jnp.concatenate([kt[:, sl], bt[:, sl]], axis=0).astype(BF16),
                hat=jnp.concatenate([bh[:, sl], kh[:, sl]], axis=0).astype(BF16)))
    zero = jnp.zeros((2 * c, LANE), BF16)
    for u in units:
        u["st"] = st_ref[u["d"], u["p"]]
    g_a = [_mm_nt(jnp.where(lo2, u["l2"], zero), u["bk"]) for u in units]
    g_b = [_mm_nt(jnp.where(lo2, zero, u["l2"]), u["kb"]) for u in units]
    ga = [jnp.where(u["strict"], jnp.concatenate([x[:c], y[:c]], axis=0), 0.0)
          for u, x, y in zip(units, g_a, g_b)]
    gr_a = [jnp.where(u["incl"], x[c:], 0.0).astype(BF16) for u, x in zip(units, g_a)]
    gr_b = [jnp.where(u["incl"], y[c:], 0.0).astype(BF16) for u, y in zip(units, g_b)]
    n_p = [jnp.where(bd2, x, 0.0) for x in ga]
    ak = [jnp.where(bd2, 0.0, x).astype(BF16) for x in ga]
    nd = [jnp.where(blk16, x, 0.0) for x in n_p]
    no = [(x - y).astype(BF16) for x, y in zip(n_p, nd)]
    ndb = [x.astype(BF16) for x in nd]
    n2 = [_mm(x, x).astype(BF16) for x in ndb]
    n4 = [_mm(x, x).astype(BF16) for x in n2]
    n8 = [_mm(x, x).astype(BF16) for x in n4]
    dinv = [eye + x for x in nd]
    dinv = [x + _mm(x, y) for x, y in zip(dinv, n2)]
    dinv = [x + _mm(x, y) for x, y in zip(dinv, n4)]
    dinv = [(x + _mm(x, y)).astype(BF16) for x, y in zip(dinv, n8)]
    m1 = [_mm(x, y) for x, y in zip(dinv, no)]
    m2 = [_mm(x, x) for x in m1]
    t_p = [eye + x for x in m1]
    t_p = [x + _mm(x, y) for x, y in zip(t_p, m2)]
    t_p = [_mm(x, y).astype(BF16) for x, y in zip(t_p, dinv)]
    akv = [_mm(x, jnp.concatenate([u["v"], u["v"]], axis=0)) for u, x in zip(units, ak)]
    base = [_mm(u["l2"], u["st"]) for u in units]
    us = [_mm(t, jnp.concatenate([b[:c], b[:c]], axis=0) + x) for t, b, x in zip(t_p, base, akv)]
    uu = [jnp.where(lo1, x[:c], x[c:]) for x in us]
    uv = [jnp.concatenate([x, u["v"]], axis=0).astype(BF16) for u, x in zip(units, uu)]
    vu = [jnp.concatenate([u["v"], x], axis=0).astype(BF16) for u, x in zip(units, uu)]
    ys = [b[c:] + jnp.where(lo1, _mm(x, p), _mm(y, q))
          for b, x, y, p, q in zip(base, gr_a, gr_b, uv, vu)]
    upd = [_mm_tn(u["hat"], x) for u, x in zip(units, uv)]
    for u, x in zip(units, upd):
        st_ref[u["d"], u["p"]] = u["st"] * u["wc"] + jnp.where(bd2, x, 0.0)
    return [jnp.concatenate(ys[d * npair:(d + 1) * npair], axis=1) for d in range(2)]


def _wkv_kernel(rf, kkf, vf, lw0, a0, kx0, rb, kkb, vb, lw1, a1, kx1, s0_ref,
                yf_o, yb_o, sfin_o, st_ref):
    j = pl.program_id(1)

    @pl.when(j == 0)
    def _():
        st_ref[...] = s0_ref[0]

    yf, yb = _wkv_step([(rf[0], kkf[0], vf[0], lw0[0], a0[0], kx0[0], False),
                        (rb[0], kkb[0], vb[0], lw1[0], a1[0], kx1[0], True)], st_ref)
    yf_o[0] = yf
    yb_o[0] = yb

    @pl.when(j == pl.num_programs(1) - 1)
    def _():
        sfin_o[0] = st_ref[...]


def _wkv(r, kk, v, lw0, a0, kx0, lw1, a1, kx1, s0):
    b, l, w = r.shape
    c = WKV_CHUNK
    nc = l // c
    fwd = pl.BlockSpec((1, c, w), lambda bi, j: (bi, j, 0))
    bwd = pl.BlockSpec((1, c, w), lambda bi, j: (bi, nc - 1 - j, 0))
    st_spec = pl.BlockSpec((1,) + s0.shape[1:], lambda bi, j: (bi, 0, 0, 0, 0))
    return pl.pallas_call(
        _wkv_kernel,
        grid=(b, nc),
        in_specs=[fwd] * 6 + [bwd] * 6 + [st_spec],
        out_specs=[fwd, bwd, st_spec],
        out_shape=[jax.ShapeDtypeStruct((b, l, w), F32), jax.ShapeDtypeStruct((b, l, w), F32),
                   jax.ShapeDtypeStruct(s0.shape, F32)],
        scratch_shapes=[pltpu.VMEM(s0.shape[1:], F32)],
        compiler_params=_cparams(("parallel", "arbitrary")),
        name="wkv",
    )(r, kk, v, lw0, a0, kx0, r, kk, v, lw1, a1, kx1, s0)


def _ssd_chunk(xs, bc, dt, arow, st_ref, d, bwd):
    c = xs.shape[0]
    o = HD * d
    acum = _mm_f32(_tri(c, bwd).astype(F32), dt * arow)
    acum_t = acum.T
    last = acum[0:1] if bwd else acum[c - 1:c]
    ea = jnp.exp(acum)
    de = jnp.exp(last - acum)
    cd = jnp.exp(last)
    bfull, cfull = bc[:, 0:LANE], bc[:, LANE:2 * LANE]
    lane = _lane((c, LANE))
    lo = lane < PD
    causal = _tri(c, bwd)
    zero = jnp.zeros_like(cfull)
    cb = [_mm_nt(jnp.where(lo, cfull, zero), bfull), _mm_nt(jnp.where(lo, zero, cfull), bfull)]
    grp_rows = _row((c, LANE)) < NS

    def colsel(m, ha, hb):
        return jnp.where(lo, jnp.broadcast_to(m[:, o + ha:o + ha + 1], (c, LANE)),
                         jnp.broadcast_to(m[:, o + hb:o + hb + 1], (c, LANE)))

    ys = []
    for p in range(HD // 2):
        g = p // 2
        ha, hb = 2 * p, 2 * p + 1
        xdt = xs[:, p * LANE:(p + 1) * LANE] * colsel(dt, ha, hb)
        yi = []
        for h in (ha, hb):
            seg = jnp.broadcast_to(acum[:, o + h:o + h + 1], (c, c)) - jnp.broadcast_to(acum_t[o + h:o + h + 1, :], (c, c))
            dec = jnp.where(causal, jnp.exp(jnp.where(causal, seg, 0.0)), 0.0)
            yi.append(_mm(cb[g] * dec, xdt))
        st = st_ref[d, p]
        y = jnp.where(lo, yi[0], yi[1]) + _mm(cfull, st) * colsel(ea, ha, hb)
        ys.append(y)
        cs = _mm_tn(bfull, xdt * colsel(de, ha, hb))
        cs = jnp.where(grp_rows if g == 0 else jnp.logical_not(grp_rows), cs, 0.0)
        cdp = jnp.where(lo[0:1], jnp.broadcast_to(cd[:, o + ha:o + ha + 1], (1, LANE)),
                        jnp.broadcast_to(cd[:, o + hb:o + hb + 1], (1, LANE)))
        st_ref[d, p] = st * cdp + cs
    return jnp.concatenate(ys, axis=1)


def _ssd_kernel(xf, bcf, dtf, xb, bcb, dtb, arow_ref, s0_ref, yf_o, yb_o, sfin_o, st_ref):
    j = pl.program_id(1)

    @pl.when(j == 0)
    def _():
        st_ref[...] = s0_ref[0]

    arow = arow_ref[...]
    yf_o[0] = _ssd_chunk(xf[0], bcf[0], dtf[0], arow, st_ref, 0, False)
    yb_o[0] = _ssd_chunk(xb[0], bcb[0], dtb[0], arow, st_ref, 1, True)

    @pl.when(j == pl.num_programs(1) - 1)
    def _():
        sfin_o[0] = st_ref[...]


def _ssd(xs, bc, dt, arow, s0):
    b, l, _ = xs.shape
    c = SSD_CHUNK
    nc = l // c

    def spec(w, rev):
        if rev:
            return pl.BlockSpec((1, c, w), lambda bi, j: (bi, nc - 1 - j, 0))
        return pl.BlockSpec((1, c, w), lambda bi, j: (bi, j, 0))

    st_spec = pl.BlockSpec((1,) + s0.shape[1:], lambda bi, j: (bi, 0, 0, 0, 0))
    return pl.pallas_call(
        _ssd_kernel,
        grid=(b, nc),
        in_specs=[spec(512, False), spec(256, False), spec(128, False),
                  spec(512, True), spec(256, True), spec(128, True),
                  _const_spec(arow.shape), st_spec],
        out_specs=[spec(512, False), spec(512, True), st_spec],
        out_shape=[jax.ShapeDtypeStruct((b, l, 512), F32), jax.ShapeDtypeStruct((b, l, 512), F32),
                   jax.ShapeDtypeStruct(s0.shape, F32)],
        scratch_shapes=[pltpu.VMEM(s0.shape[1:], F32)],
        compiler_params=_cparams(("parallel", "arbitrary")),
        name="ssd",
    )(xs, bc, dt, xs, bc, dt, arow, s0)


def _l1_post_kernel(h_ref, mod_ref, yf, yb, r, k, v, g, ydf, ydb, xs, zg, lnw, lnb, rk, dsum, gn,
                    w_ref, bd_ref, o_ref):
    bd = bd_ref[...]
    y = yf[0] + yb[0]
    mu = _mm_split(y, bd) * (1.0 / NC)
    dl = y - mu
    var = _mm_split(dl * dl, bd) * (1.0 / NC)
    y = dl * lax.rsqrt(var + WKV_LN_EPS) * lnw[...] + lnb[...]
    y = y + _mm_split(r[0] * k[0] * rk[...], bd) * v[0]
    oc = y * g[0]
    x = xs[0]
    yd = (ydf[0] + dsum[0:1] * x) + (ydb[0] + dsum[1:2] * x)
    od = _rms(yd * _silu(zg[0]), gn[...], D_INNER)
    m = _mm(oc, w_ref[0:512, :]) + _mm(od, w_ref[512:1024, :])
    o_ref[0] = h_ref[0] + mod_ref[0][2:3] * m


def _l1_post(h, mod, per_batch, acts, consts, tm):
    b, l, _ = h.shape
    mod_map = (lambda bi, i: (bi, 0, 0)) if per_batch else (lambda bi, i: (0, 0, 0))
    x_spec = pl.BlockSpec((1, tm, D_MODEL), lambda bi, i: (bi, i, 0))
    a_spec = pl.BlockSpec((1, tm, 512), lambda bi, i: (bi, i, 0))
    return pl.pallas_call(
        _l1_post_kernel,
        grid=(b, l // tm),
        in_specs=[x_spec, pl.BlockSpec((1, 6, D_MODEL), mod_map)] + [a_spec] * len(acts)
        + [_const_spec(c.shape) for c in consts],
        out_specs=x_spec,
        out_shape=jax.ShapeDtypeStruct(h.shape, F32),
        compiler_params=_cparams(("parallel", "parallel")),
        name="l1_post",
    )(h, mod, *acts, *consts)


def _rope_tables(length, dim, offset, width):
    quarter = dim // 4
    inv = ROPE_BASE ** (-jnp.arange(quarter, dtype=F32) / quarter)
    pos = jnp.arange(length)
    row = (pos // GRID_W).astype(F32)
    col = (pos % GRID_W).astype(F32)
    ar, ac = row[:, None] * inv, col[:, None] * inv
    c = jnp.concatenate([jnp.cos(ar), jnp.cos(ar), jnp.cos(ac), jnp.cos(ac)], axis=-1)
    s = jnp.concatenate([-jnp.sin(ar), jnp.sin(ar), -jnp.sin(ac), jnp.sin(ac)], axis=-1)
    cw = jnp.ones((length, width), F32).at[:, offset:offset + dim].set(c)
    sw = jnp.zeros((length, width), F32).at[:, offset:offset + dim].set(s)
    return cw, sw


def _pad_cols(w, n):
    return jnp.pad(w, ((0, 0), (0, n - w.shape[1])))


def _prep_l0(w_in, q_norm, w_uq, kv_norm, w_ukv, w_out):
    a_cols = 3 * HA * 2 * DA
    wqd = _pad_cols(w_in[:, a_cols:a_cols + Q_LORA], 256)
    wckv = w_in[:, a_cols + Q_LORA:a_cols + Q_LORA + KV_LORA]
    wkpe = jnp.pad(w_in[:, a_cols + Q_LORA + KV_LORA:], ((0, 0), (NOPE, LANE - NOPE - ROPE_B)))
    w = jnp.concatenate([w_in[:, :a_cols], wqd, wckv, wkpe], axis=1).astype(BF16)
    qn = _pad_cols(q_norm.reshape(1, -1), 256)
    wuq = jnp.pad(w_uq.reshape(Q_LORA, HB, NOPE + ROPE_B),
                  ((0, 256 - Q_LORA), (0, 0), (0, LANE - NOPE - ROPE_B))).reshape(256, HB * LANE)
    wukv = w_ukv.reshape(KV_LORA, HB, NOPE + VB)
    wk = jnp.pad(wukv[:, :, :NOPE], ((0, 0), (0, 0), (0, LANE - NOPE))).reshape(KV_LORA, HB * LANE)
    wv = wukv[:, :, NOPE:].reshape(KV_LORA, HB * VB)
    return dict(w=w, qn=qn, wuq=wuq.astype(BF16), kvn=kv_norm.reshape(1, -1),
                wukv=jnp.concatenate([wk, wv], axis=1).astype(BF16), w_out=w_out.astype(BF16))


def _prep_l1(w_in, mu, w0, w2, a0, a2, g2, k_k, k_a, r_k, ln_w, ln_b, conv_w, conv_b, A_log,
             dt_bias, D, gnorm, w_out, norm1):
    c_cols = 3 * HC * NC + 4 * 64 + 128
    conv_ch = D_INNER + 2 * G_SSM * NS
    wdt = w_in[:, c_cols + D_INNER + conv_ch:]
    wdt = _pad_cols(jnp.concatenate([wdt, wdt], axis=1), LANE)
    w = jnp.concatenate([w_in[:, :c_cols + D_INNER + conv_ch], wdt], axis=1).astype(BF16)
    z = jnp.zeros((64, HC * NC), F32)
    w2p = jnp.stack([jnp.concatenate([w2[0], z], 0), jnp.concatenate([z, w2[1]], 0)]).astype(BF16)
    a2p = jnp.stack([jnp.concatenate([a2[0], z], 0), jnp.concatenate([z, a2[1]], 0)]).astype(BF16)
    hid = jnp.arange(HC * NC) // NC
    bd = (hid[:, None] == hid[None, :]).astype(BF16)
    dtb = _pad_cols(dt_bias.reshape(1, 2 * HD), LANE)
    arow = _pad_cols((-jnp.exp(A_log.astype(F32))).reshape(1, 2 * HD), LANE)
    return dict(norm1=norm1.reshape(1, -1), w_in=w, mu=mu.reshape(1, -1), k_k=k_k.reshape(1, -1),
                k_a=k_a.reshape(1, -1), w0=w0, w2=w2p, a0=a0, a2=a2p, g2=g2.astype(BF16),
                conv_w=conv_w, conv_b=conv_b.reshape(1, -1), dt_bias=dtb, bd=bd, arow=arow,
                ln_w=ln_w.reshape(1, -1), ln_b=ln_b.reshape(1, -1), r_k=r_k.reshape(1, -1),
                dsum=jnp.repeat(D, PD, axis=1), gnorm=gnorm.reshape(1, -1),
                w_out=w_out.astype(BF16))


def _wkv_state_in(s):
    b = s.shape[0]
    st = jnp.swapaxes(s, -1, -2).reshape(b, HC // 2, 2, NC, NC)
    z = jnp.zeros_like(st[:, :, 0])
    top = jnp.concatenate([st[:, :, 0], z], axis=-1)
    bot = jnp.concatenate([z, st[:, :, 1]], axis=-1)
    return jnp.concatenate([top, bot], axis=-2)


def _wkv_state_out(s):
    a = s[:, :, :NC, :NC]
    b_ = s[:, :, NC:, NC:]
    st = jnp.stack([a, b_], axis=2).reshape(s.shape[0], HC, NC, NC)
    return jnp.swapaxes(st, -1, -2)


def _ssd_state_in(s):
    b = s.shape[0]
    st = jnp.swapaxes(s, -1, -2).reshape(b, HD // 2, 2, NS, PD)
    pair = jnp.concatenate([st[:, :, 0], st[:, :, 1]], axis=-1)
    z = jnp.zeros_like(pair)
    g0 = jnp.concatenate([pair, z], axis=-2)
    g1 = jnp.concatenate([z, pair], axis=-2)
    is_g0 = (jnp.arange(HD // 2) < HD // 4)[None, :, None, None]
    return jnp.where(is_g0, g0, g1)


def _ssd_state_out(s):
    is_g0 = (jnp.arange(HD // 2) < HD // 4)[None, :, None, None]
    pair = jnp.where(is_g0, s[:, :, :NS, :], s[:, :, NS:, :])
    st = jnp.stack([pair[..., :PD], pair[..., PD:]], axis=2).reshape(s.shape[0], HD, NS, PD)
    return jnp.swapaxes(st, -1, -2)


def _layer0(h, mod, per_batch, p, norm1, ctx, tables, lam_vecs, subln, lambda_init, tm, tq):
    qa, ka, va, qb, ckv, kpe = _l0_proj(h, mod, per_batch, norm1, p["w"], p["qn"], p["wuq"],
                                        p["kvn"], tables, tm)
    own = (ka, va, ckv, kpe)
    if ctx is not None:
        ck, cv, cckv, ckpe = ctx
        ka = jnp.concatenate([ck.astype(BF16), ka], axis=2)
        va = jnp.concatenate([cv.astype(BF16), va], axis=2)
        ckv = jnp.concatenate([cckv, ckv], axis=1)
        kpe = jnp.concatenate([jnp.pad(ckpe, ((0, 0), (0, 0), (NOPE, LANE - NOPE - ROPE_B))), kpe],
                              axis=1)
    kb, vb = _kvup(ckv, kpe, p["wukv"], min(512, ckv.shape[1]))
    oa = _attention(qa, ka, va, tq, True, lam_vecs, subln, lambda_init)
    ob = _attention(qb, kb, vb, tq, False)
    return _oproj(h, mod, per_batch, oa, ob, p["w_out"], tm), own


def _layer1(h, mod, per_batch, p, wkv0, ssm0, tm):
    (r, k, v, kk, g, lw0, a0, kx0, lw1, a1, kx1, zg, xs, bc, dt) = _l1_proj(h, mod, per_batch, p, tm)
    yf, yb, s_wkv = _wkv(r, kk, v, lw0, a0, kx0, lw1, a1, kx1, wkv0)
    ydf, ydb, s_ssm = _ssd(xs, bc, dt, p["arow"], ssm0)
    consts = [p["ln_w"], p["ln_b"], p["r_k"], p["dsum"], p["gnorm"], p["w_out"], p["bd"]]
    h = _l1_post(h, mod, per_batch, [yf, yb, r, k, v, g, ydf, ydb, xs, zg], consts, tm)
    return h, s_wkv, s_ssm


def kernel(x_prompt, x_sample, cache_l0_k, cache_l0_v, cache_l0_ckv, cache_l0_kpe, state_l1_wkv_fwd, state_l1_wkv_bwd, state_l1_ssm_fwd, state_l1_ssm_bwd, c, c_ctx, ada_w_0, ada_b_0, norm1_0, norm2_0, ffn_up_0, ffn_conv_w_0, ffn_conv_b_0, ffn_down_0, l0_w_in, l0_lambda_q1, l0_lambda_k1, l0_lambda_q2, l0_lambda_k2, l0_subln, l0_q_norm, l0_w_uq, l0_kv_norm, l0_w_ukv, l0_w_out, ada_w_1, ada_b_1, norm1_1, norm2_1, ffn_up_1, ffn_conv_w_1, ffn_conv_b_1, ffn_down_1, l1_w_in, l1_mu, l1_w0, l1_w2, l1_a0, l1_a2, l1_g2, l1_k_k, l1_k_a, l1_r_k, l1_ln_w, l1_ln_b, l1_conv_w, l1_conv_b, l1_A_log, l1_dt_bias, l1_D, l1_gnorm, l1_w_out, norm_f):
    bc_, lc, _ = x_prompt.shape
    bl, ll, _ = x_sample.shape
    tm_c, tm_l = min(256, lc), min(256, ll)
    cond = jnp.concatenate([c_ctx[None, :], c], axis=0)
    cond8 = jnp.pad(cond, ((0, 8 - cond.shape[0]), (0, 0)))
    mods = [_ada(cond8, w, b).reshape(8, 6, D_MODEL) for w, b in ((ada_w_0, ada_b_0), (ada_w_1, ada_b_1))]
    p0 = _prep_l0(l0_w_in, l0_q_norm, l0_w_uq, l0_kv_norm, l0_w_ukv, l0_w_out)
    tables = _rope_tables(ll, DA, 0, DA)
    tables = tuple(jnp.concatenate([t, t], axis=1) for t in tables) + _rope_tables(ll, ROPE_B, NOPE, LANE)
    lam_vecs = jnp.pad(jnp.stack([l0_lambda_q1, l0_lambda_k1, l0_lambda_q2, l0_lambda_k2]),
                       ((0, 4), (0, LANE - DA)))
    subln = l0_subln.reshape(1, -1)
    lambda_init = 0.8 - 0.6 * math.exp(-0.3 * 0)
    n1 = norm1_0.reshape(1, -1)
    h_ctx, own = _layer0(x_prompt, mods[0][0:1], False, p0, n1, None, None, lam_vecs, subln,
                         lambda_init, tm_c, min(256, lc))
    h_lat, _ = _layer0(x_sample, mods[0][1:1 + bl], True, p0, n1,
                       (cache_l0_k, cache_l0_v, cache_l0_ckv, cache_l0_kpe), tables, lam_vecs,
                       subln, lambda_init, tm_l, min(128, ll))
    ffn0 = (norm2_0.reshape(1, -1), ffn_up_0.astype(BF16), ffn_conv_w_0, ffn_conv_b_0.reshape(1, -1),
            ffn_down_0.astype(BF16))
    h_ctx = _ffn(h_ctx, mods[0][0:1], False, *ffn0, None, tm_c)
    h_lat = _ffn(h_lat, mods[0][1:1 + bl], True, *ffn0, None, tm_l)
    p1 = _prep_l1(l1_w_in, l1_mu, l1_w0, l1_w2, l1_a0, l1_a2, l1_g2, l1_k_k, l1_k_a, l1_r_k,
                  l1_ln_w, l1_ln_b, l1_conv_w, l1_conv_b, l1_A_log, l1_dt_bias, l1_D, l1_gnorm,
                  l1_w_out, norm1_1)
    zero_state = jnp.zeros((bc_, 2, HC // 2, LANE, LANE), F32)
    h_ctx, s_wkv, s_ssm = _layer1(h_ctx, mods[1][0:1], False, p1, zero_state, zero_state, tm_c)
    wkv0 = jnp.stack([_wkv_state_in(state_l1_wkv_fwd), _wkv_state_in(state_l1_wkv_bwd)], axis=1)
    ssm0 = jnp.stack([_ssd_state_in(state_l1_ssm_fwd), _ssd_state_in(state_l1_ssm_bwd)], axis=1)
    h_lat, _, _ = _layer1(h_lat, mods[1][1:1 + bl], True, p1, wkv0, ssm0, tm_l)
    ffn1 = (norm2_1.reshape(1, -1), ffn_up_1.astype(BF16), ffn_conv_w_1, ffn_conv_b_1.reshape(1, -1),
            ffn_down_1.astype(BF16))
    nf = norm_f.reshape(1, -1)
    y_prompt = _ffn(h_ctx, mods[1][0:1], False, *ffn1, nf, tm_c)
    y_sample = _ffn(h_lat, mods[1][1:1 + bl], True, *ffn1, nf, tm_l)
    ka, va, ckv, kpe = own
    return (y_prompt, y_sample, ka, va, ckv, kpe[:, :, NOPE:NOPE + ROPE_B],
            _wkv_state_out(s_wkv[:, 0]), _wkv_state_out(s_wkv[:, 1]),
            _ssd_state_out(s_ssm[:, 0]), _ssd_state_out(s_ssm[:, 1]))
```

```python
import functools
import math

import jax
import jax.numpy as jnp
from jax import lax
from jax.experimental import pallas as pl
from jax.experimental.pallas import tpu as pltpu

F32 = jnp.float32
BF16 = jnp.bfloat16
HIGHEST = lax.Precision.HIGHEST

D_MODEL = 1024
GRID_W = 64
ROPE_BASE = 10000.0
NORM_EPS = 1e-6
HA, DA = 4, 64
HB, NOPE, ROPE_B, VB = 4, 64, 32, 128
Q_LORA, KV_LORA = 192, 128
HC, NC = 8, 64
WKV_LN_EPS = 64e-5
WKV_DECAY_SCALE = 0.606531
HD, PD, G_SSM, NS = 8, 64, 2, 64
D_INNER = HD * PD
F_FF = 2816
LANE = 128
HALO = 8
LOG2E = math.log2(math.e)
WKV_CHUNK = 64
SSD_CHUNK = 128
VMEM_LIMIT = 56 * 1024 * 1024


def _cparams(sem):
    return pltpu.CompilerParams(dimension_semantics=sem, vmem_limit_bytes=VMEM_LIMIT)


def _mm(a, b):
    return jnp.dot(a.astype(BF16), b.astype(BF16), preferred_element_type=F32)


def _mm_nt(a, b):
    return lax.dot_general(a.astype(BF16), b.astype(BF16), (((1,), (1,)), ((), ())),
                           preferred_element_type=F32)


def _mm_tn(a, b):
    return lax.dot_general(a.astype(BF16), b.astype(BF16), (((0,), (0,)), ((), ())),
                           preferred_element_type=F32)


def _mm_f32(a, b):
    return jnp.dot(a, b, precision=HIGHEST, preferred_element_type=F32)


def _mm_split(x, m):
    hi = x.astype(BF16)
    lo = (x - hi.astype(F32)).astype(BF16)
    return (jnp.dot(hi, m, preferred_element_type=F32) + jnp.dot(lo, m, preferred_element_type=F32))


def _rms(x, g, n):
    ms = jnp.sum(x * x, axis=-1, keepdims=True) * (1.0 / n)
    return x * lax.rsqrt(ms + NORM_EPS) * g


def _silu(x):
    return x * jax.nn.sigmoid(x)


def _lane(shape):
    return lax.broadcasted_iota(jnp.int32, shape, len(shape) - 1)


def _row(shape):
    return lax.broadcasted_iota(jnp.int32, shape, len(shape) - 2)


def _const_spec(shape, single=False):
    nd = len(shape)
    if single:
        return pl.BlockSpec(shape, lambda *_: (0,) * nd, pipeline_mode=pl.Buffered(1))
    return pl.BlockSpec(shape, lambda *_: (0,) * nd)


def _ada_kernel(c_ref, w_ref, b_ref, o_ref):
    o_ref[...] = _mm(_silu(c_ref[...]), w_ref[...]) + b_ref[...]


def _ada(cond8, w, b):
    n = w.shape[1]
    tn = 1536
    return pl.pallas_call(
        _ada_kernel,
        grid=(n // tn,),
        in_specs=[_const_spec((8, D_MODEL)),
                  pl.BlockSpec((D_MODEL, tn), lambda j: (0, j)),
                  pl.BlockSpec((1, tn), lambda j: (0, j))],
        out_specs=pl.BlockSpec((8, tn), lambda j: (0, j)),
        out_shape=jax.ShapeDtypeStruct((8, n), F32),
        compiler_params=_cparams(("arbitrary",)),
        name="ada",
    )(cond8, w, b.reshape(1, n))


def _rope(x, c, s, half):
    w = x.shape[-1]
    up = pltpu.roll(x, w - half, 1)
    dn = pltpu.roll(x, half, 1)
    first = (_lane(x.shape) & (2 * half - 1)) < half
    return x * c + jnp.where(first, up, dn) * s


def _l0_proj_kernel(*refs, latent):
    if latent:
        (x_ref, mod_ref, n1_ref, w_ref, qn_ref, wuq_ref, kvn_ref, ca_ref, sa_ref, cb_ref, sb_ref,
         qa_o, ka_o, va_o, qb_o, ckv_o, kpe_o) = refs
    else:
        (x_ref, mod_ref, n1_ref, w_ref, qn_ref, wuq_ref, kvn_ref,
         qa_o, ka_o, va_o, qb_o, ckv_o, kpe_o) = refs
    mod = mod_ref[0]
    hn = _rms(x_ref[0], n1_ref[...], D_MODEL) * (1.0 + mod[1:2]) + mod[0:1]
    proj = _mm(hn, w_ref[...])
    qd = _rms(proj[:, 1536:1792], qn_ref[...], Q_LORA)
    qb = _mm(qd, wuq_ref[...])
    ckv_o[0] = _rms(proj[:, 1792:1920], kvn_ref[...], KV_LORA)
    kpe = proj[:, 1920:2048]
    if latent:
        ca, sa, cb, sb = ca_ref[...], sa_ref[...], cb_ref[...], sb_ref[...]
        kpe = _rope(kpe, cb, sb, ROPE_B // 4)
    kpe_o[0] = kpe
    for h in range(HA):
        sl = slice(h * LANE, (h + 1) * LANE)
        q = proj[:, sl]
        k = proj[:, 512 + h * LANE:512 + (h + 1) * LANE]
        qbh = qb[:, sl]
        if latent:
            q = _rope(q, ca, sa, DA // 4)
            k = _rope(k, ca, sa, DA // 4)
            qbh = _rope(qbh, cb, sb, ROPE_B // 4)
        qa_o[0, h] = (q * (DA ** -0.5 * LOG2E)).astype(qa_o.dtype)
        ka_o[0, h] = k.astype(ka_o.dtype)
        va_o[0, h] = proj[:, 1024 + h * LANE:1024 + (h + 1) * LANE].astype(va_o.dtype)
        qb_o[0, h] = (qbh * ((NOPE + ROPE_B) ** -0.5 * LOG2E)).astype(qb_o.dtype)


def _l0_proj(x, mod, per_batch, n1, w, qn, wuq, kvn, tables, tm):
    b, l, _ = x.shape
    latent = tables is not None
    kv_dtype = BF16 if latent else F32
    mod_map = (lambda bi, i: (bi, 0, 0)) if per_batch else (lambda bi, i: (0, 0, 0))
    in_specs = [pl.BlockSpec((1, tm, D_MODEL), lambda bi, i: (bi, i, 0)),
                pl.BlockSpec((1, 6, D_MODEL), mod_map),
                _const_spec((1, D_MODEL)), _const_spec(w.shape), _const_spec(qn.shape),
                _const_spec(wuq.shape), _const_spec(kvn.shape)]
    args = [x, mod, n1, w, qn, wuq, kvn]
    if latent:
        in_specs += [pl.BlockSpec((tm, LANE), lambda bi, i: (i, 0))] * 4
        args += list(tables)
    head_spec = pl.BlockSpec((1, HA, tm, LANE), lambda bi, i: (bi, 0, i, 0))
    row_spec = pl.BlockSpec((1, tm, LANE), lambda bi, i: (bi, i, 0))
    return pl.pallas_call(
        functools.partial(_l0_proj_kernel, latent=latent),
        grid=(b, l // tm),
        in_specs=in_specs,
        out_specs=[head_spec, head_spec, head_spec, head_spec, row_spec, row_spec],
        out_shape=[jax.ShapeDtypeStruct((b, HA, l, LANE), BF16),
                   jax.ShapeDtypeStruct((b, HA, l, LANE), kv_dtype),
                   jax.ShapeDtypeStruct((b, HA, l, LANE), kv_dtype),
                   jax.ShapeDtypeStruct((b, HB, l, LANE), BF16),
                   jax.ShapeDtypeStruct((b, l, LANE), F32),
                   jax.ShapeDtypeStruct((b, l, LANE), F32)],
        compiler_params=_cparams(("parallel", "parallel")),
        name="l0_proj",
    )(*args)


def _kvup_kernel(ckv_ref, kpe_ref, w_ref, kb_o, vb_o):
    kv = _mm(ckv_ref[0], w_ref[...])
    kpe = kpe_ref[0]
    for h in range(HB):
        kb_o[0, h] = (kv[:, h * LANE:(h + 1) * LANE] + kpe).astype(BF16)
        vb_o[0, h] = kv[:, 512 + h * LANE:512 + (h + 1) * LANE].astype(BF16)


def _kvup(ckv, kpe, w, tk):
    b, lk, _ = ckv.shape
    row_spec = pl.BlockSpec((1, tk, LANE), lambda bi, i: (bi, i, 0))
    head_spec = pl.BlockSpec((1, HB, tk, LANE), lambda bi, i: (bi, 0, i, 0))
    return pl.pallas_call(
        _kvup_kernel,
        grid=(b, lk // tk),
        in_specs=[row_spec, row_spec, _const_spec(w.shape)],
        out_specs=[head_spec, head_spec],
        out_shape=[jax.ShapeDtypeStruct((b, HB, lk, LANE), BF16)] * 2,
        compiler_params=_cparams(("parallel", "parallel")),
        name="kvup",
    )(ckv, kpe, w)


def _attn_kernel(*refs, diff, lambda_init, nsub):
    if diff:
        q_ref, k_ref, v_ref, lam_ref, g_ref, o_ref = refs
    else:
        q_ref, k_ref, v_ref, o_ref = refs
    k = k_ref[0, 0]
    v = v_ref[0, 0]
    ts = q_ref.shape[2] // nsub
    qs = []
    for i in range(nsub):
        q = q_ref[0, 0, i * ts:(i + 1) * ts, :]
        if diff:
            lo = _lane(q.shape) < DA
            zero = jnp.zeros_like(q)
            q = jnp.concatenate([jnp.where(lo, q, zero), jnp.where(lo, zero, q)], axis=0)
        qs.append(q)
    s = [_mm_nt(q, k) for q in qs]
    m = [jnp.max(x, axis=-1, keepdims=True) for x in s]
    p = [jnp.exp2(x - y) for x, y in zip(s, m)]
    inv = [1.0 / jnp.sum(x, axis=-1, keepdims=True) for x in p]
    o = [_mm(x, v) * y for x, y in zip(p, inv)]
    if diff:
        lv = lam_ref[...]
        lam = (jnp.exp(jnp.sum(lv[0:1] * lv[1:2], axis=-1, keepdims=True))
               - jnp.exp(jnp.sum(lv[2:3] * lv[3:4], axis=-1, keepdims=True)) + lambda_init)
        o = [_rms(x[:ts] - lam * x[ts:], g_ref[...], 2 * DA) * (1.0 - lambda_init) for x in o]
    for i in range(nsub):
        o_ref[0, i * ts:(i + 1) * ts, :] = o[i].astype(o_ref.dtype)


def _attention(q, k, v, tq, nsub, diff, lam_vecs=None, subln=None, lambda_init=0.0):
    b, h, l, _ = q.shape
    lk = k.shape[2]
    in_specs = [pl.BlockSpec((1, 1, tq, LANE), lambda bi, hi, i: (bi, hi, i, 0)),
                pl.BlockSpec((1, 1, lk, LANE), lambda bi, hi, i: (bi, hi, 0, 0)),
                pl.BlockSpec((1, 1, lk, LANE), lambda bi, hi, i: (bi, hi, 0, 0))]
    args = [q, k, v]
    if diff:
        in_specs += [_const_spec(lam_vecs.shape), _const_spec(subln.shape)]
        args += [lam_vecs, subln]
    return pl.pallas_call(
        functools.partial(_attn_kernel, diff=diff, lambda_init=lambda_init, nsub=nsub),
        grid=(b, h, l // tq),
        in_specs=in_specs,
        out_specs=pl.BlockSpec((1, tq, LANE), lambda bi, hi, i: (bi, i, hi)),
        out_shape=jax.ShapeDtypeStruct((b, l, h * LANE), BF16),
        compiler_params=_cparams(("parallel", "parallel", "parallel")),
        name="attn_diff" if diff else "attn_mla",
    )(*args)


def _oproj_kernel(h_ref, mod_ref, a_ref, b_ref, w_ref, o_ref):
    half = a_ref.shape[-1]
    m = _mm(a_ref[0], w_ref[0:half, :]) + _mm(b_ref[0], w_ref[half:, :])
    o_ref[0] = h_ref[0] + mod_ref[0][2:3] * m


def _oproj(h, mod, per_batch, oa, ob, w, tm):
    b, l, _ = h.shape
    mod_map = (lambda bi, i: (bi, 0, 0)) if per_batch else (lambda bi, i: (0, 0, 0))
    x_spec = pl.BlockSpec((1, tm, D_MODEL), lambda bi, i: (bi, i, 0))
    half_spec = pl.BlockSpec((1, tm, oa.shape[-1]), lambda bi, i: (bi, i, 0))
    return pl.pallas_call(
        _oproj_kernel,
        grid=(b, l // tm),
        in_specs=[x_spec, pl.BlockSpec((1, 6, D_MODEL), mod_map), half_spec, half_spec,
                  _const_spec(w.shape)],
        out_specs=x_spec,
        out_shape=jax.ShapeDtypeStruct(h.shape, F32),
        compiler_params=_cparams(("parallel", "parallel")),
        name="oproj",
    )(h, mod, oa, ob, w)


def _halo_rows(x_ref, xp_ref, xn_ref):
    i = pl.program_id(1)
    last = pl.num_programs(1) - 1
    xc = jnp.concatenate([xp_ref[0], x_ref[0], xn_ref[0]], axis=0)
    tm = x_ref.shape[1]
    r = _row((tm + 2 * HALO, 1))
    valid = jnp.logical_and(jnp.logical_or(r >= HALO, i > 0),
                            jnp.logical_or(r < tm + HALO, i < last))
    return xc, valid


def _shift_rows(u, tm):
    n = u.shape[0]
    up = pltpu.roll(u, 1, 0)[HALO:HALO + tm]
    dn = pltpu.roll(u, n - 1, 0)[HALO:HALO + tm]
    return up, u[HALO:HALO + tm], dn


def _ffn_kernel(x_ref, xp_ref, xn_ref, mod_ref, n2_ref, wu_ref, cw_ref, cb_ref, wd_ref,
                *rest, final, cw):
    if final:
        nf_ref, o_ref = rest
    else:
        (o_ref,) = rest
    tm = x_ref.shape[1]
    mod = mod_ref[0]
    xc, valid = _halo_rows(x_ref, xp_ref, xn_ref)
    hn = _rms(xc, n2_ref[...], D_MODEL) * (1.0 + mod[4:5]) + mod[3:4]
    hn = jnp.where(valid, hn, 0.0).astype(BF16)
    acc = jnp.zeros((tm, D_MODEL), F32)
    for c in range(F_FF // cw):
        halves = []
        for off in (c * cw, F_FF + c * cw):
            u = jnp.dot(hn, wu_ref[:, off:off + cw], preferred_element_type=F32)
            up, mid, dn = _shift_rows(u, tm)
            w3 = cw_ref[:, off:off + cw]
            halves.append(up * w3[0:1] + mid * w3[1:2] + dn * w3[2:3] + cb_ref[:, off:off + cw])
        act = (_silu(halves[0]) * halves[1]).astype(BF16)
        acc = acc + jnp.dot(act, wd_ref[c * cw:(c + 1) * cw, :], preferred_element_type=F32)
    out = x_ref[0] + mod[5:6] * acc
    if final:
        out = _rms(out, nf_ref[...], D_MODEL)
    o_ref[0] = out


def _ffn(h, mod, per_batch, n2, wu, cw3, cb, wd, nf, tm):
    b, l, _ = h.shape
    nblk = l // tm
    nb = tm // HALO
    mod_map = (lambda bi, i: (bi, 0, 0)) if per_batch else (lambda bi, i: (0, 0, 0))
    x_spec = pl.BlockSpec((1, tm, D_MODEL), lambda bi, i: (bi, i, 0))
    in_specs = [x_spec,
                pl.BlockSpec((1, HALO, D_MODEL), lambda bi, i: (bi, jnp.maximum(i * nb - 1, 0), 0)),
                pl.BlockSpec((1, HALO, D_MODEL),
                             lambda bi, i: (bi, jnp.minimum((i + 1) * nb, nblk * nb - 1), 0)),
                pl.BlockSpec((1, 6, D_MODEL), mod_map),
                _const_spec(n2.shape), _const_spec(wu.shape, True), _const_spec(cw3.shape),
                _const_spec(cb.shape), _const_spec(wd.shape, True)]
    args = [h, h, h, mod, n2, wu, cw3, cb, wd]
    final = nf is not None
    if final:
        in_specs.append(_const_spec(nf.shape))
        args.append(nf)
    cw = 2816
    return pl.pallas_call(
        functools.partial(_ffn_kernel, final=final, cw=cw),
        grid=(b, nblk),
        in_specs=in_specs,
        out_specs=x_spec,
        out_shape=jax.ShapeDtypeStruct(h.shape, F32),
        compiler_params=_cparams(("parallel", "parallel")),
        name="ffn",
    )(*args)


def _softplus(x):
    return jnp.maximum(x, 0.0) + jnp.log1p(jnp.exp(-jnp.abs(x)))


def _l1_proj_kernel(x_ref, xp_ref, xn_ref, mod_ref, n1_ref, w_ref, mu_ref, kk_ref, ka_ref,
                    w0_ref, w2_ref, a0_ref, a2_ref, g2_ref, cw_ref, cb_ref, dtb_ref, bd_ref,
                    r_o, k_o, v_o, kk_o, g_o, lw0_o, ai0_o, kx0_o, lw1_o, ai1_o, kx1_o,
                    zg_o, xs_o, bc_o, dt_o):
    tm = x_ref.shape[1]
    mod = mod_ref[0]
    xc, valid = _halo_rows(x_ref, xp_ref, xn_ref)
    hn = _rms(xc, n1_ref[...], D_MODEL) * (1.0 + mod[1:2]) + mod[0:1]
    hn = jnp.where(valid, hn, 0.0).astype(BF16)
    up, mid, dn = _shift_rows(jnp.dot(hn, w_ref[:, 0:1920], preferred_element_type=F32), tm)
    zc = mid + mu_ref[...] * (0.5 * (up + dn) - mid)
    r, k, v = zc[:, 0:512], zc[:, 512:1024], zc[:, 1024:1536]
    wd, ad, gd = zc[:, 1536:1664], zc[:, 1664:1792], zc[:, 1792:1920]
    r_o[0], k_o[0], v_o[0] = r, k, v
    kkf = k * kk_ref[...]
    kk_o[0] = kkf * lax.rsqrt(_mm_split(kkf * kkf, bd_ref[...]) + 1e-12)
    g_o[0] = _mm(jax.nn.sigmoid(gd), g2_ref[...])
    twd = jnp.tanh(wd)
    for i, (lw_o, ai_o, kx_o) in enumerate(((lw0_o, ai0_o, kx0_o), (lw1_o, ai1_o, kx1_o))):
        lw_o[0] = -WKV_DECAY_SCALE * jax.nn.sigmoid(w0_ref[i:i + 1] + _mm(twd, w2_ref[i]))
        a = jax.nn.sigmoid(a0_ref[i:i + 1] + _mm(ad, a2_ref[i]))
        ai_o[0] = a
        kx_o[0] = k * (1.0 + (a - 1.0) * ka_ref[...])
    zd = jnp.dot(hn, w_ref[:, 1920:3328], preferred_element_type=F32)
    zg_o[0] = zd[HALO:HALO + tm, 0:512]
    up, mid, dn = _shift_rows(zd[:, 512:1280], tm)
    cw3 = cw_ref[...]
    xbc = _silu(up * cw3[0:1] + mid * cw3[1:2] + dn * cw3[2:3] + cb_ref[...])
    xs_o[0] = xbc[:, 0:512]
    bc_o[0] = xbc[:, 512:768]
    dt_o[0] = _softplus(zd[HALO:HALO + tm, 1280:1408] + dtb_ref[...])


def _l1_proj(x, mod, per_batch, p, tm):
    b, l, _ = x.shape
    nblk = l // tm
    nb = tm // HALO
    mod_map = (lambda bi, i: (bi, 0, 0)) if per_batch else (lambda bi, i: (0, 0, 0))
    x_spec = pl.BlockSpec((1, tm, D_MODEL), lambda bi, i: (bi, i, 0))
    consts = [p["norm1"], p["w_in"], p["mu"], p["k_k"], p["k_a"], p["w0"], p["w2"], p["a0"],
              p["a2"], p["g2"], p["conv_w"], p["conv_b"], p["dt_bias"], p["bd"]]
    in_specs = [x_spec,
                pl.BlockSpec((1, HALO, D_MODEL), lambda bi, i: (bi, jnp.maximum(i * nb - 1, 0), 0)),
                pl.BlockSpec((1, HALO, D_MODEL),
                             lambda bi, i: (bi, jnp.minimum((i + 1) * nb, nblk * nb - 1), 0)),
                pl.BlockSpec((1, 6, D_MODEL), mod_map)] + [_const_spec(c.shape) for c in consts]

    def o_spec(w):
        return pl.BlockSpec((1, tm, w), lambda bi, i: (bi, i, 0))

    widths = [512] * 11 + [512, 512, 256, 128]
    return pl.pallas_call(
        _l1_proj_kernel,
        grid=(b, nblk),
        in_specs=in_specs,
        out_specs=[o_spec(w) for w in widths],
        out_shape=[jax.ShapeDtypeStruct((b, l, w), F32) for w in widths],
        compiler_params=_cparams(("parallel", "parallel")),
        name="l1_proj",
    )(x, x, x, mod, *consts)


def _tri(n, upper, strict=False):
    r, c = _row((n, n)), _lane((n, n))
    if upper:
        return (c > r) if strict else (c >= r)
    return (c < r) if strict else (c <= r)


def _wkv_step(dirs, st_ref):
    c = WKV_CHUNK
    npair = HC // 2
    ones = jnp.ones((c, LANE), F32)
    lane2 = _lane((2 * c, LANE))
    row2 = _row((2 * c, LANE))
    lo2 = lane2 < NC
    top2 = row2 < c
    bd2 = lo2 == top2
    tmask = row2 - jnp.where(top2, 0, c)
    imask = lane2 - jnp.where(lo2, 0, NC)
    blk16 = jnp.right_shift(tmask, 4) == jnp.right_shift(imask, 4)
    eye = jnp.logical_and(bd2, tmask == imask).astype(F32)
    lo1 = _lane((c, LANE)) < NC
    units = []
    for d, (r, kk, v, lw, a, kx, bwd) in enumerate(dirs):
        cum = _mm_f32(_tri(c, bwd).astype(F32), lw)
        last = cum[0:1] if bwd else cum[c - 1:c]
        e_in, e_out = jnp.exp(cum - lw), jnp.exp(cum)
        e_neg, e_end = jnp.exp(-cum), jnp.exp(last - cum)
        at, rt = -kk * e_in, r * e_out
        bb = kk * a
        bt, kt = bb * e_neg, kx * e_neg
        bh, kh = bb * e_end, kx * e_end
        wc = jnp.exp(lax.dot_general(lw, ones, (((0,), (0,)), ((), ())),
                                     precision=HIGHEST, preferred_element_type=F32))
        strict = (imask > tmask) if bwd else (imask < tmask)
        incl = ((imask >= tmask) if bwd else (imask <= tmask))[:c]
        for p in range(npair):
            sl = slice(p * LANE, (p + 1) * LANE)
            units.append(dict(
                d=d, p=p, strict=strict, incl=incl, v=v[:, sl], wc=wc[sl],
                l2=jnp.concatenate([at[:, sl], rt[:, sl]], axis=0).astype(BF16),
                bk=jnp.concatenate([bt[:, sl], kt[:, sl]], axis=0).astype(BF16),
                kb=jnp.concatenate([kt[:, sl], bt[:, sl]], axis=0).astype(BF16),
                hat=jnp.concatenate([bh[:, sl], kh[:, sl]], axis=0).astype(BF16)))
    zero = jnp.zeros((2 * c, LANE), BF16)
    for u in units:
        u["st"] = st_ref[u["d"], u["p"]]
    g_a = [_mm_nt(jnp.where(lo2, u["l2"], zero), u["bk"]) for u in units]
    g_b = [_mm_nt(jnp.where(lo2, zero, u["l2"]), u["kb"]) for u in units]
    ga = [jnp.where(u["strict"], jnp.concatenate([x[:c], y[:c]], axis=0), 0.0)
          for u, x, y in zip(units, g_a, g_b)]
    gr_a = [jnp.where(u["incl"], x[c:], 0.0).astype(BF16) for u, x in zip(units, g_a)]
    gr_b = [jnp.where(u["incl"], y[c:], 0.0).astype(BF16) for u, y in zip(units, g_b)]
    n_p = [jnp.where(bd2, x, 0.0) for x in ga]
    ak = [jnp.where(bd2, 0.0, x).astype(BF16) for x in ga]
    nd = [jnp.where(blk16, x, 0.0) for x in n_p]
    no = [(x - y).astype(BF16) for x, y in zip(n_p, nd)]
    ndb = [x.astype(BF16) for x in nd]
    n2 = [_mm(x, x).astype(BF16) for x in ndb]
    n4 = [_mm(x, x).astype(BF16) for x in n2]
    n8 = [_mm(x, x).astype(BF16) for x in n4]
    dinv = [eye + x for x in nd]
    dinv = [x + _mm(x, y) for x, y in zip(dinv, n2)]
    dinv = [x + _mm(x, y) for x, y in zip(dinv, n4)]
    dinv = [(x + _mm(x, y)).astype(BF16) for x, y in zip(dinv, n8)]
    m1 = [_mm(x, y) for x, y in zip(dinv, no)]
    m2 = [_mm(x, x) for x in m1]
    t_p = [eye + x for x in m1]
    t_p = [x + _mm(x, y) for x, y in zip(t_p, m2)]
    t_p = [_mm(x, y).astype(BF16) for x, y in zip(t_p, dinv)]
    akv = [_mm(x, jnp.concatenate([u["v"], u["v"]], axis=0)) for u, x in zip(units, ak)]
    base = [_mm(u["l2"], u["st"]) for u in units]
    us = [_mm(t, jnp.concatenate([b[:c], b[:c]], axis=0) + x) for t, b, x in zip(t_p, base, akv)]
    uu = [jnp.where(lo1, x[:c], x[c:]) for x in us]
    uv = [jnp.concatenate([x, u["v"]], axis=0).astype(BF16) for u, x in zip(units, uu)]
    vu = [jnp.concatenate([u["v"], x], axis=0).astype(BF16) for u, x in zip(units, uu)]
    ys = [b[c:] + jnp.where(lo1, _mm(x, p), _mm(y, q))
          for b, x, y, p, q in zip(base, gr_a, gr_b, uv, vu)]
    upd = [_mm_tn(u["hat"], x) for u, x in zip(units, uv)]
    for u, x in zip(units, upd):
        st_ref[u["d"], u["p"]] = u["st"] * u["wc"] + jnp.where(bd2, x, 0.0)
    return [jnp.concatenate(ys[d * npair:(d + 1) * npair], axis=1) for d in range(2)]


def _wkv_kernel(rf, kkf, vf, lw0, a0, kx0, rb, kkb, vb, lw1, a1, kx1, s0_ref,
                yf_o, yb_o, sfin_o, st_ref):
    j = pl.program_id(1)

    @pl.when(j == 0)
    def _():
        st_ref[...] = s0_ref[0]

    yf, yb = _wkv_step([(rf[0], kkf[0], vf[0], lw0[0], a0[0], kx0[0], False),
                        (rb[0], kkb[0], vb[0], lw1[0], a1[0], kx1[0], True)], st_ref)
    yf_o[0] = yf
    yb_o[0] = yb

    @pl.when(j == pl.num_programs(1) - 1)
    def _():
        sfin_o[0] = st_ref[...]


def _wkv(r, kk, v, lw0, a0, kx0, lw1, a1, kx1, s0):
    b, l, w = r.shape
    c = WKV_CHUNK
    nc = l // c
    fwd = pl.BlockSpec((1, c, w), lambda bi, j: (bi, j, 0))
    bwd = pl.BlockSpec((1, c, w), lambda bi, j: (bi, nc - 1 - j, 0))
    st_spec = pl.BlockSpec((1,) + s0.shape[1:], lambda bi, j: (bi, 0, 0, 0, 0))
    return pl.pallas_call(
        _wkv_kernel,
        grid=(b, nc),
        in_specs=[fwd] * 6 + [bwd] * 6 + [st_spec],
        out_specs=[fwd, bwd, st_spec],
        out_shape=[jax.ShapeDtypeStruct((b, l, w), F32), jax.ShapeDtypeStruct((b, l, w), F32),
                   jax.ShapeDtypeStruct(s0.shape, F32)],
        scratch_shapes=[pltpu.VMEM(s0.shape[1:], F32)],
        compiler_params=_cparams(("parallel", "arbitrary")),
        name="wkv",
    )(r, kk, v, lw0, a0, kx0, r, kk, v, lw1, a1, kx1, s0)


def _ssd_chunk(xs, bc, dt, arow, st_ref, d, bwd):
    c = xs.shape[0]
    o = HD * d
    acum = _mm_f32(_tri(c, bwd).astype(F32), dt * arow)
    acum_t = acum.T
    last = acum[0:1] if bwd else acum[c - 1:c]
    ea = jnp.exp(acum)
    de = jnp.exp(last - acum)
    cd = jnp.exp(last)
    bfull, cfull = bc[:, 0:LANE], bc[:, LANE:2 * LANE]
    lane = _lane((c, LANE))
    lo = lane < PD
    causal = _tri(c, bwd)
    zero = jnp.zeros_like(cfull)
    cb = [_mm_nt(jnp.where(lo, cfull, zero), bfull), _mm_nt(jnp.where(lo, zero, cfull), bfull)]
    grp_rows = _row((c, LANE)) < NS

    def colsel(m, ha, hb):
        return jnp.where(lo, jnp.broadcast_to(m[:, o + ha:o + ha + 1], (c, LANE)),
                         jnp.broadcast_to(m[:, o + hb:o + hb + 1], (c, LANE)))

    ys = []
    for p in range(HD // 2):
        g = p // 2
        ha, hb = 2 * p, 2 * p + 1
        xdt = xs[:, p * LANE:(p + 1) * LANE] * colsel(dt, ha, hb)
        yi = []
        for h in (ha, hb):
            seg = jnp.broadcast_to(acum[:, o + h:o + h + 1], (c, c)) - jnp.broadcast_to(acum_t[o + h:o + h + 1, :], (c, c))
            dec = jnp.where(causal, jnp.exp(jnp.where(causal, seg, 0.0)), 0.0)
            yi.append(_mm(cb[g] * dec, xdt))
        st = st_ref[d, p]
        y = jnp.where(lo, yi[0], yi[1]) + _mm(cfull, st) * colsel(ea, ha, hb)
        ys.append(y)
        cs = _mm_tn(bfull, xdt * colsel(de, ha, hb))
        cs = jnp.where(grp_rows if g == 0 else jnp.logical_not(grp_rows), cs, 0.0)
        cdp = jnp.where(lo[0:1], jnp.broadcast_to(cd[:, o + ha:o + ha + 1], (1, LANE)),
                        jnp.broadcast_to(cd[:, o + hb:o + hb + 1], (1, LANE)))
        st_ref[d, p] = st * cdp + cs
    return jnp.concatenate(ys, axis=1)


def _ssd_kernel(xf, bcf, dtf, xb, bcb, dtb, arow_ref, s0_ref, yf_o, yb_o, sfin_o, st_ref):
    j = pl.program_id(1)

    @pl.when(j == 0)
    def _():
        st_ref[...] = s0_ref[0]

    arow = arow_ref[...]
    yf_o[0] = _ssd_chunk(xf[0], bcf[0], dtf[0], arow, st_ref, 0, False)
    yb_o[0] = _ssd_chunk(xb[0], bcb[0], dtb[0], arow, st_ref, 1, True)

    @pl.when(j == pl.num_programs(1) - 1)
    def _():
        sfin_o[0] = st_ref[...]


def _ssd(xs, bc, dt, arow, s0):
    b, l, _ = xs.shape
    c = SSD_CHUNK
    nc = l // c

    def spec(w, rev):
        if rev:
            return pl.BlockSpec((1, c, w), lambda bi, j: (bi, nc - 1 - j, 0))
        return pl.BlockSpec((1, c, w), lambda bi, j: (bi, j, 0))

    st_spec = pl.BlockSpec((1,) + s0.shape[1:], lambda bi, j: (bi, 0, 0, 0, 0))
    return pl.pallas_call(
        _ssd_kernel,
        grid=(b, nc),
        in_specs=[spec(512, False), spec(256, False), spec(128, False),
                  spec(512, True), spec(256, True), spec(128, True),
                  _const_spec(arow.shape), st_spec],
        out_specs=[spec(512, False), spec(512, True), st_spec],
        out_shape=[jax.ShapeDtypeStruct((b, l, 512), F32), jax.ShapeDtypeStruct((b, l, 512), F32),
                   jax.ShapeDtypeStruct(s0.shape, F32)],
        scratch_shapes=[pltpu.VMEM(s0.shape[1:], F32)],
        compiler_params=_cparams(("parallel", "arbitrary")),
        name="ssd",
    )(xs, bc, dt, xs, bc, dt, arow, s0)


def _l1_post_kernel(h_ref, mod_ref, yf, yb, r, k, v, g, ydf, ydb, xs, zg, lnw, lnb, rk, dsum, gn,
                    w_ref, bd_ref, o_ref):
    bd = bd_ref[...]
    y = yf[0] + yb[0]
    mu = _mm_split(y, bd) * (1.0 / NC)
    dl = y - mu
    var = _mm_split(dl * dl, bd) * (1.0 / NC)
    y = dl * lax.rsqrt(var + WKV_LN_EPS) * lnw[...] + lnb[...]
    y = y + _mm_split(r[0] * k[0] * rk[...], bd) * v[0]
    oc = y * g[0]
    x = xs[0]
    yd = (ydf[0] + dsum[0:1] * x) + (ydb[0] + dsum[1:2] * x)
    od = _rms(yd * _silu(zg[0]), gn[...], D_INNER)
    m = _mm(oc, w_ref[0:512, :]) + _mm(od, w_ref[512:1024, :])
    o_ref[0] = h_ref[0] + mod_ref[0][2:3] * m


def _l1_post(h, mod, per_batch, acts, consts, tm):
    b, l, _ = h.shape
    mod_map = (lambda bi, i: (bi, 0, 0)) if per_batch else (lambda bi, i: (0, 0, 0))
    x_spec = pl.BlockSpec((1, tm, D_MODEL), lambda bi, i: (bi, i, 0))
    a_spec = pl.BlockSpec((1, tm, 512), lambda bi, i: (bi, i, 0))
    return pl.pallas_call(
        _l1_post_kernel,
        grid=(b, l // tm),
        in_specs=[x_spec, pl.BlockSpec((1, 6, D_MODEL), mod_map)] + [a_spec] * len(acts)
        + [_const_spec(c.shape) for c in consts],
        out_specs=x_spec,
        out_shape=jax.ShapeDtypeStruct(h.shape, F32),
        compiler_params=_cparams(("parallel", "parallel")),
        name="l1_post",
    )(h, mod, *acts, *consts)


def _rope_tables(length, dim, offset, width):
    quarter = dim // 4
    inv = ROPE_BASE ** (-jnp.arange(quarter, dtype=F32) / quarter)
    pos = jnp.arange(length)
    row = (pos // GRID_W).astype(F32)
    col = (pos % GRID_W).astype(F32)
    ar, ac = row[:, None] * inv, col[:, None] * inv
    c = jnp.concatenate([jnp.cos(ar), jnp.cos(ar), jnp.cos(ac), jnp.cos(ac)], axis=-1)
    s = jnp.concatenate([-jnp.sin(ar), jnp.sin(ar), -jnp.sin(ac), jnp.sin(ac)], axis=-1)
    cw = jnp.ones((length, width), F32).at[:, offset:offset + dim].set(c)
    sw = jnp.zeros((length, width), F32).at[:, offset:offset + dim].set(s)
    return cw, sw


def _pad_cols(w, n):
    return jnp.pad(w, ((0, 0), (0, n - w.shape[1])))


def _prep_l0(w_in, q_norm, w_uq, kv_norm, w_ukv, w_out):
    a_cols = 3 * HA * 2 * DA
    wqd = _pad_cols(w_in[:, a_cols:a_cols + Q_LORA], 256)
    wckv = w_in[:, a_cols + Q_LORA:a_cols + Q_LORA + KV_LORA]
    wkpe = jnp.pad(w_in[:, a_cols + Q_LORA + KV_LORA:], ((0, 0), (NOPE, LANE - NOPE - ROPE_B)))
    w = jnp.concatenate([w_in[:, :a_cols], wqd, wckv, wkpe], axis=1).astype(BF16)
    qn = _pad_cols(q_norm.reshape(1, -1), 256)
    wuq = jnp.pad(w_uq.reshape(Q_LORA, HB, NOPE + ROPE_B),
                  ((0, 256 - Q_LORA), (0, 0), (0, LANE - NOPE - ROPE_B))).reshape(256, HB * LANE)
    wukv = w_ukv.reshape(KV_LORA, HB, NOPE + VB)
    wk = jnp.pad(wukv[:, :, :NOPE], ((0, 0), (0, 0), (0, LANE - NOPE))).reshape(KV_LORA, HB * LANE)
    wv = wukv[:, :, NOPE:].reshape(KV_LORA, HB * VB)
    return dict(w=w, qn=qn, wuq=wuq.astype(BF16), kvn=kv_norm.reshape(1, -1),
                wukv=jnp.concatenate([wk, wv], axis=1).astype(BF16), w_out=w_out.astype(BF16))


def _prep_l1(w_in, mu, w0, w2, a0, a2, g2, k_k, k_a, r_k, ln_w, ln_b, conv_w, conv_b, A_log,
             dt_bias, D, gnorm, w_out, norm1):
    c_cols = 3 * HC * NC + 4 * 64 + 128
    conv_ch = D_INNER + 2 * G_SSM * NS
    wdt = w_in[:, c_cols + D_INNER + conv_ch:]
    wdt = _pad_cols(jnp.concatenate([wdt, wdt], axis=1), LANE)
    w = jnp.concatenate([w_in[:, :c_cols + D_INNER + conv_ch], wdt], axis=1).astype(BF16)
    z = jnp.zeros((64, HC * NC), F32)
    w2p = jnp.stack([jnp.concatenate([w2[0], z], 0), jnp.concatenate([z, w2[1]], 0)]).astype(BF16)
    a2p = jnp.stack([jnp.concatenate([a2[0], z], 0), jnp.concatenate([z, a2[1]], 0)]).astype(BF16)
    hid = jnp.arange(HC * NC) // NC
    bd = (hid[:, None] == hid[None, :]).astype(BF16)
    dtb = _pad_cols(dt_bias.reshape(1, 2 * HD), LANE)
    arow = _pad_cols((-jnp.exp(A_log.astype(F32))).reshape(1, 2 * HD), LANE)
    return dict(norm1=norm1.reshape(1, -1), w_in=w, mu=mu.reshape(1, -1), k_k=k_k.reshape(1, -1),
                k_a=k_a.reshape(1, -1), w0=w0, w2=w2p, a0=a0, a2=a2p, g2=g2.astype(BF16),
                conv_w=conv_w, conv_b=conv_b.reshape(1, -1), dt_bias=dtb, bd=bd, arow=arow,
                ln_w=ln_w.reshape(1, -1), ln_b=ln_b.reshape(1, -1), r_k=r_k.reshape(1, -1),
                dsum=jnp.repeat(D, PD, axis=1), gnorm=gnorm.reshape(1, -1),
                w_out=w_out.astype(BF16))


def _wkv_state_in(s):
    b = s.shape[0]
    st = jnp.swapaxes(s, -1, -2).reshape(b, HC // 2, 2, NC, NC)
    z = jnp.zeros_like(st[:, :, 0])
    top = jnp.concatenate([st[:, :, 0], z], axis=-1)
    bot = jnp.concatenate([z, st[:, :, 1]], axis=-1)
    return jnp.concatenate([top, bot], axis=-2)


def _wkv_state_out(s):
    a = s[:, :, :NC, :NC]
    b_ = s[:, :, NC:, NC:]
    st = jnp.stack([a, b_], axis=2).reshape(s.shape[0], HC, NC, NC)
    return jnp.swapaxes(st, -1, -2)


def _ssd_state_in(s):
    b = s.shape[0]
    st = jnp.swapaxes(s, -1, -2).reshape(b, HD // 2, 2, NS, PD)
    pair = jnp.concatenate([st[:, :, 0], st[:, :, 1]], axis=-1)
    z = jnp.zeros_like(pair)
    g0 = jnp.concatenate([pair, z], axis=-2)
    g1 = jnp.concatenate([z, pair], axis=-2)
    is_g0 = (jnp.arange(HD // 2) < HD // 4)[None, :, None, None]
    return jnp.where(is_g0, g0, g1)


def _ssd_state_out(s):
    is_g0 = (jnp.arange(HD // 2) < HD // 4)[None, :, None, None]
    pair = jnp.where(is_g0, s[:, :, :NS, :], s[:, :, NS:, :])
    st = jnp.stack([pair[..., :PD], pair[..., PD:]], axis=2).reshape(s.shape[0], HD, NS, PD)
    return jnp.swapaxes(st, -1, -2)


def _layer0(h, mod, per_batch, p, norm1, ctx, tables, lam_vecs, subln, lambda_init, tm, tq):
    qa, ka, va, qb, ckv, kpe = _l0_proj(h, mod, per_batch, norm1, p["w"], p["qn"], p["wuq"],
                                        p["kvn"], tables, tm)
    own = (ka, va, ckv, kpe)
    if ctx is not None:
        ck, cv, cckv, ckpe = ctx
        ka = jnp.concatenate([ck.astype(BF16), ka], axis=2)
        va = jnp.concatenate([cv.astype(BF16), va], axis=2)
        ckv = jnp.concatenate([cckv, ckv], axis=1)
        kpe = jnp.concatenate([jnp.pad(ckpe, ((0, 0), (0, 0), (NOPE, LANE - NOPE - ROPE_B))), kpe],
                              axis=1)
    kb, vb = _kvup(ckv, kpe, p["wukv"], min(512, ckv.shape[1]))
    nsub = tq // 128
    oa = _attention(qa, ka, va, tq, nsub, True, lam_vecs, subln, lambda_init)
    ob = _attention(qb, kb, vb, tq, nsub, False)
    return _oproj(h, mod, per_batch, oa, ob, p["w_out"], tm), own


def _layer1(h, mod, per_batch, p, wkv0, ssm0, tm):
    (r, k, v, kk, g, lw0, a0, kx0, lw1, a1, kx1, zg, xs, bc, dt) = _l1_proj(h, mod, per_batch, p, tm)
    yf, yb, s_wkv = _wkv(r, kk, v, lw0, a0, kx0, lw1, a1, kx1, wkv0)
    ydf, ydb, s_ssm = _ssd(xs, bc, dt, p["arow"], ssm0)
    consts = [p["ln_w"], p["ln_b"], p["r_k"], p["dsum"], p["gnorm"], p["w_out"], p["bd"]]
    h = _l1_post(h, mod, per_batch, [yf, yb, r, k, v, g, ydf, ydb, xs, zg], consts, tm)
    return h, s_wkv, s_ssm


def kernel(x_prompt, x_sample, cache_l0_k, cache_l0_v, cache_l0_ckv, cache_l0_kpe, state_l1_wkv_fwd, state_l1_wkv_bwd, state_l1_ssm_fwd, state_l1_ssm_bwd, c, c_ctx, ada_w_0, ada_b_0, norm1_0, norm2_0, ffn_up_0, ffn_conv_w_0, ffn_conv_b_0, ffn_down_0, l0_w_in, l0_lambda_q1, l0_lambda_k1, l0_lambda_q2, l0_lambda_k2, l0_subln, l0_q_norm, l0_w_uq, l0_kv_norm, l0_w_ukv, l0_w_out, ada_w_1, ada_b_1, norm1_1, norm2_1, ffn_up_1, ffn_conv_w_1, ffn_conv_b_1, ffn_down_1, l1_w_in, l1_mu, l1_w0, l1_w2, l1_a0, l1_a2, l1_g2, l1_k_k, l1_k_a, l1_r_k, l1_ln_w, l1_ln_b, l1_conv_w, l1_conv_b, l1_A_log, l1_dt_bias, l1_D, l1_gnorm, l1_w_out, norm_f):
    bc_, lc, _ = x_prompt.shape
    bl, ll, _ = x_sample.shape
    tm_c, tm_l = min(256, lc), min(256, ll)
    cond = jnp.concatenate([c_ctx[None, :], c], axis=0)
    cond8 = jnp.pad(cond, ((0, 8 - cond.shape[0]), (0, 0)))
    mods = [_ada(cond8, w, b).reshape(8, 6, D_MODEL) for w, b in ((ada_w_0, ada_b_0), (ada_w_1, ada_b_1))]
    p0 = _prep_l0(l0_w_in, l0_q_norm, l0_w_uq, l0_kv_norm, l0_w_ukv, l0_w_out)
    tables = _rope_tables(ll, DA, 0, DA)
    tables = tuple(jnp.concatenate([t, t], axis=1) for t in tables) + _rope_tables(ll, ROPE_B, NOPE, LANE)
    lam_vecs = jnp.pad(jnp.stack([l0_lambda_q1, l0_lambda_k1, l0_lambda_q2, l0_lambda_k2]),
                       ((0, 4), (0, LANE - DA)))
    subln = l0_subln.reshape(1, -1)
    lambda_init = 0.8 - 0.6 * math.exp(-0.3 * 0)
    n1 = norm1_0.reshape(1, -1)
    h_ctx, own = _layer0(x_prompt, mods[0][0:1], False, p0, n1, None, None, lam_vecs, subln,
                         lambda_init, tm_c, min(256, lc))
    h_lat, _ = _layer0(x_sample, mods[0][1:1 + bl], True, p0, n1,
                       (cache_l0_k, cache_l0_v, cache_l0_ckv, cache_l0_kpe), tables, lam_vecs,
                       subln, lambda_init, tm_l, min(256, ll))
    ffn0 = (norm2_0.reshape(1, -1), ffn_up_0.astype(BF16), ffn_conv_w_0, ffn_conv_b_0.reshape(1, -1),
            ffn_down_0.astype(BF16))
    h_ctx = _ffn(h_ctx, mods[0][0:1], False, *ffn0, None, tm_c)
    h_lat = _ffn(h_lat, mods[0][1:1 + bl], True, *ffn0, None, min(512, ll))
    p1 = _prep_l1(l1_w_in, l1_mu, l1_w0, l1_w2, l1_a0, l1_a2, l1_g2, l1_k_k, l1_k_a, l1_r_k,
                  l1_ln_w, l1_ln_b, l1_conv_w, l1_conv_b, l1_A_log, l1_dt_bias, l1_D, l1_gnorm,
                  l1_w_out, norm1_1)
    zero_state = jnp.zeros((bc_, 2, HC // 2, LANE, LANE), F32)
    h_ctx, s_wkv, s_ssm = _layer1(h_ctx, mods[1][0:1], False, p1, zero_state, zero_state, tm_c)
    wkv0 = jnp.stack([_wkv_state_in(state_l1_wkv_fwd), _wkv_state_in(state_l1_wkv_bwd)], axis=1)
    ssm0 = jnp.stack([_ssd_state_in(state_l1_ssm_fwd), _ssd_state_in(state_l1_ssm_bwd)], axis=1)
    h_lat, _, _ = _layer1(h_lat, mods[1][1:1 + bl], True, p1, wkv0, ssm0, tm_l)
    ffn1 = (norm2_1.reshape(1, -1), ffn_up_1.astype(BF16), ffn_conv_w_1, ffn_conv_b_1.reshape(1, -1),
            ffn_down_1.astype(BF16))
    nf = norm_f.reshape(1, -1)
    y_prompt = _ffn(h_ctx, mods[1][0:1], False, *ffn1, nf, tm_c)
    y_sample = _ffn(h_lat, mods[1][1:1 + bl], True, *ffn1, nf, min(512, ll))
    ka, va, ckv, kpe = own
    return (y_prompt, y_sample, ka, va, ckv, kpe[:, :, NOPE:NOPE + ROPE_B],
            _wkv_state_out(s_wkv[:, 0]), _wkv_state_out(s_wkv[:, 1]),
            _ssd_state_out(s_ssm[:, 0]), _ssd_state_out(s_ssm[:, 1]))
```

```python
import functools
import math

import jax
import jax.numpy as jnp
import numpy as np
from jax import lax
from jax.experimental import pallas as pl
from jax.experimental.pallas import tpu as pltpu

F32 = jnp.float32
BF16 = jnp.bfloat16
HIGHEST = lax.Precision.HIGHEST

D_MODEL = 1024
GRID_W = 64
ROPE_BASE = 10000.0
NORM_EPS = 1e-6
HA, DA = 4, 64
HB, NOPE, ROPE_B, VB = 4, 64, 32, 128
Q_LORA, KV_LORA = 192, 128
HC, NC = 8, 64
WKV_LN_EPS = 64e-5
WKV_DECAY_SCALE = 0.606531
HD, PD, G_SSM, NS = 8, 64, 2, 64
D_INNER = HD * PD
F_FF = 2816
LANE = 128
HALO = 8
LOG2E = math.log2(math.e)
WKV_CHUNK = 64
SSD_CHUNK = 128
VMEM_LIMIT = 56 * 1024 * 1024


def _cparams(sem):
    return pltpu.CompilerParams(dimension_semantics=sem, vmem_limit_bytes=VMEM_LIMIT)


def _mm(a, b):
    return jnp.dot(a.astype(BF16), b.astype(BF16), preferred_element_type=F32)


def _mm_nt(a, b):
    return lax.dot_general(a.astype(BF16), b.astype(BF16), (((1,), (1,)), ((), ())),
                           preferred_element_type=F32)


def _mm_tn(a, b):
    return lax.dot_general(a.astype(BF16), b.astype(BF16), (((0,), (0,)), ((), ())),
                           preferred_element_type=F32)


def _mm_f32(a, b):
    return jnp.dot(a, b, precision=HIGHEST, preferred_element_type=F32)


def _mm_split(x, m):
    hi = x.astype(BF16)
    lo = (x - hi.astype(F32)).astype(BF16)
    return (jnp.dot(hi, m, preferred_element_type=F32) + jnp.dot(lo, m, preferred_element_type=F32))


def _rms(x, g, n):
    ms = jnp.sum(x * x, axis=-1, keepdims=True) * (1.0 / n)
    return x * lax.rsqrt(ms + NORM_EPS) * g


def _silu(x):
    return x * jax.nn.sigmoid(x)


def _lane(shape):
    return lax.broadcasted_iota(jnp.int32, shape, len(shape) - 1)


def _row(shape):
    return lax.broadcasted_iota(jnp.int32, shape, len(shape) - 2)


def _const_spec(shape, single=False):
    nd = len(shape)
    if single:
        return pl.BlockSpec(shape, lambda *_: (0,) * nd, pipeline_mode=pl.Buffered(1))
    return pl.BlockSpec(shape, lambda *_: (0,) * nd)


def _ada_kernel(c_ref, w_ref, b_ref, o_ref):
    o_ref[...] = _mm(_silu(c_ref[...]), w_ref[...]) + b_ref[...]


def _ada(cond8, w, b):
    n = w.shape[1]
    tn = 1536
    return pl.pallas_call(
        _ada_kernel,
        grid=(n // tn,),
        in_specs=[_const_spec((8, D_MODEL)),
                  pl.BlockSpec((D_MODEL, tn), lambda j: (0, j)),
                  pl.BlockSpec((1, tn), lambda j: (0, j))],
        out_specs=pl.BlockSpec((8, tn), lambda j: (0, j)),
        out_shape=jax.ShapeDtypeStruct((8, n), F32),
        compiler_params=_cparams(("arbitrary",)),
        name="ada",
    )(cond8, w, b.reshape(1, n))


def _rope(x, c, s, half):
    w = x.shape[-1]
    up = pltpu.roll(x, w - half, 1)
    dn = pltpu.roll(x, half, 1)
    first = (_lane(x.shape) & (2 * half - 1)) < half
    return x * c + jnp.where(first, up, dn) * s


def _l0_proj_kernel(*refs, latent):
    if latent:
        (x_ref, mod_ref, n1_ref, w_ref, qn_ref, wuq_ref, kvn_ref, ca_ref, sa_ref, cb_ref, sb_ref,
         qa_o, ka_o, va_o, qb_o, ckv_o, kpe_o) = refs
    else:
        (x_ref, mod_ref, n1_ref, w_ref, qn_ref, wuq_ref, kvn_ref,
         qa_o, ka_o, va_o, qb_o, ckv_o, kpe_o) = refs
    mod = mod_ref[0]
    hn = _rms(x_ref[0], n1_ref[...], D_MODEL) * (1.0 + mod[1:2]) + mod[0:1]
    proj = _mm(hn, w_ref[...])
    qd = _rms(proj[:, 1536:1792], qn_ref[...], Q_LORA)
    qb = _mm(qd, wuq_ref[...])
    ckv_o[0] = _rms(proj[:, 1792:1920], kvn_ref[...], KV_LORA)
    kpe = proj[:, 1920:2048]
    if latent:
        ca, sa, cb, sb = ca_ref[...], sa_ref[...], cb_ref[...], sb_ref[...]
        kpe = _rope(kpe, cb, sb, ROPE_B // 4)
    kpe_o[0] = kpe
    for h in range(HA):
        sl = slice(h * LANE, (h + 1) * LANE)
        q = proj[:, sl]
        k = proj[:, 512 + h * LANE:512 + (h + 1) * LANE]
        qbh = qb[:, sl]
        if latent:
            q = _rope(q, ca, sa, DA // 4)
            k = _rope(k, ca, sa, DA // 4)
            qbh = _rope(qbh, cb, sb, ROPE_B // 4)
        qa_o[0, h] = (q * (DA ** -0.5 * LOG2E)).astype(qa_o.dtype)
        ka_o[0, h] = k.astype(ka_o.dtype)
        va_o[0, h] = proj[:, 1024 + h * LANE:1024 + (h + 1) * LANE].astype(va_o.dtype)
        qb_o[0, h] = (qbh * ((NOPE + ROPE_B) ** -0.5 * LOG2E)).astype(qb_o.dtype)


def _l0_proj(x, mod, per_batch, n1, w, qn, wuq, kvn, tables, tm):
    b, l, _ = x.shape
    latent = tables is not None
    kv_dtype = BF16 if latent else F32
    mod_map = (lambda bi, i: (bi, 0, 0)) if per_batch else (lambda bi, i: (0, 0, 0))
    in_specs = [pl.BlockSpec((1, tm, D_MODEL), lambda bi, i: (bi, i, 0)),
                pl.BlockSpec((1, 6, D_MODEL), mod_map),
                _const_spec((1, D_MODEL)), _const_spec(w.shape), _const_spec(qn.shape),
                _const_spec(wuq.shape), _const_spec(kvn.shape)]
    args = [x, mod, n1, w, qn, wuq, kvn]
    if latent:
        in_specs += [pl.BlockSpec((tm, LANE), lambda bi, i: (i, 0))] * 4
        args += list(tables)
    head_spec = pl.BlockSpec((1, HA, tm, LANE), lambda bi, i: (bi, 0, i, 0))
    row_spec = pl.BlockSpec((1, tm, LANE), lambda bi, i: (bi, i, 0))
    return pl.pallas_call(
        functools.partial(_l0_proj_kernel, latent=latent),
        grid=(b, l // tm),
        in_specs=in_specs,
        out_specs=[head_spec, head_spec, head_spec, head_spec, row_spec, row_spec],
        out_shape=[jax.ShapeDtypeStruct((b, HA, l, LANE), BF16),
                   jax.ShapeDtypeStruct((b, HA, l, LANE), kv_dtype),
                   jax.ShapeDtypeStruct((b, HA, l, LANE), kv_dtype),
                   jax.ShapeDtypeStruct((b, HB, l, LANE), BF16),
                   jax.ShapeDtypeStruct((b, l, LANE), F32),
                   jax.ShapeDtypeStruct((b, l, LANE), F32)],
        compiler_params=_cparams(("parallel", "parallel")),
        name="l0_proj",
    )(*args)


def _kvup_kernel(ckv_ref, kpe_ref, w_ref, kb_o, vb_o):
    kv = _mm(ckv_ref[0], w_ref[...])
    kpe = kpe_ref[0]
    for h in range(HB):
        kb_o[0, h] = (kv[:, h * LANE:(h + 1) * LANE] + kpe).astype(BF16)
        vb_o[0, h] = kv[:, 512 + h * LANE:512 + (h + 1) * LANE].astype(BF16)


def _kvup(ckv, kpe, w, tk):
    b, lk, _ = ckv.shape
    row_spec = pl.BlockSpec((1, tk, LANE), lambda bi, i: (bi, i, 0))
    head_spec = pl.BlockSpec((1, HB, tk, LANE), lambda bi, i: (bi, 0, i, 0))
    return pl.pallas_call(
        _kvup_kernel,
        grid=(b, lk // tk),
        in_specs=[row_spec, row_spec, _const_spec(w.shape)],
        out_specs=[head_spec, head_spec],
        out_shape=[jax.ShapeDtypeStruct((b, HB, lk, LANE), BF16)] * 2,
        compiler_params=_cparams(("parallel", "parallel")),
        name="kvup",
    )(ckv, kpe, w)


def _attn_kernel(*refs, diff, lambda_init, nsub):
    if diff:
        q_ref, k_ref, v_ref, lam_ref, g_ref, o_ref = refs
    else:
        q_ref, k_ref, v_ref, o_ref = refs
    k = k_ref[0, 0]
    v = v_ref[0, 0]
    ts = q_ref.shape[2] // nsub
    qs = []
    for i in range(nsub):
        q = q_ref[0, 0, i * ts:(i + 1) * ts, :]
        if diff:
            lo = _lane(q.shape) < DA
            zero = jnp.zeros_like(q)
            q = jnp.concatenate([jnp.where(lo, q, zero), jnp.where(lo, zero, q)], axis=0)
        qs.append(q)
    s = [_mm_nt(q, k) for q in qs]
    m = [jnp.max(x, axis=-1, keepdims=True) for x in s]
    p = [jnp.exp2(x - y) for x, y in zip(s, m)]
    inv = [1.0 / jnp.sum(x, axis=-1, keepdims=True) for x in p]
    o = [_mm(x, v) * y for x, y in zip(p, inv)]
    if diff:
        lv = lam_ref[...]
        lam = (jnp.exp(jnp.sum(lv[0:1] * lv[1:2], axis=-1, keepdims=True))
               - jnp.exp(jnp.sum(lv[2:3] * lv[3:4], axis=-1, keepdims=True)) + lambda_init)
        o = [_rms(x[:ts] - lam * x[ts:], g_ref[...], 2 * DA) * (1.0 - lambda_init) for x in o]
    for i in range(nsub):
        o_ref[0, i * ts:(i + 1) * ts, :] = o[i].astype(o_ref.dtype)


def _attention(q, k, v, tq, nsub, diff, lam_vecs=None, subln=None, lambda_init=0.0):
    b, h, l, _ = q.shape
    lk = k.shape[2]
    in_specs = [pl.BlockSpec((1, 1, tq, LANE), lambda bi, hi, i: (bi, hi, i, 0)),
                pl.BlockSpec((1, 1, lk, LANE), lambda bi, hi, i: (bi, hi, 0, 0)),
                pl.BlockSpec((1, 1, lk, LANE), lambda bi, hi, i: (bi, hi, 0, 0))]
    args = [q, k, v]
    if diff:
        in_specs += [_const_spec(lam_vecs.shape), _const_spec(subln.shape)]
        args += [lam_vecs, subln]
    return pl.pallas_call(
        functools.partial(_attn_kernel, diff=diff, lambda_init=lambda_init, nsub=nsub),
        grid=(b, h, l // tq),
        in_specs=in_specs,
        out_specs=pl.BlockSpec((1, tq, LANE), lambda bi, hi, i: (bi, i, hi)),
        out_shape=jax.ShapeDtypeStruct((b, l, h * LANE), BF16),
        compiler_params=_cparams(("parallel", "parallel", "parallel")),
        name="attn_diff" if diff else "attn_mla",
    )(*args)


def _oproj_kernel(h_ref, mod_ref, a_ref, b_ref, w_ref, o_ref):
    half = a_ref.shape[-1]
    m = _mm(a_ref[0], w_ref[0:half, :]) + _mm(b_ref[0], w_ref[half:, :])
    o_ref[0] = h_ref[0] + mod_ref[0][2:3] * m


def _oproj(h, mod, per_batch, oa, ob, w, tm):
    b, l, _ = h.shape
    mod_map = (lambda bi, i: (bi, 0, 0)) if per_batch else (lambda bi, i: (0, 0, 0))
    x_spec = pl.BlockSpec((1, tm, D_MODEL), lambda bi, i: (bi, i, 0))
    half_spec = pl.BlockSpec((1, tm, oa.shape[-1]), lambda bi, i: (bi, i, 0))
    return pl.pallas_call(
        _oproj_kernel,
        grid=(b, l // tm),
        in_specs=[x_spec, pl.BlockSpec((1, 6, D_MODEL), mod_map), half_spec, half_spec,
                  _const_spec(w.shape)],
        out_specs=x_spec,
        out_shape=jax.ShapeDtypeStruct(h.shape, F32),
        compiler_params=_cparams(("parallel", "parallel")),
        name="oproj",
    )(h, mod, oa, ob, w)


def _halo_rows(x_ref, xp_ref, xn_ref):
    i = pl.program_id(1)
    last = pl.num_programs(1) - 1
    xc = jnp.concatenate([xp_ref[0], x_ref[0], xn_ref[0]], axis=0)
    tm = x_ref.shape[1]
    r = _row((tm + 2 * HALO, 1))
    valid = jnp.logical_and(jnp.logical_or(r >= HALO, i > 0),
                            jnp.logical_or(r < tm + HALO, i < last))
    return xc, valid


def _shift_rows(u, tm):
    n = u.shape[0]
    up = pltpu.roll(u, 1, 0)[HALO:HALO + tm]
    dn = pltpu.roll(u, n - 1, 0)[HALO:HALO + tm]
    return up, u[HALO:HALO + tm], dn


def _ffn_kernel(x_ref, xp_ref, xn_ref, mod_ref, n2_ref, wu_ref, cw_ref, cb_ref, wd_ref,
                *rest, final, cw):
    if final:
        nf_ref, o_ref = rest
    else:
        (o_ref,) = rest
    tm = x_ref.shape[1]
    mod = mod_ref[0]
    xc, valid = _halo_rows(x_ref, xp_ref, xn_ref)
    hn = _rms(xc, n2_ref[...], D_MODEL) * (1.0 + mod[4:5]) + mod[3:4]
    hn = jnp.where(valid, hn, 0.0).astype(BF16)
    acc = jnp.zeros((tm, D_MODEL), F32)
    for c in range(F_FF // cw):
        halves = []
        for off in (c * cw, F_FF + c * cw):
            u = jnp.dot(hn, wu_ref[:, off:off + cw], preferred_element_type=F32)
            up, mid, dn = _shift_rows(u, tm)
            w3 = cw_ref[:, off:off + cw]
            halves.append(up * w3[0:1] + mid * w3[1:2] + dn * w3[2:3] + cb_ref[:, off:off + cw])
        act = (_silu(halves[0]) * halves[1]).astype(BF16)
        acc = acc + jnp.dot(act, wd_ref[c * cw:(c + 1) * cw, :], preferred_element_type=F32)
    out = x_ref[0] + mod[5:6] * acc
    if final:
        out = _rms(out, nf_ref[...], D_MODEL)
    o_ref[0] = out


def _ffn(h, mod, per_batch, n2, wu, cw3, cb, wd, nf, tm):
    b, l, _ = h.shape
    nblk = l // tm
    nb = tm // HALO
    mod_map = (lambda bi, i: (bi, 0, 0)) if per_batch else (lambda bi, i: (0, 0, 0))
    x_spec = pl.BlockSpec((1, tm, D_MODEL), lambda bi, i: (bi, i, 0))
    in_specs = [x_spec,
                pl.BlockSpec((1, HALO, D_MODEL), lambda bi, i: (bi, jnp.maximum(i * nb - 1, 0), 0)),
                pl.BlockSpec((1, HALO, D_MODEL),
                             lambda bi, i: (bi, jnp.minimum((i + 1) * nb, nblk * nb - 1), 0)),
                pl.BlockSpec((1, 6, D_MODEL), mod_map),
                _const_spec(n2.shape), _const_spec(wu.shape, True), _const_spec(cw3.shape),
                _const_spec(cb.shape), _const_spec(wd.shape, True)]
    args = [h, h, h, mod, n2, wu, cw3, cb, wd]
    final = nf is not None
    if final:
        in_specs.append(_const_spec(nf.shape))
        args.append(nf)
    cw = 2816
    return pl.pallas_call(
        functools.partial(_ffn_kernel, final=final, cw=cw),
        grid=(b, nblk),
        in_specs=in_specs,
        out_specs=x_spec,
        out_shape=jax.ShapeDtypeStruct(h.shape, F32),
        compiler_params=_cparams(("parallel", "parallel")),
        name="ffn",
    )(*args)


def _softplus(x):
    return jnp.maximum(x, 0.0) + jnp.log1p(jnp.exp(-jnp.abs(x)))


def _l1_proj_kernel(x_ref, xp_ref, xn_ref, mod_ref, n1_ref, w_ref, mu_ref, kk_ref, ka_ref,
                    w0_ref, w2_ref, a0_ref, a2_ref, g2_ref, cw_ref, cb_ref, dtb_ref, bd_ref,
                    r_o, k_o, v_o, kk_o, g_o, lw0_o, ai0_o, kx0_o, lw1_o, ai1_o, kx1_o,
                    zg_o, xs_o, bc_o, dt_o):
    tm = x_ref.shape[1]
    mod = mod_ref[0]
    xc, valid = _halo_rows(x_ref, xp_ref, xn_ref)
    hn = _rms(xc, n1_ref[...], D_MODEL) * (1.0 + mod[1:2]) + mod[0:1]
    hn = jnp.where(valid, hn, 0.0).astype(BF16)
    up, mid, dn = _shift_rows(jnp.dot(hn, w_ref[:, 0:1920], preferred_element_type=F32), tm)
    zc = mid + mu_ref[...] * (0.5 * (up + dn) - mid)
    r, k, v = zc[:, 0:512], zc[:, 512:1024], zc[:, 1024:1536]
    wd, ad, gd = zc[:, 1536:1664], zc[:, 1664:1792], zc[:, 1792:1920]
    r_o[0], k_o[0], v_o[0] = r, k, v
    kkf = k * kk_ref[...]
    kk_o[0] = kkf * lax.rsqrt(_mm_split(kkf * kkf, bd_ref[...]) + 1e-12)
    g_o[0] = _mm(jax.nn.sigmoid(gd), g2_ref[...])
    twd = jnp.tanh(wd)
    for i, (lw_o, ai_o, kx_o) in enumerate(((lw0_o, ai0_o, kx0_o), (lw1_o, ai1_o, kx1_o))):
        lw_o[0] = -WKV_DECAY_SCALE * jax.nn.sigmoid(w0_ref[i:i + 1] + _mm(twd, w2_ref[i]))
        a = jax.nn.sigmoid(a0_ref[i:i + 1] + _mm(ad, a2_ref[i]))
        ai_o[0] = a
        kx_o[0] = k * (1.0 + (a - 1.0) * ka_ref[...])
    zd = jnp.dot(hn, w_ref[:, 1920:3328], preferred_element_type=F32)
    zg_o[0] = zd[HALO:HALO + tm, 0:512]
    up, mid, dn = _shift_rows(zd[:, 512:1280], tm)
    cw3 = cw_ref[...]
    xbc = _silu(up * cw3[0:1] + mid * cw3[1:2] + dn * cw3[2:3] + cb_ref[...])
    xs_o[0] = xbc[:, 0:512]
    bc_o[0] = xbc[:, 512:768]
    dt_o[0] = _softplus(zd[HALO:HALO + tm, 1280:1408] + dtb_ref[...])


def _l1_proj(x, mod, per_batch, p, tm):
    b, l, _ = x.shape
    nblk = l // tm
    nb = tm // HALO
    mod_map = (lambda bi, i: (bi, 0, 0)) if per_batch else (lambda bi, i: (0, 0, 0))
    x_spec = pl.BlockSpec((1, tm, D_MODEL), lambda bi, i: (bi, i, 0))
    consts = [p["norm1"], p["w_in"], p["mu"], p["k_k"], p["k_a"], p["w0"], p["w2"], p["a0"],
              p["a2"], p["g2"], p["conv_w"], p["conv_b"], p["dt_bias"], p["bd"]]
    in_specs = [x_spec,
                pl.BlockSpec((1, HALO, D_MODEL), lambda bi, i: (bi, jnp.maximum(i * nb - 1, 0), 0)),
                pl.BlockSpec((1, HALO, D_MODEL),
                             lambda bi, i: (bi, jnp.minimum((i + 1) * nb, nblk * nb - 1), 0)),
                pl.BlockSpec((1, 6, D_MODEL), mod_map)] + [_const_spec(c.shape) for c in consts]

    def o_spec(w):
        return pl.BlockSpec((1, tm, w), lambda bi, i: (bi, i, 0))

    widths = [512] * 11 + [512, 512, 256, 128]
    return pl.pallas_call(
        _l1_proj_kernel,
        grid=(b, nblk),
        in_specs=in_specs,
        out_specs=[o_spec(w) for w in widths],
        out_shape=[jax.ShapeDtypeStruct((b, l, w), F32) for w in widths],
        compiler_params=_cparams(("parallel", "parallel")),
        name="l1_proj",
    )(x, x, x, mod, *consts)


def _tri(n, upper, strict=False):
    r, c = _row((n, n)), _lane((n, n))
    if upper:
        return (c > r) if strict else (c >= r)
    return (c < r) if strict else (c <= r)


def _wkv_step(dirs, st_ref):
    c = WKV_CHUNK
    npair = HC // 2
    ones = jnp.ones((c, LANE), F32)
    lane2 = _lane((2 * c, LANE))
    row2 = _row((2 * c, LANE))
    lo2 = lane2 < NC
    top2 = row2 < c
    bd2 = lo2 == top2
    tmask = row2 - jnp.where(top2, 0, c)
    imask = lane2 - jnp.where(lo2, 0, NC)
    blk16 = jnp.right_shift(tmask, 4) == jnp.right_shift(imask, 4)
    eye = jnp.logical_and(bd2, tmask == imask).astype(F32)
    lo1 = _lane((c, LANE)) < NC
    units = []
    for d, (r, kk, v, lw, a, kx, bwd) in enumerate(dirs):
        cum = _mm_f32(_tri(c, bwd).astype(F32), lw)
        last = cum[0:1] if bwd else cum[c - 1:c]
        e_in, e_out = jnp.exp(cum - lw), jnp.exp(cum)
        e_neg, e_end = jnp.exp(-cum), jnp.exp(last - cum)
        at, rt = -kk * e_in, r * e_out
        bb = kk * a
        bt, kt = bb * e_neg, kx * e_neg
        bh, kh = bb * e_end, kx * e_end
        wc = jnp.exp(lax.dot_general(lw, ones, (((0,), (0,)), ((), ())),
                                     precision=HIGHEST, preferred_element_type=F32))
        strict = (imask > tmask) if bwd else (imask < tmask)
        incl = ((imask >= tmask) if bwd else (imask <= tmask))[:c]
        for p in range(npair):
            sl = slice(p * LANE, (p + 1) * LANE)
            units.append(dict(
                d=d, p=p, strict=strict, incl=incl, v=v[:, sl], wc=wc[sl],
                l2=jnp.concatenate([at[:, sl], rt[:, sl]], axis=0).astype(BF16),
                bk=jnp.concatenate([bt[:, sl], kt[:, sl]], axis=0).astype(BF16),
                kb=jnp.concatenate([kt[:, sl], bt[:, sl]], axis=0).astype(BF16),
                hat=jnp.concatenate([bh[:, sl], kh[:, sl]], axis=0).astype(BF16)))
    zero = jnp.zeros((2 * c, LANE), BF16)
    for u in units:
        u["st"] = st_ref[u["d"], u["p"]]
    g_a = [_mm_nt(jnp.where(lo2, u["l2"], zero), u["bk"]) for u in units]
    g_b = [_mm_nt(jnp.where(lo2, zero, u["l2"]), u["kb"]) for u in units]
    ga = [jnp.where(u["strict"], jnp.concatenate([x[:c], y[:c]], axis=0), 0.0)
          for u, x, y in zip(units, g_a, g_b)]
    gr_a = [jnp.where(u["incl"], x[c:], 0.0).astype(BF16) for u, x in zip(units, g_a)]
    gr_b = [jnp.where(u["incl"], y[c:], 0.0).astype(BF16) for u, y in zip(units, g_b)]
    n_p = [jnp.where(bd2, x, 0.0) for x in ga]
    ak = [jnp.where(bd2, 0.0, x).astype(BF16) for x in ga]
    nd = [jnp.where(blk16, x, 0.0) for x in n_p]
    no = [(x - y).astype(BF16) for x, y in zip(n_p, nd)]
    ndb = [x.astype(BF16) for x in nd]
    n2 = [_mm(x, x).astype(BF16) for x in ndb]
    n4 = [_mm(x, x).astype(BF16) for x in n2]
    n8 = [_mm(x, x).astype(BF16) for x in n4]
    dinv = [eye + x for x in nd]
    dinv = [x + _mm(x, y) for x, y in zip(dinv, n2)]
    dinv = [x + _mm(x, y) for x, y in zip(dinv, n4)]
    dinv = [(x + _mm(x, y)).astype(BF16) for x, y in zip(dinv, n8)]
    m1 = [_mm(x, y) for x, y in zip(dinv, no)]
    m2 = [_mm(x, x) for x in m1]
    t_p = [eye + x for x in m1]
    t_p = [x + _mm(x, y) for x, y in zip(t_p, m2)]
    t_p = [_mm(x, y).astype(BF16) for x, y in zip(t_p, dinv)]
    akv = [_mm(x, jnp.concatenate([u["v"], u["v"]], axis=0)) for u, x in zip(units, ak)]
    base = [_mm(u["l2"], u["st"]) for u in units]
    us = [_mm(t, jnp.concatenate([b[:c], b[:c]], axis=0) + x) for t, b, x in zip(t_p, base, akv)]
    uu = [jnp.where(lo1, x[:c], x[c:]) for x in us]
    uv = [jnp.concatenate([x, u["v"]], axis=0).astype(BF16) for u, x in zip(units, uu)]
    vu = [jnp.concatenate([u["v"], x], axis=0).astype(BF16) for u, x in zip(units, uu)]
    ys = [b[c:] + jnp.where(lo1, _mm(x, p), _mm(y, q))
          for b, x, y, p, q in zip(base, gr_a, gr_b, uv, vu)]
    upd = [_mm_tn(u["hat"], x) for u, x in zip(units, uv)]
    for u, x in zip(units, upd):
        st_ref[u["d"], u["p"]] = u["st"] * u["wc"] + jnp.where(bd2, x, 0.0)
    return [jnp.concatenate(ys[d * npair:(d + 1) * npair], axis=1) for d in range(2)]


def _wkv_kernel(rf, kkf, vf, lw0, a0, kx0, rb, kkb, vb, lw1, a1, kx1, s0_ref,
                yf_o, yb_o, sfin_o, st_ref):
    j = pl.program_id(1)

    @pl.when(j == 0)
    def _():
        st_ref[...] = s0_ref[0]

    yf, yb = _wkv_step([(rf[0], kkf[0], vf[0], lw0[0], a0[0], kx0[0], False),
                        (rb[0], kkb[0], vb[0], lw1[0], a1[0], kx1[0], True)], st_ref)
    yf_o[0] = yf
    yb_o[0] = yb

    @pl.when(j == pl.num_programs(1) - 1)
    def _():
        sfin_o[0] = st_ref[...]


def _wkv(r, kk, v, lw0, a0, kx0, lw1, a1, kx1, s0):
    b, l, w = r.shape
    c = WKV_CHUNK
    nc = l // c
    fwd = pl.BlockSpec((1, c, w), lambda bi, j: (bi, j, 0))
    bwd = pl.BlockSpec((1, c, w), lambda bi, j: (bi, nc - 1 - j, 0))
    st_spec = pl.BlockSpec((1,) + s0.shape[1:], lambda bi, j: (bi, 0, 0, 0, 0))
    return pl.pallas_call(
        _wkv_kernel,
        grid=(b, nc),
        in_specs=[fwd] * 6 + [bwd] * 6 + [st_spec],
        out_specs=[fwd, bwd, st_spec],
        out_shape=[jax.ShapeDtypeStruct((b, l, w), F32), jax.ShapeDtypeStruct((b, l, w), F32),
                   jax.ShapeDtypeStruct(s0.shape, F32)],
        scratch_shapes=[pltpu.VMEM(s0.shape[1:], F32)],
        compiler_params=_cparams(("parallel", "arbitrary")),
        name="wkv",
    )(r, kk, v, lw0, a0, kx0, r, kk, v, lw1, a1, kx1, s0)


def _ssd_step(dirs, arow, st_ref):
    c = SSD_CHUNK
    npair = HD // 2
    lane = _lane((c, LANE))
    lo = lane < PD
    grp_rows = _row((c, LANE)) < NS
    zero = jnp.zeros((c, LANE), F32)
    units = []
    for d, (xs, bc, dt, bwd) in enumerate(dirs):
        o = HD * d
        acum = _mm_f32(_tri(c, bwd).astype(F32), dt * arow)
        acum_t = acum.T
        last = acum[0:1] if bwd else acum[c - 1:c]
        ea, de, cd = jnp.exp(acum), jnp.exp(last - acum), jnp.exp(last)
        bfull, cfull = bc[:, 0:LANE], bc[:, LANE:2 * LANE]
        causal = _tri(c, bwd)
        cb = [_mm_nt(jnp.where(lo, cfull, zero), bfull), _mm_nt(jnp.where(lo, zero, cfull), bfull)]

        def colsel(m, ha, hb, o=o):
            return jnp.where(lo, jnp.broadcast_to(m[:, o + ha:o + ha + 1], (c, LANE)),
                             jnp.broadcast_to(m[:, o + hb:o + hb + 1], (c, LANE)))

        for p in range(npair):
            ha, hb = 2 * p, 2 * p + 1
            decs = []
            for h in (ha, hb):
                seg = (jnp.broadcast_to(acum[:, o + h:o + h + 1], (c, c))
                       - jnp.broadcast_to(acum_t[o + h:o + h + 1, :], (c, c)))
                decs.append(jnp.where(causal, jnp.exp(jnp.where(causal, seg, 0.0)), 0.0))
            cdp = jnp.where(lo[0:1], jnp.broadcast_to(cd[:, o + ha:o + ha + 1], (1, LANE)),
                            jnp.broadcast_to(cd[:, o + hb:o + hb + 1], (1, LANE)))
            xdt = xs[:, p * LANE:(p + 1) * LANE] * colsel(dt, ha, hb)
            units.append(dict(
                d=d, p=p, g0=(p // 2 == 0), cdp=cdp, bfull=bfull.astype(BF16),
                cfull=cfull.astype(BF16), xdt=xdt.astype(BF16),
                xde=(xdt * colsel(de, ha, hb)).astype(BF16), eap=colsel(ea, ha, hb),
                sc_a=(cb[p // 2] * decs[0]).astype(BF16), sc_b=(cb[p // 2] * decs[1]).astype(BF16)))
    for u in units:
        u["st"] = st_ref[u["d"], u["p"]]
    yi_a = [_mm(u["sc_a"], u["xdt"]) for u in units]
    yi_b = [_mm(u["sc_b"], u["xdt"]) for u in units]
    yo = [_mm(u["cfull"], u["st"]) for u in units]
    cs = [_mm_tn(u["bfull"], u["xde"]) for u in units]
    ys = [jnp.where(lo, a, b) + o_ * u["eap"] for u, a, b, o_ in zip(units, yi_a, yi_b, yo)]
    for u, x in zip(units, cs):
        keep = grp_rows if u["g0"] else jnp.logical_not(grp_rows)
        st_ref[u["d"], u["p"]] = u["st"] * u["cdp"] + jnp.where(keep, x, 0.0)
    return [jnp.concatenate(ys[d * npair:(d + 1) * npair], axis=1) for d in range(2)]


def _ssd_kernel(xf, bcf, dtf, xb, bcb, dtb, arow_ref, s0_ref, yf_o, yb_o, sfin_o, st_ref):
    j = pl.program_id(1)

    @pl.when(j == 0)
    def _():
        st_ref[...] = s0_ref[0]

    yf, yb = _ssd_step([(xf[0], bcf[0], dtf[0], False), (xb[0], bcb[0], dtb[0], True)],
                       arow_ref[...], st_ref)
    yf_o[0] = yf
    yb_o[0] = yb

    @pl.when(j == pl.num_programs(1) - 1)
    def _():
        sfin_o[0] = st_ref[...]


def _ssd(xs, bc, dt, arow, s0):
    b, l, _ = xs.shape
    c = SSD_CHUNK
    nc = l // c

    def spec(w, rev):
        if rev:
            return pl.BlockSpec((1, c, w), lambda bi, j: (bi, nc - 1 - j, 0))
        return pl.BlockSpec((1, c, w), lambda bi, j: (bi, j, 0))

    st_spec = pl.BlockSpec((1,) + s0.shape[1:], lambda bi, j: (bi, 0, 0, 0, 0))
    return pl.pallas_call(
        _ssd_kernel,
        grid=(b, nc),
        in_specs=[spec(512, False), spec(256, False), spec(128, False),
                  spec(512, True), spec(256, True), spec(128, True),
                  _const_spec(arow.shape), st_spec],
        out_specs=[spec(512, False), spec(512, True), st_spec],
        out_shape=[jax.ShapeDtypeStruct((b, l, 512), F32), jax.ShapeDtypeStruct((b, l, 512), F32),
                   jax.ShapeDtypeStruct(s0.shape, F32)],
        scratch_shapes=[pltpu.VMEM(s0.shape[1:], F32)],
        compiler_params=_cparams(("parallel", "arbitrary")),
        name="ssd",
    )(xs, bc, dt, xs, bc, dt, arow, s0)


def _l1_post_kernel(h_ref, mod_ref, yf, yb, r, k, v, g, ydf, ydb, xs, zg, lnw, lnb, rk, dsum, gn,
                    w_ref, bd_ref, o_ref):
    bd = bd_ref[...]
    y = yf[0] + yb[0]
    mu = _mm_split(y, bd) * (1.0 / NC)
    dl = y - mu
    var = _mm_split(dl * dl, bd) * (1.0 / NC)
    y = dl * lax.rsqrt(var + WKV_LN_EPS) * lnw[...] + lnb[...]
    y = y + _mm_split(r[0] * k[0] * rk[...], bd) * v[0]
    oc = y * g[0]
    x = xs[0]
    yd = (ydf[0] + dsum[0:1] * x) + (ydb[0] + dsum[1:2] * x)
    od = _rms(yd * _silu(zg[0]), gn[...], D_INNER)
    m = _mm(oc, w_ref[0:512, :]) + _mm(od, w_ref[512:1024, :])
    o_ref[0] = h_ref[0] + mod_ref[0][2:3] * m


def _l1_post(h, mod, per_batch, acts, consts, tm):
    b, l, _ = h.shape
    mod_map = (lambda bi, i: (bi, 0, 0)) if per_batch else (lambda bi, i: (0, 0, 0))
    x_spec = pl.BlockSpec((1, tm, D_MODEL), lambda bi, i: (bi, i, 0))
    a_spec = pl.BlockSpec((1, tm, 512), lambda bi, i: (bi, i, 0))
    return pl.pallas_call(
        _l1_post_kernel,
        grid=(b, l // tm),
        in_specs=[x_spec, pl.BlockSpec((1, 6, D_MODEL), mod_map)] + [a_spec] * len(acts)
        + [_const_spec(c.shape) for c in consts],
        out_specs=x_spec,
        out_shape=jax.ShapeDtypeStruct(h.shape, F32),
        compiler_params=_cparams(("parallel", "parallel")),
        name="l1_post",
    )(h, mod, *acts, *consts)


def _rope_tables(length, dim, offset, width):
    quarter = dim // 4
    inv = np.float32(ROPE_BASE) ** (-np.arange(quarter, dtype=np.float32) / np.float32(quarter))
    pos = np.arange(length)
    row = (pos // GRID_W).astype(np.float32)
    col = (pos % GRID_W).astype(np.float32)
    ar, ac = row[:, None] * inv, col[:, None] * inv
    c = np.concatenate([np.cos(ar), np.cos(ar), np.cos(ac), np.cos(ac)], axis=-1)
    s = np.concatenate([-np.sin(ar), np.sin(ar), -np.sin(ac), np.sin(ac)], axis=-1)
    cw = np.ones((length, width), np.float32)
    sw = np.zeros((length, width), np.float32)
    cw[:, offset:offset + dim] = c
    sw[:, offset:offset + dim] = s
    return cw, sw


def _pad_cols(w, n):
    return jnp.pad(w, ((0, 0), (0, n - w.shape[1])))


def _prep_l0(w_in, q_norm, w_uq, kv_norm, w_ukv, w_out):
    a_cols = 3 * HA * 2 * DA
    wqd = _pad_cols(w_in[:, a_cols:a_cols + Q_LORA], 256)
    wckv = w_in[:, a_cols + Q_LORA:a_cols + Q_LORA + KV_LORA]
    wkpe = jnp.pad(w_in[:, a_cols + Q_LORA + KV_LORA:], ((0, 0), (NOPE, LANE - NOPE - ROPE_B)))
    w = jnp.concatenate([w_in[:, :a_cols], wqd, wckv, wkpe], axis=1).astype(BF16)
    qn = _pad_cols(q_norm.reshape(1, -1), 256)
    wuq = jnp.pad(w_uq.reshape(Q_LORA, HB, NOPE + ROPE_B),
                  ((0, 256 - Q_LORA), (0, 0), (0, LANE - NOPE - ROPE_B))).reshape(256, HB * LANE)
    wukv = w_ukv.reshape(KV_LORA, HB, NOPE + VB)
    wk = jnp.pad(wukv[:, :, :NOPE], ((0, 0), (0, 0), (0, LANE - NOPE))).reshape(KV_LORA, HB * LANE)
    wv = wukv[:, :, NOPE:].reshape(KV_LORA, HB * VB)
    return dict(w=w, qn=qn, wuq=wuq.astype(BF16), kvn=kv_norm.reshape(1, -1),
                wukv=jnp.concatenate([wk, wv], axis=1).astype(BF16), w_out=w_out.astype(BF16))


def _prep_l1(w_in, mu, w0, w2, a0, a2, g2, k_k, k_a, r_k, ln_w, ln_b, conv_w, conv_b, A_log,
             dt_bias, D, gnorm, w_out, norm1):
    c_cols = 3 * HC * NC + 4 * 64 + 128
    conv_ch = D_INNER + 2 * G_SSM * NS
    wdt = w_in[:, c_cols + D_INNER + conv_ch:]
    wdt = _pad_cols(jnp.concatenate([wdt, wdt], axis=1), LANE)
    w = jnp.concatenate([w_in[:, :c_cols + D_INNER + conv_ch], wdt], axis=1).astype(BF16)
    z = jnp.zeros((64, HC * NC), F32)
    w2p = jnp.stack([jnp.concatenate([w2[0], z], 0), jnp.concatenate([z, w2[1]], 0)]).astype(BF16)
    a2p = jnp.stack([jnp.concatenate([a2[0], z], 0), jnp.concatenate([z, a2[1]], 0)]).astype(BF16)
    hid = jnp.arange(HC * NC) // NC
    bd = (hid[:, None] == hid[None, :]).astype(BF16)
    dtb = _pad_cols(dt_bias.reshape(1, 2 * HD), LANE)
    arow = _pad_cols((-jnp.exp(A_log.astype(F32))).reshape(1, 2 * HD), LANE)
    return dict(norm1=norm1.reshape(1, -1), w_in=w, mu=mu.reshape(1, -1), k_k=k_k.reshape(1, -1),
                k_a=k_a.reshape(1, -1), w0=w0, w2=w2p, a0=a0, a2=a2p, g2=g2.astype(BF16),
                conv_w=conv_w, conv_b=conv_b.reshape(1, -1), dt_bias=dtb, bd=bd, arow=arow,
                ln_w=ln_w.reshape(1, -1), ln_b=ln_b.reshape(1, -1), r_k=r_k.reshape(1, -1),
                dsum=jnp.repeat(D, PD, axis=1), gnorm=gnorm.reshape(1, -1),
                w_out=w_out.astype(BF16))


def _wkv_state_in(s):
    b = s.shape[0]
    st = jnp.swapaxes(s, -1, -2).reshape(b, HC // 2, 2, NC, NC)
    z = jnp.zeros_like(st[:, :, 0])
    top = jnp.concatenate([st[:, :, 0], z], axis=-1)
    bot = jnp.concatenate([z, st[:, :, 1]], axis=-1)
    return jnp.concatenate([top, bot], axis=-2)


def _wkv_state_out(s):
    a = s[:, :, :NC, :NC]
    b_ = s[:, :, NC:, NC:]
    st = jnp.stack([a, b_], axis=2).reshape(s.shape[0], HC, NC, NC)
    return jnp.swapaxes(st, -1, -2)


def _ssd_state_in(s):
    b = s.shape[0]
    st = jnp.swapaxes(s, -1, -2).reshape(b, HD // 2, 2, NS, PD)
    pair = jnp.concatenate([st[:, :, 0], st[:, :, 1]], axis=-1)
    z = jnp.zeros_like(pair)
    g0 = jnp.concatenate([pair, z], axis=-2)
    g1 = jnp.concatenate([z, pair], axis=-2)
    is_g0 = (jnp.arange(HD // 2) < HD // 4)[None, :, None, None]
    return jnp.where(is_g0, g0, g1)


def _ssd_state_out(s):
    is_g0 = (jnp.arange(HD // 2) < HD // 4)[None, :, None, None]
    pair = jnp.where(is_g0, s[:, :, :NS, :], s[:, :, NS:, :])
    st = jnp.stack([pair[..., :PD], pair[..., PD:]], axis=2).reshape(s.shape[0], HD, NS, PD)
    return jnp.swapaxes(st, -1, -2)


def _layer0(h, mod, per_batch, p, norm1, ctx, tables, lam_vecs, subln, lambda_init, tm, tq):
    qa, ka, va, qb, ckv, kpe = _l0_proj(h, mod, per_batch, norm1, p["w"], p["qn"], p["wuq"],
                                        p["kvn"], tables, tm)
    own = (ka, va, ckv, kpe)
    if ctx is not None:
        ck, cv, cckv, ckpe = ctx
        ka = jnp.concatenate([ck.astype(BF16), ka], axis=2)
        va = jnp.concatenate([cv.astype(BF16), va], axis=2)
        ckv = jnp.concatenate([cckv, ckv], axis=1)
        kpe = jnp.concatenate([jnp.pad(ckpe, ((0, 0), (0, 0), (NOPE, LANE - NOPE - ROPE_B))), kpe],
                              axis=1)
    kb, vb = _kvup(ckv, kpe, p["wukv"], min(512, ckv.shape[1]))
    nsub = tq // 128
    oa = _attention(qa, ka, va, tq, nsub, True, lam_vecs, subln, lambda_init)
    ob = _attention(qb, kb, vb, tq, nsub, False)
    return _oproj(h, mod, per_batch, oa, ob, p["w_out"], tm), own


def _layer1(h, mod, per_batch, p, wkv0, ssm0, tm):
    (r, k, v, kk, g, lw0, a0, kx0, lw1, a1, kx1, zg, xs, bc, dt) = _l1_proj(h, mod, per_batch, p, tm)
    yf, yb, s_wkv = _wkv(r, kk, v, lw0, a0, kx0, lw1, a1, kx1, wkv0)
    ydf, ydb, s_ssm = _ssd(xs, bc, dt, p["arow"], ssm0)
    consts = [p["ln_w"], p["ln_b"], p["r_k"], p["dsum"], p["gnorm"], p["w_out"], p["bd"]]
    h = _l1_post(h, mod, per_batch, [yf, yb, r, k, v, g, ydf, ydb, xs, zg], consts, tm)
    return h, s_wkv, s_ssm


def kernel(x_prompt, x_sample, cache_l0_k, cache_l0_v, cache_l0_ckv, cache_l0_kpe, state_l1_wkv_fwd, state_l1_wkv_bwd, state_l1_ssm_fwd, state_l1_ssm_bwd, c, c_ctx, ada_w_0, ada_b_0, norm1_0, norm2_0, ffn_up_0, ffn_conv_w_0, ffn_conv_b_0, ffn_down_0, l0_w_in, l0_lambda_q1, l0_lambda_k1, l0_lambda_q2, l0_lambda_k2, l0_subln, l0_q_norm, l0_w_uq, l0_kv_norm, l0_w_ukv, l0_w_out, ada_w_1, ada_b_1, norm1_1, norm2_1, ffn_up_1, ffn_conv_w_1, ffn_conv_b_1, ffn_down_1, l1_w_in, l1_mu, l1_w0, l1_w2, l1_a0, l1_a2, l1_g2, l1_k_k, l1_k_a, l1_r_k, l1_ln_w, l1_ln_b, l1_conv_w, l1_conv_b, l1_A_log, l1_dt_bias, l1_D, l1_gnorm, l1_w_out, norm_f):
    bc_, lc, _ = x_prompt.shape
    bl, ll, _ = x_sample.shape
    tm_c, tm_l = min(256, lc), min(256, ll)
    cond = jnp.concatenate([c_ctx[None, :], c], axis=0)
    cond8 = jnp.pad(cond, ((0, 8 - cond.shape[0]), (0, 0)))
    mods = [_ada(cond8, w, b).reshape(8, 6, D_MODEL) for w, b in ((ada_w_0, ada_b_0), (ada_w_1, ada_b_1))]
    p0 = _prep_l0(l0_w_in, l0_q_norm, l0_w_uq, l0_kv_norm, l0_w_ukv, l0_w_out)
    tables = _rope_tables(ll, DA, 0, DA)
    tables = tuple(np.concatenate([t, t], axis=1) for t in tables) + _rope_tables(ll, ROPE_B, NOPE, LANE)
    tables = tuple(jnp.asarray(t) for t in tables)
    lam_vecs = jnp.pad(jnp.stack([l0_lambda_q1, l0_lambda_k1, l0_lambda_q2, l0_lambda_k2]),
                       ((0, 4), (0, LANE - DA)))
    subln = l0_subln.reshape(1, -1)
    lambda_init = 0.8 - 0.6 * math.exp(-0.3 * 0)
    n1 = norm1_0.reshape(1, -1)
    h_ctx, own = _layer0(x_prompt, mods[0][0:1], False, p0, n1, None, None, lam_vecs, subln,
                         lambda_init, tm_c, min(256, lc))
    h_lat, _ = _layer0(x_sample, mods[0][1:1 + bl], True, p0, n1,
                       (cache_l0_k, cache_l0_v, cache_l0_ckv, cache_l0_kpe), tables, lam_vecs,
                       subln, lambda_init, tm_l, min(256, ll))
    ffn0 = (norm2_0.reshape(1, -1), ffn_up_0.astype(BF16), ffn_conv_w_0, ffn_conv_b_0.reshape(1, -1),
            ffn_down_0.astype(BF16))
    h_ctx = _ffn(h_ctx, mods[0][0:1], False, *ffn0, None, tm_c)
    h_lat = _ffn(h_lat, mods[0][1:1 + bl], True, *ffn0, None, min(512, ll))
    p1 = _prep_l1(l1_w_in, l1_mu, l1_w0, l1_w2, l1_a0, l1_a2, l1_g2, l1_k_k, l1_k_a, l1_r_k,
                  l1_ln_w, l1_ln_b, l1_conv_w, l1_conv_b, l1_A_log, l1_dt_bias, l1_D, l1_gnorm,
                  l1_w_out, norm1_1)
    zero_state = jnp.zeros((bc_, 2, HC // 2, LANE, LANE), F32)
    h_ctx, s_wkv, s_ssm = _layer1(h_ctx, mods[1][0:1], False, p1, zero_state, zero_state, tm_c)
    wkv0 = jnp.stack([_wkv_state_in(state_l1_wkv_fwd), _wkv_state_in(state_l1_wkv_bwd)], axis=1)
    ssm0 = jnp.stack([_ssd_state_in(state_l1_ssm_fwd), _ssd_state_in(state_l1_ssm_bwd)], axis=1)
    h_lat, _, _ = _layer1(h_lat, mods[1][1:1 + bl], True, p1, wkv0, ssm0, tm_l)
    ffn1 = (norm2_1.reshape(1, -1), ffn_up_1.astype(BF16), ffn_conv_w_1, ffn_conv_b_1.reshape(1, -1),
            ffn_down_1.astype(BF16))
    nf = norm_f.reshape(1, -1)
    y_prompt = _ffn(h_ctx, mods[1][0:1], False, *ffn1, nf, tm_c)
    y_sample = _ffn(h_lat, mods[1][1:1 + bl], True, *ffn1, nf, min(512, ll))
    ka, va, ckv, kpe = own
    return (y_prompt, y_sample, ka, va, ckv, kpe[:, :, NOPE:NOPE + ROPE_B],
            _wkv_state_out(s_wkv[:, 0]), _wkv_state_out(s_wkv[:, 1]),
            _ssd_state_out(s_ssm[:, 0]), _ssd_state_out(s_ssm[:, 1]))
```

```python
import functools
import math

import jax
import jax.numpy as jnp
import numpy as np
from jax import lax
from jax.experimental import pallas as pl
from jax.experimental.pallas import tpu as pltpu

F32 = jnp.float32
BF16 = jnp.bfloat16
HIGHEST = lax.Precision.HIGHEST

D_MODEL = 1024
GRID_W = 64
ROPE_BASE = 10000.0
NORM_EPS = 1e-6
HA, DA = 4, 64
HB, NOPE, ROPE_B, VB = 4, 64, 32, 128
Q_LORA, KV_LORA = 192, 128
HC, NC = 8, 64
WKV_LN_EPS = 64e-5
WKV_DECAY_SCALE = 0.606531
HD, PD, G_SSM, NS = 8, 64, 2, 64
D_INNER = HD * PD
F_FF = 2816
LANE = 128
HALO = 8
LOG2E = math.log2(math.e)
WKV_CHUNK = 64
SSD_CHUNK = 128
VMEM_LIMIT = 56 * 1024 * 1024


def _cparams(sem):
    return pltpu.CompilerParams(dimension_semantics=sem, vmem_limit_bytes=VMEM_LIMIT)


def _mm(a, b):
    return jnp.dot(a.astype(BF16), b.astype(BF16), preferred_element_type=F32)


def _mm_nt(a, b):
    return lax.dot_general(a.astype(BF16), b.astype(BF16), (((1,), (1,)), ((), ())),
                           preferred_element_type=F32)


def _mm_tn(a, b):
    return lax.dot_general(a.astype(BF16), b.astype(BF16), (((0,), (0,)), ((), ())),
                           preferred_element_type=F32)


def _mm_f32(a, b):
    return jnp.dot(a, b, precision=HIGHEST, preferred_element_type=F32)


def _mm_split(x, m):
    hi = x.astype(BF16)
    lo = (x - hi.astype(F32)).astype(BF16)
    return (jnp.dot(hi, m, preferred_element_type=F32) + jnp.dot(lo, m, preferred_element_type=F32))


def _rms(x, g, n):
    ms = jnp.sum(x * x, axis=-1, keepdims=True) * (1.0 / n)
    return x * lax.rsqrt(ms + NORM_EPS) * g


def _silu(x):
    return x * jax.nn.sigmoid(x)


def _lane(shape):
    return lax.broadcasted_iota(jnp.int32, shape, len(shape) - 1)


def _row(shape):
    return lax.broadcasted_iota(jnp.int32, shape, len(shape) - 2)


def _const_spec(shape, single=False):
    nd = len(shape)
    if single:
        return pl.BlockSpec(shape, lambda *_: (0,) * nd, pipeline_mode=pl.Buffered(1))
    return pl.BlockSpec(shape, lambda *_: (0,) * nd)


def _ada_kernel(c_ref, w_ref, b_ref, o_ref):
    o_ref[...] = _mm(_silu(c_ref[...]), w_ref[...]) + b_ref[...]


def _ada(cond8, w, b):
    n = w.shape[1]
    tn = 1536
    return pl.pallas_call(
        _ada_kernel,
        grid=(n // tn,),
        in_specs=[_const_spec((8, D_MODEL)),
                  pl.BlockSpec((D_MODEL, tn), lambda j: (0, j)),
                  pl.BlockSpec((1, tn), lambda j: (0, j))],
        out_specs=pl.BlockSpec((8, tn), lambda j: (0, j)),
        out_shape=jax.ShapeDtypeStruct((8, n), F32),
        compiler_params=_cparams(("arbitrary",)),
        name="ada",
    )(cond8, w, b.reshape(1, n))


def _rope(x, c, s, half):
    w = x.shape[-1]
    up = pltpu.roll(x, w - half, 1)
    dn = pltpu.roll(x, half, 1)
    first = (_lane(x.shape) & (2 * half - 1)) < half
    return x * c + jnp.where(first, up, dn) * s


def _l0_proj_kernel(*refs, latent):
    if latent:
        (x_ref, mod_ref, n1_ref, w_ref, qn_ref, wuq_ref, kvn_ref, ca_ref, sa_ref, cb_ref, sb_ref,
         qa_o, ka_o, va_o, qb_o, ckv_o, kpe_o) = refs
    else:
        (x_ref, mod_ref, n1_ref, w_ref, qn_ref, wuq_ref, kvn_ref,
         qa_o, ka_o, va_o, qb_o, ckv_o, kpe_o) = refs
    mod = mod_ref[0]
    hn = _rms(x_ref[0], n1_ref[...], D_MODEL) * (1.0 + mod[1:2]) + mod[0:1]
    proj = _mm(hn, w_ref[...])
    qd = _rms(proj[:, 1536:1792], qn_ref[...], Q_LORA)
    qb = _mm(qd, wuq_ref[...])
    ckv_o[0] = _rms(proj[:, 1792:1920], kvn_ref[...], KV_LORA)
    kpe = proj[:, 1920:2048]
    if latent:
        ca, sa, cb, sb = ca_ref[...], sa_ref[...], cb_ref[...], sb_ref[...]
        kpe = _rope(kpe, cb, sb, ROPE_B // 4)
    kpe_o[0] = kpe
    for h in range(HA):
        sl = slice(h * LANE, (h + 1) * LANE)
        q = proj[:, sl]
        k = proj[:, 512 + h * LANE:512 + (h + 1) * LANE]
        qbh = qb[:, sl]
        if latent:
            q = _rope(q, ca, sa, DA // 4)
            k = _rope(k, ca, sa, DA // 4)
            qbh = _rope(qbh, cb, sb, ROPE_B // 4)
        qa_o[0, h] = (q * (DA ** -0.5 * LOG2E)).astype(qa_o.dtype)
        ka_o[0, h] = k.astype(ka_o.dtype)
        va_o[0, h] = proj[:, 1024 + h * LANE:1024 + (h + 1) * LANE].astype(va_o.dtype)
        qb_o[0, h] = (qbh * ((NOPE + ROPE_B) ** -0.5 * LOG2E)).astype(qb_o.dtype)


def _l0_proj(x, mod, per_batch, n1, w, qn, wuq, kvn, tables, tm):
    b, l, _ = x.shape
    latent = tables is not None
    kv_dtype = BF16 if latent else F32
    mod_map = (lambda bi, i: (bi, 0, 0)) if per_batch else (lambda bi, i: (0, 0, 0))
    in_specs = [pl.BlockSpec((1, tm, D_MODEL), lambda bi, i: (bi, i, 0)),
                pl.BlockSpec((1, 6, D_MODEL), mod_map),
                _const_spec((1, D_MODEL)), _const_spec(w.shape), _const_spec(qn.shape),
                _const_spec(wuq.shape), _const_spec(kvn.shape)]
    args = [x, mod, n1, w, qn, wuq, kvn]
    if latent:
        in_specs += [pl.BlockSpec((tm, LANE), lambda bi, i: (i, 0))] * 4
        args += list(tables)
    head_spec = pl.BlockSpec((1, HA, tm, LANE), lambda bi, i: (bi, 0, i, 0))
    row_spec = pl.BlockSpec((1, tm, LANE), lambda bi, i: (bi, i, 0))
    return pl.pallas_call(
        functools.partial(_l0_proj_kernel, latent=latent),
        grid=(b, l // tm),
        in_specs=in_specs,
        out_specs=[head_spec, head_spec, head_spec, head_spec, row_spec, row_spec],
        out_shape=[jax.ShapeDtypeStruct((b, HA, l, LANE), BF16),
                   jax.ShapeDtypeStruct((b, HA, l, LANE), kv_dtype),
                   jax.ShapeDtypeStruct((b, HA, l, LANE), kv_dtype),
                   jax.ShapeDtypeStruct((b, HB, l, LANE), BF16),
                   jax.ShapeDtypeStruct((b, l, LANE), F32),
                   jax.ShapeDtypeStruct((b, l, LANE), F32)],
        compiler_params=_cparams(("parallel", "parallel")),
        name="l0_proj",
    )(*args)


def _kvup_kernel(ckv_ref, kpe_ref, w_ref, kb_o, vb_o):
    kv = _mm(ckv_ref[0], w_ref[...])
    kpe = kpe_ref[0]
    for h in range(HB):
        kb_o[0, h] = (kv[:, h * LANE:(h + 1) * LANE] + kpe).astype(BF16)
        vb_o[0, h] = kv[:, 512 + h * LANE:512 + (h + 1) * LANE].astype(BF16)


def _kvup(ckv, kpe, w, tk):
    b, lk, _ = ckv.shape
    row_spec = pl.BlockSpec((1, tk, LANE), lambda bi, i: (bi, i, 0))
    head_spec = pl.BlockSpec((1, HB, tk, LANE), lambda bi, i: (bi, 0, i, 0))
    return pl.pallas_call(
        _kvup_kernel,
        grid=(b, lk // tk),
        in_specs=[row_spec, row_spec, _const_spec(w.shape)],
        out_specs=[head_spec, head_spec],
        out_shape=[jax.ShapeDtypeStruct((b, HB, lk, LANE), BF16)] * 2,
        compiler_params=_cparams(("parallel", "parallel")),
        name="kvup",
    )(ckv, kpe, w)


def _attn_kernel(*refs, diff, lambda_init, nsub):
    if diff:
        q_ref, k_ref, v_ref, lam_ref, g_ref, o_ref = refs
    else:
        q_ref, k_ref, v_ref, o_ref = refs
    k = k_ref[0, 0]
    v = v_ref[0, 0]
    ts = q_ref.shape[2] // nsub
    qs = []
    for i in range(nsub):
        q = q_ref[0, 0, i * ts:(i + 1) * ts, :]
        if diff:
            lo = _lane(q.shape) < DA
            zero = jnp.zeros_like(q)
            q = jnp.concatenate([jnp.where(lo, q, zero), jnp.where(lo, zero, q)], axis=0)
        qs.append(q)
    s = [_mm_nt(q, k) for q in qs]
    m = [jnp.max(x, axis=-1, keepdims=True) for x in s]
    p = [jnp.exp2(x - y) for x, y in zip(s, m)]
    inv = [1.0 / jnp.sum(x, axis=-1, keepdims=True) for x in p]
    o = [_mm(x, v) * y for x, y in zip(p, inv)]
    if diff:
        lv = lam_ref[...]
        lam = (jnp.exp(jnp.sum(lv[0:1] * lv[1:2], axis=-1, keepdims=True))
               - jnp.exp(jnp.sum(lv[2:3] * lv[3:4], axis=-1, keepdims=True)) + lambda_init)
        o = [_rms(x[:ts] - lam * x[ts:], g_ref[...], 2 * DA) * (1.0 - lambda_init) for x in o]
    for i in range(nsub):
        o_ref[0, i * ts:(i + 1) * ts, :] = o[i].astype(o_ref.dtype)


def _attention(q, k, v, tq, nsub, diff, lam_vecs=None, subln=None, lambda_init=0.0):
    b, h, l, _ = q.shape
    lk = k.shape[2]
    in_specs = [pl.BlockSpec((1, 1, tq, LANE), lambda bi, hi, i: (bi, hi, i, 0)),
                pl.BlockSpec((1, 1, lk, LANE), lambda bi, hi, i: (bi, hi, 0, 0)),
                pl.BlockSpec((1, 1, lk, LANE), lambda bi, hi, i: (bi, hi, 0, 0))]
    args = [q, k, v]
    if diff:
        in_specs += [_const_spec(lam_vecs.shape), _const_spec(subln.shape)]
        args += [lam_vecs, subln]
    return pl.pallas_call(
        functools.partial(_attn_kernel, diff=diff, lambda_init=lambda_init, nsub=nsub),
        grid=(b, h, l // tq),
        in_specs=in_specs,
        out_specs=pl.BlockSpec((1, tq, LANE), lambda bi, hi, i: (bi, i, hi)),
        out_shape=jax.ShapeDtypeStruct((b, l, h * LANE), BF16),
        compiler_params=_cparams(("parallel", "parallel", "parallel")),
        name="attn_diff" if diff else "attn_mla",
    )(*args)


def _oproj_kernel(h_ref, mod_ref, a_ref, b_ref, w_ref, o_ref):
    half = a_ref.shape[-1]
    m = _mm(a_ref[0], w_ref[0:half, :]) + _mm(b_ref[0], w_ref[half:, :])
    o_ref[0] = h_ref[0] + mod_ref[0][2:3] * m


def _oproj(h, mod, per_batch, oa, ob, w, tm):
    b, l, _ = h.shape
    mod_map = (lambda bi, i: (bi, 0, 0)) if per_batch else (lambda bi, i: (0, 0, 0))
    x_spec = pl.BlockSpec((1, tm, D_MODEL), lambda bi, i: (bi, i, 0))
    half_spec = pl.BlockSpec((1, tm, oa.shape[-1]), lambda bi, i: (bi, i, 0))
    return pl.pallas_call(
        _oproj_kernel,
        grid=(b, l // tm),
        in_specs=[x_spec, pl.BlockSpec((1, 6, D_MODEL), mod_map), half_spec, half_spec,
                  _const_spec(w.shape)],
        out_specs=x_spec,
        out_shape=jax.ShapeDtypeStruct(h.shape, F32),
        compiler_params=_cparams(("parallel", "parallel")),
        name="oproj",
    )(h, mod, oa, ob, w)


def _halo_rows(x_ref, xp_ref, xn_ref):
    i = pl.program_id(1)
    last = pl.num_programs(1) - 1
    xc = jnp.concatenate([xp_ref[0], x_ref[0], xn_ref[0]], axis=0)
    tm = x_ref.shape[1]
    r = _row((tm + 2 * HALO, 1))
    valid = jnp.logical_and(jnp.logical_or(r >= HALO, i > 0),
                            jnp.logical_or(r < tm + HALO, i < last))
    return xc, valid


def _shift_rows(u, tm):
    n = u.shape[0]
    up = pltpu.roll(u, 1, 0)[HALO:HALO + tm]
    dn = pltpu.roll(u, n - 1, 0)[HALO:HALO + tm]
    return up, u[HALO:HALO + tm], dn


def _ffn_kernel(x_ref, xp_ref, xn_ref, mod_ref, n2_ref, wu_ref, cw_ref, cb_ref, wd_ref,
                *rest, final, cw):
    if final:
        nf_ref, o_ref = rest
    else:
        (o_ref,) = rest
    tm = x_ref.shape[1]
    mod = mod_ref[0]
    xc, valid = _halo_rows(x_ref, xp_ref, xn_ref)
    hn = _rms(xc, n2_ref[...], D_MODEL) * (1.0 + mod[4:5]) + mod[3:4]
    hn = jnp.where(valid, hn, 0.0).astype(BF16)
    acc = jnp.zeros((tm, D_MODEL), F32)
    for c in range(F_FF // cw):
        halves = []
        for off in (c * cw, F_FF + c * cw):
            u = jnp.dot(hn, wu_ref[:, off:off + cw], preferred_element_type=F32)
            up, mid, dn = _shift_rows(u, tm)
            w3 = cw_ref[:, off:off + cw]
            halves.append(up * w3[0:1] + mid * w3[1:2] + dn * w3[2:3] + cb_ref[:, off:off + cw])
        act = (_silu(halves[0]) * halves[1]).astype(BF16)
        acc = acc + jnp.dot(act, wd_ref[c * cw:(c + 1) * cw, :], preferred_element_type=F32)
    out = x_ref[0] + mod[5:6] * acc
    if final:
        out = _rms(out, nf_ref[...], D_MODEL)
    o_ref[0] = out


def _ffn(h, mod, per_batch, n2, wu, cw3, cb, wd, nf, tm):
    b, l, _ = h.shape
    nblk = l // tm
    nb = tm // HALO
    mod_map = (lambda bi, i: (bi, 0, 0)) if per_batch else (lambda bi, i: (0, 0, 0))
    x_spec = pl.BlockSpec((1, tm, D_MODEL), lambda bi, i: (bi, i, 0))
    in_specs = [x_spec,
                pl.BlockSpec((1, HALO, D_MODEL), lambda bi, i: (bi, jnp.maximum(i * nb - 1, 0), 0)),
                pl.BlockSpec((1, HALO, D_MODEL),
                             lambda bi, i: (bi, jnp.minimum((i + 1) * nb, nblk * nb - 1), 0)),
                pl.BlockSpec((1, 6, D_MODEL), mod_map),
                _const_spec(n2.shape), _const_spec(wu.shape, True), _const_spec(cw3.shape),
                _const_spec(cb.shape), _const_spec(wd.shape, True)]
    args = [h, h, h, mod, n2, wu, cw3, cb, wd]
    final = nf is not None
    if final:
        in_specs.append(_const_spec(nf.shape))
        args.append(nf)
    cw = 2816
    return pl.pallas_call(
        functools.partial(_ffn_kernel, final=final, cw=cw),
        grid=(b, nblk),
        in_specs=in_specs,
        out_specs=x_spec,
        out_shape=jax.ShapeDtypeStruct(h.shape, F32),
        compiler_params=_cparams(("parallel", "parallel")),
        name="ffn",
    )(*args)


def _softplus(x):
    return jnp.maximum(x, 0.0) + jnp.log1p(jnp.exp(-jnp.abs(x)))


def _l1_proj_kernel(x_ref, xp_ref, xn_ref, mod_ref, n1_ref, w_ref, mu_ref, kk_ref, ka_ref,
                    w0_ref, w2_ref, a0_ref, a2_ref, g2_ref, cw_ref, cb_ref, dtb_ref, bd_ref,
                    r_o, k_o, v_o, kk_o, g_o, lw0_o, ai0_o, kx0_o, lw1_o, ai1_o, kx1_o,
                    zg_o, xs_o, bc_o, dt_o):
    tm = x_ref.shape[1]
    mod = mod_ref[0]
    xc, valid = _halo_rows(x_ref, xp_ref, xn_ref)
    hn = _rms(xc, n1_ref[...], D_MODEL) * (1.0 + mod[1:2]) + mod[0:1]
    hn = jnp.where(valid, hn, 0.0).astype(BF16)
    up, mid, dn = _shift_rows(jnp.dot(hn, w_ref[:, 0:1920], preferred_element_type=F32), tm)
    zc = mid + mu_ref[...] * (0.5 * (up + dn) - mid)
    r, k, v = zc[:, 0:512], zc[:, 512:1024], zc[:, 1024:1536]
    wd, ad, gd = zc[:, 1536:1664], zc[:, 1664:1792], zc[:, 1792:1920]
    r_o[0], k_o[0], v_o[0] = r, k, v
    kkf = k * kk_ref[...]
    kk_o[0] = kkf * lax.rsqrt(_mm_split(kkf * kkf, bd_ref[...]) + 1e-12)
    g_o[0] = _mm(jax.nn.sigmoid(gd), g2_ref[...])
    twd = jnp.tanh(wd)
    for i, (lw_o, ai_o, kx_o) in enumerate(((lw0_o, ai0_o, kx0_o), (lw1_o, ai1_o, kx1_o))):
        lw_o[0] = -WKV_DECAY_SCALE * jax.nn.sigmoid(w0_ref[i:i + 1] + _mm(twd, w2_ref[i]))
        a = jax.nn.sigmoid(a0_ref[i:i + 1] + _mm(ad, a2_ref[i]))
        ai_o[0] = a
        kx_o[0] = k * (1.0 + (a - 1.0) * ka_ref[...])
    zd = jnp.dot(hn, w_ref[:, 1920:3328], preferred_element_type=F32)
    zg_o[0] = zd[HALO:HALO + tm, 0:512]
    up, mid, dn = _shift_rows(zd[:, 512:1280], tm)
    cw3 = cw_ref[...]
    xbc = _silu(up * cw3[0:1] + mid * cw3[1:2] + dn * cw3[2:3] + cb_ref[...])
    xs_o[0] = xbc[:, 0:512]
    bc_o[0] = xbc[:, 512:768]
    dt_o[0] = _softplus(zd[HALO:HALO + tm, 1280:1408] + dtb_ref[...])


def _l1_proj(x, mod, per_batch, p, tm):
    b, l, _ = x.shape
    nblk = l // tm
    nb = tm // HALO
    mod_map = (lambda bi, i: (bi, 0, 0)) if per_batch else (lambda bi, i: (0, 0, 0))
    x_spec = pl.BlockSpec((1, tm, D_MODEL), lambda bi, i: (bi, i, 0))
    consts = [p["norm1"], p["w_in"], p["mu"], p["k_k"], p["k_a"], p["w0"], p["w2"], p["a0"],
              p["a2"], p["g2"], p["conv_w"], p["conv_b"], p["dt_bias"], p["bd"]]
    in_specs = [x_spec,
                pl.BlockSpec((1, HALO, D_MODEL), lambda bi, i: (bi, jnp.maximum(i * nb - 1, 0), 0)),
                pl.BlockSpec((1, HALO, D_MODEL),
                             lambda bi, i: (bi, jnp.minimum((i + 1) * nb, nblk * nb - 1), 0)),
                pl.BlockSpec((1, 6, D_MODEL), mod_map)] + [_const_spec(c.shape) for c in consts]

    def o_spec(w):
        return pl.BlockSpec((1, tm, w), lambda bi, i: (bi, i, 0))

    widths = [512] * 11 + [512, 512, 256, 128]
    return pl.pallas_call(
        _l1_proj_kernel,
        grid=(b, nblk),
        in_specs=in_specs,
        out_specs=[o_spec(w) for w in widths],
        out_shape=[jax.ShapeDtypeStruct((b, l, w), F32) for w in widths],
        compiler_params=_cparams(("parallel", "parallel")),
        name="l1_proj",
    )(x, x, x, mod, *consts)


def _tri(n, upper, strict=False):
    r, c = _row((n, n)), _lane((n, n))
    if upper:
        return (c > r) if strict else (c >= r)
    return (c < r) if strict else (c <= r)


def _wkv_step(dirs, st_ref):
    c = WKV_CHUNK
    npair = HC // 2
    lane2 = _lane((2 * c, LANE))
    row2 = _row((2 * c, LANE))
    lo2 = lane2 < NC
    top2 = row2 < c
    bd2 = lo2 == top2
    tmask = row2 - jnp.where(top2, 0, c)
    imask = lane2 - jnp.where(lo2, 0, NC)
    blk16 = jnp.right_shift(tmask, 4) == jnp.right_shift(imask, 4)
    eye = jnp.logical_and(bd2, tmask == imask).astype(F32)
    lo1 = _lane((c, LANE)) < NC
    units = []
    for d, (r, kk, v, lw, a, kx, bwd) in enumerate(dirs):
        cum = _mm_f32(_tri(c, bwd).astype(F32), lw)
        last = cum[0:1] if bwd else cum[c - 1:c]
        e_in, e_out = jnp.exp(cum - lw), jnp.exp(cum)
        e_neg, e_end = jnp.exp(-cum), jnp.exp(last - cum)
        at, rt = -kk * e_in, r * e_out
        bb = kk * a
        bt, kt = bb * e_neg, kx * e_neg
        bh, kh = bb * e_end, kx * e_end
        wc = jnp.exp(last)
        strict = (imask > tmask) if bwd else (imask < tmask)
        incl = ((imask >= tmask) if bwd else (imask <= tmask))[:c]
        for p in range(npair):
            sl = slice(p * LANE, (p + 1) * LANE)
            units.append(dict(
                d=d, p=p, strict=strict, incl=incl, v=v[:, sl], wc=wc[:, sl],
                l2=jnp.concatenate([at[:, sl], rt[:, sl]], axis=0).astype(BF16),
                bk=jnp.concatenate([bt[:, sl], kt[:, sl]], axis=0).astype(BF16),
                kb=jnp.concatenate([kt[:, sl], bt[:, sl]], axis=0).astype(BF16),
                hat=jnp.concatenate([bh[:, sl], kh[:, sl]], axis=0).astype(BF16)))
    zero = jnp.zeros((2 * c, LANE), BF16)
    for u in units:
        u["st"] = st_ref[u["d"], u["p"]]
    g_a = [_mm_nt(jnp.where(lo2, u["l2"], zero), u["bk"]) for u in units]
    g_b = [_mm_nt(jnp.where(lo2, zero, u["l2"]), u["kb"]) for u in units]
    ga = [jnp.where(u["strict"], jnp.concatenate([x[:c], y[:c]], axis=0), 0.0)
          for u, x, y in zip(units, g_a, g_b)]
    gr_a = [jnp.where(u["incl"], x[c:], 0.0).astype(BF16) for u, x in zip(units, g_a)]
    gr_b = [jnp.where(u["incl"], y[c:], 0.0).astype(BF16) for u, y in zip(units, g_b)]
    n_p = [jnp.where(bd2, x, 0.0) for x in ga]
    ak = [jnp.where(bd2, 0.0, x).astype(BF16) for x in ga]
    nd = [jnp.where(blk16, x, 0.0) for x in n_p]
    no = [(x - y).astype(BF16) for x, y in zip(n_p, nd)]
    ndb = [x.astype(BF16) for x in nd]
    n2 = [_mm(x, x).astype(BF16) for x in ndb]
    n4 = [_mm(x, x).astype(BF16) for x in n2]
    n8 = [_mm(x, x).astype(BF16) for x in n4]
    dinv = [eye + x for x in nd]
    dinv = [x + _mm(x, y) for x, y in zip(dinv, n2)]
    dinv = [x + _mm(x, y) for x, y in zip(dinv, n4)]
    dinv = [(x + _mm(x, y)).astype(BF16) for x, y in zip(dinv, n8)]
    m1 = [_mm(x, y) for x, y in zip(dinv, no)]
    m2 = [_mm(x, x) for x in m1]
    t_p = [eye + x for x in m1]
    t_p = [x + _mm(x, y) for x, y in zip(t_p, m2)]
    t_p = [_mm(x, y).astype(BF16) for x, y in zip(t_p, dinv)]
    akv = [_mm(x, jnp.concatenate([u["v"], u["v"]], axis=0)) for u, x in zip(units, ak)]
    base = [_mm_nt(u["l2"], u["st"]) for u in units]
    us = [_mm(t, jnp.concatenate([b[:c], b[:c]], axis=0) + x) for t, b, x in zip(t_p, base, akv)]
    uu = [jnp.where(lo1, x[:c], x[c:]) for x in us]
    uv = [jnp.concatenate([x, u["v"]], axis=0).astype(BF16) for u, x in zip(units, uu)]
    vu = [jnp.concatenate([u["v"], x], axis=0).astype(BF16) for u, x in zip(units, uu)]
    ys = [b[c:] + jnp.where(lo1, _mm(x, p), _mm(y, q))
          for b, x, y, p, q in zip(base, gr_a, gr_b, uv, vu)]
    upd = [_mm_tn(x, u["hat"]) for u, x in zip(units, uv)]
    for u, x in zip(units, upd):
        st_ref[u["d"], u["p"]] = u["st"] * u["wc"] + jnp.where(bd2, x, 0.0)
    return [jnp.concatenate(ys[d * npair:(d + 1) * npair], axis=1) for d in range(2)]


def _wkv_kernel(rf, kkf, vf, lw0, a0, kx0, rb, kkb, vb, lw1, a1, kx1, s0_ref,
                yf_o, yb_o, sfin_o, st_ref):
    j = pl.program_id(1)

    @pl.when(j == 0)
    def _():
        st_ref[...] = s0_ref[0]

    yf, yb = _wkv_step([(rf[0], kkf[0], vf[0], lw0[0], a0[0], kx0[0], False),
                        (rb[0], kkb[0], vb[0], lw1[0], a1[0], kx1[0], True)], st_ref)
    yf_o[0] = yf
    yb_o[0] = yb

    @pl.when(j == pl.num_programs(1) - 1)
    def _():
        sfin_o[0] = st_ref[...]


def _wkv(r, kk, v, lw0, a0, kx0, lw1, a1, kx1, s0):
    b, l, w = r.shape
    c = WKV_CHUNK
    nc = l // c
    fwd = pl.BlockSpec((1, c, w), lambda bi, j: (bi, j, 0))
    bwd = pl.BlockSpec((1, c, w), lambda bi, j: (bi, nc - 1 - j, 0))
    st_spec = pl.BlockSpec((1,) + s0.shape[1:], lambda bi, j: (bi, 0, 0, 0, 0))
    return pl.pallas_call(
        _wkv_kernel,
        grid=(b, nc),
        in_specs=[fwd] * 6 + [bwd] * 6 + [st_spec],
        out_specs=[fwd, bwd, st_spec],
        out_shape=[jax.ShapeDtypeStruct((b, l, w), F32), jax.ShapeDtypeStruct((b, l, w), F32),
                   jax.ShapeDtypeStruct(s0.shape, F32)],
        scratch_shapes=[pltpu.VMEM(s0.shape[1:], F32)],
        compiler_params=_cparams(("parallel", "arbitrary")),
        name="wkv",
    )(r, kk, v, lw0, a0, kx0, r, kk, v, lw1, a1, kx1, s0)


def _ssd_step(dirs, arow, st_ref):
    c = SSD_CHUNK
    npair = HD // 2
    lane = _lane((c, LANE))
    lo = lane < PD
    grp_rows = _row((c, LANE)) < NS
    zero = jnp.zeros((c, LANE), F32)
    units = []
    for d, (xs, bc, dt, bwd) in enumerate(dirs):
        o = HD * d
        acum = _mm_f32(_tri(c, bwd).astype(F32), dt * arow)
        acum_t = acum.T
        last = acum[0:1] if bwd else acum[c - 1:c]
        ea, de, cd = jnp.exp(acum), jnp.exp(last - acum), jnp.exp(last)
        bfull, cfull = bc[:, 0:LANE], bc[:, LANE:2 * LANE]
        causal = _tri(c, bwd)
        cb = [_mm_nt(jnp.where(lo, cfull, zero), bfull), _mm_nt(jnp.where(lo, zero, cfull), bfull)]

        def colsel(m, ha, hb, o=o):
            return jnp.where(lo, jnp.broadcast_to(m[:, o + ha:o + ha + 1], (c, LANE)),
                             jnp.broadcast_to(m[:, o + hb:o + hb + 1], (c, LANE)))

        for p in range(npair):
            ha, hb = 2 * p, 2 * p + 1
            decs = []
            for h in (ha, hb):
                seg = (jnp.broadcast_to(acum[:, o + h:o + h + 1], (c, c))
                       - jnp.broadcast_to(acum_t[o + h:o + h + 1, :], (c, c)))
                decs.append(jnp.where(causal, jnp.exp(jnp.where(causal, seg, 0.0)), 0.0))
            cdp = jnp.where(lo[0:1], jnp.broadcast_to(cd[:, o + ha:o + ha + 1], (1, LANE)),
                            jnp.broadcast_to(cd[:, o + hb:o + hb + 1], (1, LANE)))
            xdt = xs[:, p * LANE:(p + 1) * LANE] * colsel(dt, ha, hb)
            units.append(dict(
                d=d, p=p, g0=(p // 2 == 0), cdp=cdp, bfull=bfull.astype(BF16),
                cfull=cfull.astype(BF16), xdt=xdt.astype(BF16),
                xde=(xdt * colsel(de, ha, hb)).astype(BF16), eap=colsel(ea, ha, hb),
                sc_a=(cb[p // 2] * decs[0]).astype(BF16), sc_b=(cb[p // 2] * decs[1]).astype(BF16)))
    for u in units:
        u["st"] = st_ref[u["d"], u["p"]]
    yi_a = [_mm(u["sc_a"], u["xdt"]) for u in units]
    yi_b = [_mm(u["sc_b"], u["xdt"]) for u in units]
    yo = [_mm(u["cfull"], u["st"]) for u in units]
    cs = [_mm_tn(u["bfull"], u["xde"]) for u in units]
    ys = [jnp.where(lo, a, b) + o_ * u["eap"] for u, a, b, o_ in zip(units, yi_a, yi_b, yo)]
    for u, x in zip(units, cs):
        keep = grp_rows if u["g0"] else jnp.logical_not(grp_rows)
        st_ref[u["d"], u["p"]] = u["st"] * u["cdp"] + jnp.where(keep, x, 0.0)
    return [jnp.concatenate(ys[d * npair:(d + 1) * npair], axis=1) for d in range(2)]


def _ssd_kernel(xf, bcf, dtf, xb, bcb, dtb, arow_ref, s0_ref, yf_o, yb_o, sfin_o, st_ref):
    j = pl.program_id(1)

    @pl.when(j == 0)
    def _():
        st_ref[...] = s0_ref[0]

    yf, yb = _ssd_step([(xf[0], bcf[0], dtf[0], False), (xb[0], bcb[0], dtb[0], True)],
                       arow_ref[...], st_ref)
    yf_o[0] = yf
    yb_o[0] = yb

    @pl.when(j == pl.num_programs(1) - 1)
    def _():
        sfin_o[0] = st_ref[...]


def _ssd(xs, bc, dt, arow, s0):
    b, l, _ = xs.shape
    c = SSD_CHUNK
    nc = l // c

    def spec(w, rev):
        if rev:
            return pl.BlockSpec((1, c, w), lambda bi, j: (bi, nc - 1 - j, 0))
        return pl.BlockSpec((1, c, w), lambda bi, j: (bi, j, 0))

    st_spec = pl.BlockSpec((1,) + s0.shape[1:], lambda bi, j: (bi, 0, 0, 0, 0))
    return pl.pallas_call(
        _ssd_kernel,
        grid=(b, nc),
        in_specs=[spec(512, False), spec(256, False), spec(128, False),
                  spec(512, True), spec(256, True), spec(128, True),
                  _const_spec(arow.shape), st_spec],
        out_specs=[spec(512, False), spec(512, True), st_spec],
        out_shape=[jax.ShapeDtypeStruct((b, l, 512), F32), jax.ShapeDtypeStruct((b, l, 512), F32),
                   jax.ShapeDtypeStruct(s0.shape, F32)],
        scratch_shapes=[pltpu.VMEM(s0.shape[1:], F32)],
        compiler_params=_cparams(("parallel", "arbitrary")),
        name="ssd",
    )(xs, bc, dt, xs, bc, dt, arow, s0)


def _l1_post_kernel(h_ref, mod_ref, yf, yb, r, k, v, g, ydf, ydb, xs, zg, lnw, lnb, rk, dsum, gn,
                    w_ref, bd_ref, o_ref):
    bd = bd_ref[...]
    y = yf[0] + yb[0]
    mu = _mm_split(y, bd) * (1.0 / NC)
    dl = y - mu
    var = _mm_split(dl * dl, bd) * (1.0 / NC)
    y = dl * lax.rsqrt(var + WKV_LN_EPS) * lnw[...] + lnb[...]
    y = y + _mm_split(r[0] * k[0] * rk[...], bd) * v[0]
    oc = y * g[0]
    x = xs[0]
    yd = (ydf[0] + dsum[0:1] * x) + (ydb[0] + dsum[1:2] * x)
    od = _rms(yd * _silu(zg[0]), gn[...], D_INNER)
    m = _mm(oc, w_ref[0:512, :]) + _mm(od, w_ref[512:1024, :])
    o_ref[0] = h_ref[0] + mod_ref[0][2:3] * m


def _l1_post(h, mod, per_batch, acts, consts, tm):
    b, l, _ = h.shape
    mod_map = (lambda bi, i: (bi, 0, 0)) if per_batch else (lambda bi, i: (0, 0, 0))
    x_spec = pl.BlockSpec((1, tm, D_MODEL), lambda bi, i: (bi, i, 0))
    a_spec = pl.BlockSpec((1, tm, 512), lambda bi, i: (bi, i, 0))
    return pl.pallas_call(
        _l1_post_kernel,
        grid=(b, l // tm),
        in_specs=[x_spec, pl.BlockSpec((1, 6, D_MODEL), mod_map)] + [a_spec] * len(acts)
        + [_const_spec(c.shape) for c in consts],
        out_specs=x_spec,
        out_shape=jax.ShapeDtypeStruct(h.shape, F32),
        compiler_params=_cparams(("parallel", "parallel")),
        name="l1_post",
    )(h, mod, *acts, *consts)


def _rope_tables(length, dim, offset, width):
    quarter = dim // 4
    inv = np.float32(ROPE_BASE) ** (-np.arange(quarter, dtype=np.float32) / np.float32(quarter))
    pos = np.arange(length)
    row = (pos // GRID_W).astype(np.float32)
    col = (pos % GRID_W).astype(np.float32)
    ar, ac = row[:, None] * inv, col[:, None] * inv
    c = np.concatenate([np.cos(ar), np.cos(ar), np.cos(ac), np.cos(ac)], axis=-1)
    s = np.concatenate([-np.sin(ar), np.sin(ar), -np.sin(ac), np.sin(ac)], axis=-1)
    cw = np.ones((length, width), np.float32)
    sw = np.zeros((length, width), np.float32)
    cw[:, offset:offset + dim] = c
    sw[:, offset:offset + dim] = s
    return cw, sw


def _pad_cols(w, n):
    return jnp.pad(w, ((0, 0), (0, n - w.shape[1])))


def _prep_l0(w_in, q_norm, w_uq, kv_norm, w_ukv, w_out):
    a_cols = 3 * HA * 2 * DA
    wqd = _pad_cols(w_in[:, a_cols:a_cols + Q_LORA], 256)
    wckv = w_in[:, a_cols + Q_LORA:a_cols + Q_LORA + KV_LORA]
    wkpe = jnp.pad(w_in[:, a_cols + Q_LORA + KV_LORA:], ((0, 0), (NOPE, LANE - NOPE - ROPE_B)))
    w = jnp.concatenate([w_in[:, :a_cols], wqd, wckv, wkpe], axis=1).astype(BF16)
    qn = _pad_cols(q_norm.reshape(1, -1), 256)
    wuq = jnp.pad(w_uq.reshape(Q_LORA, HB, NOPE + ROPE_B),
                  ((0, 256 - Q_LORA), (0, 0), (0, LANE - NOPE - ROPE_B))).reshape(256, HB * LANE)
    wukv = w_ukv.reshape(KV_LORA, HB, NOPE + VB)
    wk = jnp.pad(wukv[:, :, :NOPE], ((0, 0), (0, 0), (0, LANE - NOPE))).reshape(KV_LORA, HB * LANE)
    wv = wukv[:, :, NOPE:].reshape(KV_LORA, HB * VB)
    return dict(w=w, qn=qn, wuq=wuq.astype(BF16), kvn=kv_norm.reshape(1, -1),
                wukv=jnp.concatenate([wk, wv], axis=1).astype(BF16), w_out=w_out.astype(BF16))


def _prep_l1(w_in, mu, w0, w2, a0, a2, g2, k_k, k_a, r_k, ln_w, ln_b, conv_w, conv_b, A_log,
             dt_bias, D, gnorm, w_out, norm1):
    c_cols = 3 * HC * NC + 4 * 64 + 128
    conv_ch = D_INNER + 2 * G_SSM * NS
    wdt = w_in[:, c_cols + D_INNER + conv_ch:]
    wdt = _pad_cols(jnp.concatenate([wdt, wdt], axis=1), LANE)
    w = jnp.concatenate([w_in[:, :c_cols + D_INNER + conv_ch], wdt], axis=1).astype(BF16)
    z = jnp.zeros((64, HC * NC), F32)
    w2p = jnp.stack([jnp.concatenate([w2[0], z], 0), jnp.concatenate([z, w2[1]], 0)]).astype(BF16)
    a2p = jnp.stack([jnp.concatenate([a2[0], z], 0), jnp.concatenate([z, a2[1]], 0)]).astype(BF16)
    hid = jnp.arange(HC * NC) // NC
    bd = (hid[:, None] == hid[None, :]).astype(BF16)
    dtb = _pad_cols(dt_bias.reshape(1, 2 * HD), LANE)
    arow = _pad_cols((-jnp.exp(A_log.astype(F32))).reshape(1, 2 * HD), LANE)
    return dict(norm1=norm1.reshape(1, -1), w_in=w, mu=mu.reshape(1, -1), k_k=k_k.reshape(1, -1),
                k_a=k_a.reshape(1, -1), w0=w0, w2=w2p, a0=a0, a2=a2p, g2=g2.astype(BF16),
                conv_w=conv_w, conv_b=conv_b.reshape(1, -1), dt_bias=dtb, bd=bd, arow=arow,
                ln_w=ln_w.reshape(1, -1), ln_b=ln_b.reshape(1, -1), r_k=r_k.reshape(1, -1),
                dsum=jnp.repeat(D, PD, axis=1), gnorm=gnorm.reshape(1, -1),
                w_out=w_out.astype(BF16))


def _wkv_state_in(s):
    b = s.shape[0]
    st = s.reshape(b, HC // 2, 2, NC, NC)
    z = jnp.zeros_like(st[:, :, 0])
    top = jnp.concatenate([st[:, :, 0], z], axis=-1)
    bot = jnp.concatenate([z, st[:, :, 1]], axis=-1)
    return jnp.concatenate([top, bot], axis=-2)


def _wkv_state_out(s):
    a = s[:, :, :NC, :NC]
    b_ = s[:, :, NC:, NC:]
    return jnp.stack([a, b_], axis=2).reshape(s.shape[0], HC, NC, NC)


def _ssd_state_in(s):
    b = s.shape[0]
    st = jnp.swapaxes(s, -1, -2).reshape(b, HD // 2, 2, NS, PD)
    pair = jnp.concatenate([st[:, :, 0], st[:, :, 1]], axis=-1)
    z = jnp.zeros_like(pair)
    g0 = jnp.concatenate([pair, z], axis=-2)
    g1 = jnp.concatenate([z, pair], axis=-2)
    is_g0 = (jnp.arange(HD // 2) < HD // 4)[None, :, None, None]
    return jnp.where(is_g0, g0, g1)


def _ssd_state_out(s):
    is_g0 = (jnp.arange(HD // 2) < HD // 4)[None, :, None, None]
    pair = jnp.where(is_g0, s[:, :, :NS, :], s[:, :, NS:, :])
    st = jnp.stack([pair[..., :PD], pair[..., PD:]], axis=2).reshape(s.shape[0], HD, NS, PD)
    return jnp.swapaxes(st, -1, -2)


def _layer0(h, mod, per_batch, p, norm1, ctx, tables, lam_vecs, subln, lambda_init, tm, tq):
    qa, ka, va, qb, ckv, kpe = _l0_proj(h, mod, per_batch, norm1, p["w"], p["qn"], p["wuq"],
                                        p["kvn"], tables, tm)
    own = (ka, va, ckv, kpe)
    if ctx is not None:
        ck, cv, cckv, ckpe = ctx
        ka = jnp.concatenate([ck.astype(BF16), ka], axis=2)
        va = jnp.concatenate([cv.astype(BF16), va], axis=2)
        ckv = jnp.concatenate([cckv, ckv], axis=1)
        kpe = jnp.concatenate([jnp.pad(ckpe, ((0, 0), (0, 0), (NOPE, LANE - NOPE - ROPE_B))), kpe],
                              axis=1)
    kb, vb = _kvup(ckv, kpe, p["wukv"], min(512, ckv.shape[1]))
    nsub = tq // 128
    oa = _attention(qa, ka, va, tq, nsub, True, lam_vecs, subln, lambda_init)
    ob = _attention(qb, kb, vb, tq, nsub, False)
    return _oproj(h, mod, per_batch, oa, ob, p["w_out"], tm), own


def _layer1(h, mod, per_batch, p, wkv0, ssm0, tm):
    (r, k, v, kk, g, lw0, a0, kx0, lw1, a1, kx1, zg, xs, bc, dt) = _l1_proj(h, mod, per_batch, p, tm)
    yf, yb, s_wkv = _wkv(r, kk, v, lw0, a0, kx0, lw1, a1, kx1, wkv0)
    ydf, ydb, s_ssm = _ssd(xs, bc, dt, p["arow"], ssm0)
    consts = [p["ln_w"], p["ln_b"], p["r_k"], p["dsum"], p["gnorm"], p["w_out"], p["bd"]]
    h = _l1_post(h, mod, per_batch, [yf, yb, r, k, v, g, ydf, ydb, xs, zg], consts, tm)
    return h, s_wkv, s_ssm


def kernel(x_prompt, x_sample, cache_l0_k, cache_l0_v, cache_l0_ckv, cache_l0_kpe, state_l1_wkv_fwd, state_l1_wkv_bwd, state_l1_ssm_fwd, state_l1_ssm_bwd, c, c_ctx, ada_w_0, ada_b_0, norm1_0, norm2_0, ffn_up_0, ffn_conv_w_0, ffn_conv_b_0, ffn_down_0, l0_w_in, l0_lambda_q1, l0_lambda_k1, l0_lambda_q2, l0_lambda_k2, l0_subln, l0_q_norm, l0_w_uq, l0_kv_norm, l0_w_ukv, l0_w_out, ada_w_1, ada_b_1, norm1_1, norm2_1, ffn_up_1, ffn_conv_w_1, ffn_conv_b_1, ffn_down_1, l1_w_in, l1_mu, l1_w0, l1_w2, l1_a0, l1_a2, l1_g2, l1_k_k, l1_k_a, l1_r_k, l1_ln_w, l1_ln_b, l1_conv_w, l1_conv_b, l1_A_log, l1_dt_bias, l1_D, l1_gnorm, l1_w_out, norm_f):
    bc_, lc, _ = x_prompt.shape
    bl, ll, _ = x_sample.shape
    tm_c, tm_l = min(256, lc), min(256, ll)
    cond = jnp.concatenate([c_ctx[None, :], c], axis=0)
    cond8 = jnp.pad(cond, ((0, 8 - cond.shape[0]), (0, 0)))
    mods = [_ada(cond8, w, b).reshape(8, 6, D_MODEL) for w, b in ((ada_w_0, ada_b_0), (ada_w_1, ada_b_1))]
    p0 = _prep_l0(l0_w_in, l0_q_norm, l0_w_uq, l0_kv_norm, l0_w_ukv, l0_w_out)
    tables = _rope_tables(ll, DA, 0, DA)
    tables = tuple(np.concatenate([t, t], axis=1) for t in tables) + _rope_tables(ll, ROPE_B, NOPE, LANE)
    tables = tuple(jnp.asarray(t) for t in tables)
    lam_vecs = jnp.pad(jnp.stack([l0_lambda_q1, l0_lambda_k1, l0_lambda_q2, l0_lambda_k2]),
                       ((0, 4), (0, LANE - DA)))
    subln = l0_subln.reshape(1, -1)
    lambda_init = 0.8 - 0.6 * math.exp(-0.3 * 0)
    n1 = norm1_0.reshape(1, -1)
    h_ctx, own = _layer0(x_prompt, mods[0][0:1], False, p0, n1, None, None, lam_vecs, subln,
                         lambda_init, tm_c, min(256, lc))
    h_lat, _ = _layer0(x_sample, mods[0][1:1 + bl], True, p0, n1,
                       (cache_l0_k, cache_l0_v, cache_l0_ckv, cache_l0_kpe), tables, lam_vecs,
                       subln, lambda_init, tm_l, min(256, ll))
    ffn0 = (norm2_0.reshape(1, -1), ffn_up_0.astype(BF16), ffn_conv_w_0, ffn_conv_b_0.reshape(1, -1),
            ffn_down_0.astype(BF16))
    h_ctx = _ffn(h_ctx, mods[0][0:1], False, *ffn0, None, tm_c)
    h_lat = _ffn(h_lat, mods[0][1:1 + bl], True, *ffn0, None, min(512, ll))
    p1 = _prep_l1(l1_w_in, l1_mu, l1_w0, l1_w2, l1_a0, l1_a2, l1_g2, l1_k_k, l1_k_a, l1_r_k,
                  l1_ln_w, l1_ln_b, l1_conv_w, l1_conv_b, l1_A_log, l1_dt_bias, l1_D, l1_gnorm,
                  l1_w_out, norm1_1)
    zero_state = jnp.zeros((bc_, 2, HC // 2, LANE, LANE), F32)
    h_ctx, s_wkv, s_ssm = _layer1(h_ctx, mods[1][0:1], False, p1, zero_state, zero_state, tm_c)
    wkv0 = jnp.stack([_wkv_state_in(state_l1_wkv_fwd), _wkv_state_in(state_l1_wkv_bwd)], axis=1)
    ssm0 = jnp.stack([_ssd_state_in(state_l1_ssm_fwd), _ssd_state_in(state_l1_ssm_bwd)], axis=1)
    h_lat, _, _ = _layer1(h_lat, mods[1][1:1 + bl], True, p1, wkv0, ssm0, tm_l)
    ffn1 = (norm2_1.reshape(1, -1), ffn_up_1.astype(BF16), ffn_conv_w_1, ffn_conv_b_1.reshape(1, -1),
            ffn_down_1.astype(BF16))
    nf = norm_f.reshape(1, -1)
    y_prompt = _ffn(h_ctx, mods[1][0:1], False, *ffn1, nf, tm_c)
    y_sample = _ffn(h_lat, mods[1][1:1 + bl], True, *ffn1, nf, min(512, ll))
    ka, va, ckv, kpe = own
    return (y_prompt, y_sample, ka, va, ckv, kpe[:, :, NOPE:NOPE + ROPE_B],
            _wkv_state_out(s_wkv[:, 0]), _wkv_state_out(s_wkv[:, 1]),
            _ssd_state_out(s_ssm[:, 0]), _ssd_state_out(s_ssm[:, 1]))
```

```python
import functools
import math

import jax
import jax.numpy as jnp
import numpy as np
from jax import lax
from jax.experimental import pallas as pl
from jax.experimental.pallas import tpu as pltpu

F32 = jnp.float32
BF16 = jnp.bfloat16
HIGHEST = lax.Precision.HIGHEST

D_MODEL = 1024
GRID_W = 64
ROPE_BASE = 10000.0
NORM_EPS = 1e-6
HA, DA = 4, 64
HB, NOPE, ROPE_B, VB = 4, 64, 32, 128
Q_LORA, KV_LORA = 192, 128
HC, NC = 8, 64
WKV_LN_EPS = 64e-5
WKV_DECAY_SCALE = 0.606531
HD, PD, G_SSM, NS = 8, 64, 2, 64
D_INNER = HD * PD
F_FF = 2816
LANE = 128
HALO = 8
LOG2E = math.log2(math.e)
WKV_CHUNK = 64
SSD_CHUNK = 128
VMEM_LIMIT = 56 * 1024 * 1024


def _cparams(sem):
    return pltpu.CompilerParams(dimension_semantics=sem, vmem_limit_bytes=VMEM_LIMIT)


def _mm(a, b):
    return jnp.dot(a.astype(BF16), b.astype(BF16), preferred_element_type=F32)


def _mm_nt(a, b):
    return lax.dot_general(a.astype(BF16), b.astype(BF16), (((1,), (1,)), ((), ())),
                           preferred_element_type=F32)


def _mm_tn(a, b):
    return lax.dot_general(a.astype(BF16), b.astype(BF16), (((0,), (0,)), ((), ())),
                           preferred_element_type=F32)


def _mm_f32(a, b):
    return jnp.dot(a, b, precision=HIGHEST, preferred_element_type=F32)


def _mm_split(x, m):
    hi = x.astype(BF16)
    lo = (x - hi.astype(F32)).astype(BF16)
    return (jnp.dot(hi, m, preferred_element_type=F32) + jnp.dot(lo, m, preferred_element_type=F32))


def _rms(x, g, n):
    ms = jnp.sum(x * x, axis=-1, keepdims=True) * (1.0 / n)
    return x * lax.rsqrt(ms + NORM_EPS) * g


def _silu(x):
    return x * jax.nn.sigmoid(x)


def _lane(shape):
    return lax.broadcasted_iota(jnp.int32, shape, len(shape) - 1)


def _row(shape):
    return lax.broadcasted_iota(jnp.int32, shape, len(shape) - 2)


def _const_spec(shape, single=False):
    nd = len(shape)
    if single:
        return pl.BlockSpec(shape, lambda *_: (0,) * nd, pipeline_mode=pl.Buffered(1))
    return pl.BlockSpec(shape, lambda *_: (0,) * nd)


def _ada_kernel(c_ref, w_ref, b_ref, o_ref):
    o_ref[...] = _mm(_silu(c_ref[...]), w_ref[...]) + b_ref[...]


def _ada(cond8, w, b):
    n = w.shape[1]
    tn = 1536
    return pl.pallas_call(
        _ada_kernel,
        grid=(n // tn,),
        in_specs=[_const_spec((8, D_MODEL)),
                  pl.BlockSpec((D_MODEL, tn), lambda j: (0, j)),
                  pl.BlockSpec((1, tn), lambda j: (0, j))],
        out_specs=pl.BlockSpec((8, tn), lambda j: (0, j)),
        out_shape=jax.ShapeDtypeStruct((8, n), F32),
        compiler_params=_cparams(("arbitrary",)),
        name="ada",
    )(cond8, w, b.reshape(1, n))


def _rope(x, c, s, half):
    w = x.shape[-1]
    up = pltpu.roll(x, w - half, 1)
    dn = pltpu.roll(x, half, 1)
    first = (_lane(x.shape) & (2 * half - 1)) < half
    return x * c + jnp.where(first, up, dn) * s


def _l0_proj_kernel(*refs, latent):
    if latent:
        (x_ref, mod_ref, n1_ref, w_ref, qn_ref, wuq_ref, kvn_ref, ca_ref, sa_ref, cb_ref, sb_ref,
         qa_o, ka_o, va_o, qb_o, ckv_o, kpe_o) = refs
    else:
        (x_ref, mod_ref, n1_ref, w_ref, qn_ref, wuq_ref, kvn_ref,
         qa_o, ka_o, va_o, qb_o, ckv_o, kpe_o) = refs
    mod = mod_ref[0]
    hn = _rms(x_ref[0], n1_ref[...], D_MODEL) * (1.0 + mod[1:2]) + mod[0:1]
    proj = _mm(hn, w_ref[...])
    qd = _rms(proj[:, 1536:1792], qn_ref[...], Q_LORA)
    qb = _mm(qd, wuq_ref[...])
    ckv_o[0] = _rms(proj[:, 1792:1920], kvn_ref[...], KV_LORA)
    kpe = proj[:, 1920:2048]
    if latent:
        ca, sa, cb, sb = ca_ref[...], sa_ref[...], cb_ref[...], sb_ref[...]
        kpe = _rope(kpe, cb, sb, ROPE_B // 4)
    kpe_o[0] = kpe
    for h in range(HA):
        sl = slice(h * LANE, (h + 1) * LANE)
        q = proj[:, sl]
        k = proj[:, 512 + h * LANE:512 + (h + 1) * LANE]
        qbh = qb[:, sl]
        if latent:
            q = _rope(q, ca, sa, DA // 4)
            k = _rope(k, ca, sa, DA // 4)
            qbh = _rope(qbh, cb, sb, ROPE_B // 4)
        qa_o[0, h] = (q * (DA ** -0.5 * LOG2E)).astype(qa_o.dtype)
        ka_o[0, h] = k.astype(ka_o.dtype)
        va_o[0, h] = proj[:, 1024 + h * LANE:1024 + (h + 1) * LANE].astype(va_o.dtype)
        qb_o[0, h] = (qbh * ((NOPE + ROPE_B) ** -0.5 * LOG2E)).astype(qb_o.dtype)


def _l0_proj(x, mod, per_batch, n1, w, qn, wuq, kvn, tables, tm):
    b, l, _ = x.shape
    latent = tables is not None
    kv_dtype = BF16 if latent else F32
    mod_map = (lambda bi, i: (bi, 0, 0)) if per_batch else (lambda bi, i: (0, 0, 0))
    in_specs = [pl.BlockSpec((1, tm, D_MODEL), lambda bi, i: (bi, i, 0)),
                pl.BlockSpec((1, 6, D_MODEL), mod_map),
                _const_spec((1, D_MODEL)), _const_spec(w.shape), _const_spec(qn.shape),
                _const_spec(wuq.shape), _const_spec(kvn.shape)]
    args = [x, mod, n1, w, qn, wuq, kvn]
    if latent:
        in_specs += [pl.BlockSpec((tm, LANE), lambda bi, i: (i, 0))] * 4
        args += list(tables)
    head_spec = pl.BlockSpec((1, HA, tm, LANE), lambda bi, i: (bi, 0, i, 0))
    row_spec = pl.BlockSpec((1, tm, LANE), lambda bi, i: (bi, i, 0))
    return pl.pallas_call(
        functools.partial(_l0_proj_kernel, latent=latent),
        grid=(b, l // tm),
        in_specs=in_specs,
        out_specs=[head_spec, head_spec, head_spec, head_spec, row_spec, row_spec],
        out_shape=[jax.ShapeDtypeStruct((b, HA, l, LANE), BF16),
                   jax.ShapeDtypeStruct((b, HA, l, LANE), kv_dtype),
                   jax.ShapeDtypeStruct((b, HA, l, LANE), kv_dtype),
                   jax.ShapeDtypeStruct((b, HB, l, LANE), BF16),
                   jax.ShapeDtypeStruct((b, l, LANE), F32),
                   jax.ShapeDtypeStruct((b, l, LANE), F32)],
        compiler_params=_cparams(("parallel", "parallel")),
        name="l0_proj",
    )(*args)


def _kvup_kernel(ckv_ref, kpe_ref, w_ref, kb_o, vb_o):
    kv = _mm(ckv_ref[0], w_ref[...])
    kpe = kpe_ref[0]
    for h in range(HB):
        kb_o[0, h] = (kv[:, h * LANE:(h + 1) * LANE] + kpe).astype(BF16)
        vb_o[0, h] = kv[:, 512 + h * LANE:512 + (h + 1) * LANE].astype(BF16)


def _kvup(ckv, kpe, w, tk):
    b, lk, _ = ckv.shape
    row_spec = pl.BlockSpec((1, tk, LANE), lambda bi, i: (bi, i, 0))
    head_spec = pl.BlockSpec((1, HB, tk, LANE), lambda bi, i: (bi, 0, i, 0))
    return pl.pallas_call(
        _kvup_kernel,
        grid=(b, lk // tk),
        in_specs=[row_spec, row_spec, _const_spec(w.shape)],
        out_specs=[head_spec, head_spec],
        out_shape=[jax.ShapeDtypeStruct((b, HB, lk, LANE), BF16)] * 2,
        compiler_params=_cparams(("parallel", "parallel")),
        name="kvup",
    )(ckv, kpe, w)


def _attn_kernel(*refs, diff, lambda_init, nsub):
    if diff:
        q_ref, k_ref, v_ref, lam_ref, g_ref, o_ref = refs
    else:
        q_ref, k_ref, v_ref, o_ref = refs
    k = k_ref[0, 0]
    v = v_ref[0, 0]
    ts = q_ref.shape[2] // nsub
    qs = []
    for i in range(nsub):
        q = q_ref[0, 0, i * ts:(i + 1) * ts, :]
        if diff:
            lo = _lane(q.shape) < DA
            zero = jnp.zeros_like(q)
            q = jnp.concatenate([jnp.where(lo, q, zero), jnp.where(lo, zero, q)], axis=0)
        qs.append(q)
    s = [_mm_nt(q, k) for q in qs]
    m = [jnp.max(x, axis=-1, keepdims=True) for x in s]
    p = [jnp.exp2(x - y) for x, y in zip(s, m)]
    inv = [1.0 / jnp.sum(x, axis=-1, keepdims=True) for x in p]
    o = [_mm(x, v) * y for x, y in zip(p, inv)]
    if diff:
        lv = lam_ref[...]
        lam = (jnp.exp(jnp.sum(lv[0:1] * lv[1:2], axis=-1, keepdims=True))
               - jnp.exp(jnp.sum(lv[2:3] * lv[3:4], axis=-1, keepdims=True)) + lambda_init)
        o = [_rms(x[:ts] - lam * x[ts:], g_ref[...], 2 * DA) * (1.0 - lambda_init) for x in o]
    for i in range(nsub):
        o_ref[0, i * ts:(i + 1) * ts, :] = o[i].astype(o_ref.dtype)


def _attention(q, k, v, tq, nsub, diff, lam_vecs=None, subln=None, lambda_init=0.0):
    b, h, l, _ = q.shape
    lk = k.shape[2]
    in_specs = [pl.BlockSpec((1, 1, tq, LANE), lambda bi, hi, i: (bi, hi, i, 0)),
                pl.BlockSpec((1, 1, lk, LANE), lambda bi, hi, i: (bi, hi, 0, 0)),
                pl.BlockSpec((1, 1, lk, LANE), lambda bi, hi, i: (bi, hi, 0, 0))]
    args = [q, k, v]
    if diff:
        in_specs += [_const_spec(lam_vecs.shape), _const_spec(subln.shape)]
        args += [lam_vecs, subln]
    return pl.pallas_call(
        functools.partial(_attn_kernel, diff=diff, lambda_init=lambda_init, nsub=nsub),
        grid=(b, h, l // tq),
        in_specs=in_specs,
        out_specs=pl.BlockSpec((1, tq, LANE), lambda bi, hi, i: (bi, i, hi)),
        out_shape=jax.ShapeDtypeStruct((b, l, h * LANE), BF16),
        compiler_params=_cparams(("parallel", "parallel", "parallel")),
        name="attn_diff" if diff else "attn_mla",
    )(*args)


def _oproj_kernel(h_ref, mod_ref, a_ref, b_ref, w_ref, o_ref):
    half = a_ref.shape[-1]
    m = _mm(a_ref[0], w_ref[0:half, :]) + _mm(b_ref[0], w_ref[half:, :])
    o_ref[0] = h_ref[0] + mod_ref[0][2:3] * m


def _oproj(h, mod, per_batch, oa, ob, w, tm):
    b, l, _ = h.shape
    mod_map = (lambda bi, i: (bi, 0, 0)) if per_batch else (lambda bi, i: (0, 0, 0))
    x_spec = pl.BlockSpec((1, tm, D_MODEL), lambda bi, i: (bi, i, 0))
    half_spec = pl.BlockSpec((1, tm, oa.shape[-1]), lambda bi, i: (bi, i, 0))
    return pl.pallas_call(
        _oproj_kernel,
        grid=(b, l // tm),
        in_specs=[x_spec, pl.BlockSpec((1, 6, D_MODEL), mod_map), half_spec, half_spec,
                  _const_spec(w.shape)],
        out_specs=x_spec,
        out_shape=jax.ShapeDtypeStruct(h.shape, F32),
        compiler_params=_cparams(("parallel", "parallel")),
        name="oproj",
    )(h, mod, oa, ob, w)


def _halo_rows(x_ref, xp_ref, xn_ref):
    i = pl.program_id(1)
    last = pl.num_programs(1) - 1
    xc = jnp.concatenate([xp_ref[0], x_ref[0], xn_ref[0]], axis=0)
    tm = x_ref.shape[1]
    r = _row((tm + 2 * HALO, 1))
    valid = jnp.logical_and(jnp.logical_or(r >= HALO, i > 0),
                            jnp.logical_or(r < tm + HALO, i < last))
    return xc, valid


def _shift_rows(u, tm):
    n = u.shape[0]
    up = pltpu.roll(u, 1, 0)[HALO:HALO + tm]
    dn = pltpu.roll(u, n - 1, 0)[HALO:HALO + tm]
    return up, u[HALO:HALO + tm], dn


def _ffn_kernel(x_ref, xp_ref, xn_ref, mod_ref, n2_ref, wu_ref, cw_ref, cb_ref, wd_ref,
                *rest, final, cw):
    if final:
        nf_ref, o_ref = rest
    else:
        (o_ref,) = rest
    tm = x_ref.shape[1]
    mod = mod_ref[0]
    xc, valid = _halo_rows(x_ref, xp_ref, xn_ref)
    hn = _rms(xc, n2_ref[...], D_MODEL) * (1.0 + mod[4:5]) + mod[3:4]
    hn = jnp.where(valid, hn, 0.0).astype(BF16)
    acc = jnp.zeros((tm, D_MODEL), F32)
    for c in range(F_FF // cw):
        halves = []
        for off in (c * cw, F_FF + c * cw):
            u = jnp.dot(hn, wu_ref[:, off:off + cw], preferred_element_type=F32)
            up, mid, dn = _shift_rows(u, tm)
            w3 = cw_ref[:, off:off + cw]
            halves.append(up * w3[0:1] + mid * w3[1:2] + dn * w3[2:3] + cb_ref[:, off:off + cw])
        act = (_silu(halves[0]) * halves[1]).astype(BF16)
        acc = acc + jnp.dot(act, wd_ref[c * cw:(c + 1) * cw, :], preferred_element_type=F32)
    out = x_ref[0] + mod[5:6] * acc
    if final:
        out = _rms(out, nf_ref[...], D_MODEL)
    o_ref[0] = out


def _ffn(h, mod, per_batch, n2, wu, cw3, cb, wd, nf, tm):
    b, l, _ = h.shape
    nblk = l // tm
    nb = tm // HALO
    mod_map = (lambda bi, i: (bi, 0, 0)) if per_batch else (lambda bi, i: (0, 0, 0))
    x_spec = pl.BlockSpec((1, tm, D_MODEL), lambda bi, i: (bi, i, 0))
    in_specs = [x_spec,
                pl.BlockSpec((1, HALO, D_MODEL), lambda bi, i: (bi, jnp.maximum(i * nb - 1, 0), 0)),
                pl.BlockSpec((1, HALO, D_MODEL),
                             lambda bi, i: (bi, jnp.minimum((i + 1) * nb, nblk * nb - 1), 0)),
                pl.BlockSpec((1, 6, D_MODEL), mod_map),
                _const_spec(n2.shape), _const_spec(wu.shape, True), _const_spec(cw3.shape),
                _const_spec(cb.shape), _const_spec(wd.shape, True)]
    args = [h, h, h, mod, n2, wu, cw3, cb, wd]
    final = nf is not None
    if final:
        in_specs.append(_const_spec(nf.shape))
        args.append(nf)
    cw = 2816
    return pl.pallas_call(
        functools.partial(_ffn_kernel, final=final, cw=cw),
        grid=(b, nblk),
        in_specs=in_specs,
        out_specs=x_spec,
        out_shape=jax.ShapeDtypeStruct(h.shape, F32),
        compiler_params=_cparams(("parallel", "parallel")),
        name="ffn",
    )(*args)


def _softplus(x):
    return jnp.maximum(x, 0.0) + jnp.log1p(jnp.exp(-jnp.abs(x)))


def _l1_proj_kernel(x_ref, xp_ref, xn_ref, mod_ref, n1_ref, w_ref, mu_ref, kk_ref,
                    g2_ref, cw_ref, cb_ref, dtb_ref, bd_ref,
                    r_o, k_o, v_o, kk_o, g_o, wdad_o, zg_o, xs_o, bc_o, dt_o):
    tm = x_ref.shape[1]
    mod = mod_ref[0]
    xc, valid = _halo_rows(x_ref, xp_ref, xn_ref)
    hn = _rms(xc, n1_ref[...], D_MODEL) * (1.0 + mod[1:2]) + mod[0:1]
    hn = jnp.where(valid, hn, 0.0).astype(BF16)
    up, mid, dn = _shift_rows(jnp.dot(hn, w_ref[:, 0:1920], preferred_element_type=F32), tm)
    zc = mid + mu_ref[...] * (0.5 * (up + dn) - mid)
    r, k, v = zc[:, 0:512], zc[:, 512:1024], zc[:, 1024:1536]
    gd = zc[:, 1792:1920]
    r_o[0], k_o[0], v_o[0] = r, k, v
    wdad_o[0] = zc[:, 1536:1792]
    kkf = k * kk_ref[...]
    kk_o[0] = kkf * lax.rsqrt(_mm_split(kkf * kkf, bd_ref[...]) + 1e-12)
    g_o[0] = _mm(jax.nn.sigmoid(gd), g2_ref[...])
    zd = jnp.dot(hn, w_ref[:, 1920:3328], preferred_element_type=F32)
    zg_o[0] = zd[HALO:HALO + tm, 0:512]
    up, mid, dn = _shift_rows(zd[:, 512:1280], tm)
    cw3 = cw_ref[...]
    xbc = _silu(up * cw3[0:1] + mid * cw3[1:2] + dn * cw3[2:3] + cb_ref[...])
    xs_o[0] = xbc[:, 0:512]
    bc_o[0] = xbc[:, 512:768]
    dt_o[0] = _softplus(zd[HALO:HALO + tm, 1280:1408] + dtb_ref[...])


def _l1_proj(x, mod, per_batch, p, tm):
    b, l, _ = x.shape
    nblk = l // tm
    nb = tm // HALO
    mod_map = (lambda bi, i: (bi, 0, 0)) if per_batch else (lambda bi, i: (0, 0, 0))
    x_spec = pl.BlockSpec((1, tm, D_MODEL), lambda bi, i: (bi, i, 0))
    consts = [p["norm1"], p["w_in"], p["mu"], p["k_k"], p["g2"], p["conv_w"], p["conv_b"],
              p["dt_bias"], p["bd"]]
    in_specs = [x_spec,
                pl.BlockSpec((1, HALO, D_MODEL), lambda bi, i: (bi, jnp.maximum(i * nb - 1, 0), 0)),
                pl.BlockSpec((1, HALO, D_MODEL),
                             lambda bi, i: (bi, jnp.minimum((i + 1) * nb, nblk * nb - 1), 0)),
                pl.BlockSpec((1, 6, D_MODEL), mod_map)] + [_const_spec(c.shape) for c in consts]

    def o_spec(w):
        return pl.BlockSpec((1, tm, w), lambda bi, i: (bi, i, 0))

    widths = [512] * 5 + [256, 512, 512, 256, 128]
    return pl.pallas_call(
        _l1_proj_kernel,
        grid=(b, nblk),
        in_specs=in_specs,
        out_specs=[o_spec(w) for w in widths],
        out_shape=[jax.ShapeDtypeStruct((b, l, w), F32) for w in widths],
        compiler_params=_cparams(("parallel", "parallel")),
        name="l1_proj",
    )(x, x, x, mod, *consts)


def _tri(n, upper, strict=False):
    r, c = _row((n, n)), _lane((n, n))
    if upper:
        return (c > r) if strict else (c >= r)
    return (c < r) if strict else (c <= r)


def _wkv_step(dirs, st_ref):
    c = WKV_CHUNK
    npair = HC // 2
    lane2 = _lane((2 * c, LANE))
    row2 = _row((2 * c, LANE))
    lo2 = lane2 < NC
    top2 = row2 < c
    bd2 = lo2 == top2
    tmask = row2 - jnp.where(top2, 0, c)
    imask = lane2 - jnp.where(lo2, 0, NC)
    blk16 = jnp.right_shift(tmask, 4) == jnp.right_shift(imask, 4)
    eye = jnp.logical_and(bd2, tmask == imask).astype(F32)
    lo1 = _lane((c, LANE)) < NC
    units = []
    for d, (r, kk, v, lw, a, kx, bwd) in enumerate(dirs):
        cum = _mm_f32(_tri(c, bwd).astype(F32), lw)
        last = cum[0:1] if bwd else cum[c - 1:c]
        e_in, e_out = jnp.exp(cum - lw), jnp.exp(cum)
        e_neg, e_end = jnp.exp(-cum), jnp.exp(last - cum)
        at, rt = -kk * e_in, r * e_out
        bb = kk * a
        bt, kt = bb * e_neg, kx * e_neg
        bh, kh = bb * e_end, kx * e_end
        wc = jnp.exp(last)
        strict = (imask > tmask) if bwd else (imask < tmask)
        incl = ((imask >= tmask) if bwd else (imask <= tmask))[:c]
        for p in range(npair):
            sl = slice(p * LANE, (p + 1) * LANE)
            units.append(dict(
                d=d, p=p, strict=strict, incl=incl, v=v[:, sl], wc=wc[:, sl],
                l2=jnp.concatenate([at[:, sl], rt[:, sl]], axis=0).astype(BF16),
                bk=jnp.concatenate([bt[:, sl], kt[:, sl]], axis=0).astype(BF16),
                kb=jnp.concatenate([kt[:, sl], bt[:, sl]], axis=0).astype(BF16),
                hat=jnp.concatenate([bh[:, sl], kh[:, sl]], axis=0).astype(BF16)))
    zero = jnp.zeros((2 * c, LANE), BF16)
    for u in units:
        u["st"] = st_ref[u["d"], u["p"]]
    g_a = [_mm_nt(jnp.where(lo2, u["l2"], zero), u["bk"]) for u in units]
    g_b = [_mm_nt(jnp.where(lo2, zero, u["l2"]), u["kb"]) for u in units]
    ga = [jnp.where(u["strict"], jnp.concatenate([x[:c], y[:c]], axis=0), 0.0)
          for u, x, y in zip(units, g_a, g_b)]
    gr_a = [jnp.where(u["incl"], x[c:], 0.0).astype(BF16) for u, x in zip(units, g_a)]
    gr_b = [jnp.where(u["incl"], y[c:], 0.0).astype(BF16) for u, y in zip(units, g_b)]
    n_p = [jnp.where(bd2, x, 0.0) for x in ga]
    ak = [jnp.where(bd2, 0.0, x).astype(BF16) for x in ga]
    nd = [jnp.where(blk16, x, 0.0) for x in n_p]
    no = [(x - y).astype(BF16) for x, y in zip(n_p, nd)]
    ndb = [x.astype(BF16) for x in nd]
    n2 = [_mm(x, x).astype(BF16) for x in ndb]
    n4 = [_mm(x, x).astype(BF16) for x in n2]
    n8 = [_mm(x, x).astype(BF16) for x in n4]
    dinv = [eye + x for x in nd]
    dinv = [x + _mm(x, y) for x, y in zip(dinv, n2)]
    dinv = [x + _mm(x, y) for x, y in zip(dinv, n4)]
    dinv = [(x + _mm(x, y)).astype(BF16) for x, y in zip(dinv, n8)]
    m1 = [_mm(x, y) for x, y in zip(dinv, no)]
    m2 = [_mm(x, x) for x in m1]
    t_p = [eye + x for x in m1]
    t_p = [x + _mm(x, y) for x, y in zip(t_p, m2)]
    t_p = [_mm(x, y).astype(BF16) for x, y in zip(t_p, dinv)]
    akv = [_mm(x, jnp.concatenate([u["v"], u["v"]], axis=0)) for u, x in zip(units, ak)]
    base = [_mm_nt(u["l2"], u["st"]) for u in units]
    us = [_mm(t, jnp.concatenate([b[:c], b[:c]], axis=0) + x) for t, b, x in zip(t_p, base, akv)]
    uu = [jnp.where(lo1, x[:c], x[c:]) for x in us]
    uv = [jnp.concatenate([x, u["v"]], axis=0).astype(BF16) for u, x in zip(units, uu)]
    vu = [jnp.concatenate([u["v"], x], axis=0).astype(BF16) for u, x in zip(units, uu)]
    ys = [b[c:] + jnp.where(lo1, _mm(x, p), _mm(y, q))
          for b, x, y, p, q in zip(base, gr_a, gr_b, uv, vu)]
    upd = [_mm_tn(x, u["hat"]) for u, x in zip(units, uv)]
    for u, x in zip(units, upd):
        st_ref[u["d"], u["p"]] = u["st"] * u["wc"] + jnp.where(bd2, x, 0.0)
    return [jnp.concatenate(ys[d * npair:(d + 1) * npair], axis=1) for d in range(2)]


def _wkv_kernel(rf, kkf, vf, kf, wf, rb, kkb, vb, kb, wb, w0_ref, w2_ref, a0_ref, a2_ref, ka_ref,
                s0_ref, yf_o, yb_o, sfin_o, st_ref):
    j = pl.program_id(1)

    @pl.when(j == 0)
    def _():
        st_ref[...] = s0_ref[0]

    dirs = []
    for i, (r, kk, v, k, wdad) in enumerate(((rf, kkf, vf, kf, wf), (rb, kkb, vb, kb, wb))):
        wd, ad = wdad[0][:, 0:LANE], wdad[0][:, LANE:2 * LANE]
        lw = -WKV_DECAY_SCALE * jax.nn.sigmoid(w0_ref[i:i + 1] + _mm(jnp.tanh(wd), w2_ref[i]))
        a = jax.nn.sigmoid(a0_ref[i:i + 1] + _mm(ad, a2_ref[i]))
        kx = k[0] * (1.0 + (a - 1.0) * ka_ref[...])
        dirs.append((r[0], kk[0], v[0], lw, a, kx, i == 1))
    yf, yb = _wkv_step(dirs, st_ref)
    yf_o[0] = yf
    yb_o[0] = yb

    @pl.when(j == pl.num_programs(1) - 1)
    def _():
        sfin_o[0] = st_ref[...]


def _wkv(r, kk, v, k, wdad, p, s0):
    b, l, w = r.shape
    c = WKV_CHUNK
    nc = l // c
    fwd = pl.BlockSpec((1, c, w), lambda bi, j: (bi, j, 0))
    bwd = pl.BlockSpec((1, c, w), lambda bi, j: (bi, nc - 1 - j, 0))
    fwd2 = pl.BlockSpec((1, c, 2 * LANE), lambda bi, j: (bi, j, 0))
    bwd2 = pl.BlockSpec((1, c, 2 * LANE), lambda bi, j: (bi, nc - 1 - j, 0))
    st_spec = pl.BlockSpec((1,) + s0.shape[1:], lambda bi, j: (bi, 0, 0, 0, 0))
    consts = [p["w0"], p["w2"], p["a0"], p["a2"], p["k_a"]]
    return pl.pallas_call(
        _wkv_kernel,
        grid=(b, nc),
        in_specs=[fwd] * 4 + [fwd2] + [bwd] * 4 + [bwd2] + [_const_spec(x.shape) for x in consts]
        + [st_spec],
        out_specs=[fwd, bwd, st_spec],
        out_shape=[jax.ShapeDtypeStruct((b, l, w), F32), jax.ShapeDtypeStruct((b, l, w), F32),
                   jax.ShapeDtypeStruct(s0.shape, F32)],
        scratch_shapes=[pltpu.VMEM(s0.shape[1:], F32)],
        compiler_params=_cparams(("parallel", "arbitrary")),
        name="wkv",
    )(r, kk, v, k, wdad, r, kk, v, k, wdad, *consts, s0)


def _ssd_step(dirs, arow, st_ref):
    c = SSD_CHUNK
    npair = HD // 2
    lane = _lane((c, LANE))
    lo = lane < PD
    grp_rows = _row((c, LANE)) < NS
    zero = jnp.zeros((c, LANE), F32)
    units = []
    for d, (xs, bc, dt, bwd) in enumerate(dirs):
        o = HD * d
        acum = _mm_f32(_tri(c, bwd).astype(F32), dt * arow)
        acum_t = acum.T
        last = acum[0:1] if bwd else acum[c - 1:c]
        ea, de, cd = jnp.exp(acum), jnp.exp(last - acum), jnp.exp(last)
        bfull, cfull = bc[:, 0:LANE], bc[:, LANE:2 * LANE]
        causal = _tri(c, bwd)
        cb = [_mm_nt(jnp.where(lo, cfull, zero), bfull), _mm_nt(jnp.where(lo, zero, cfull), bfull)]

        def colsel(m, ha, hb, o=o):
            return jnp.where(lo, jnp.broadcast_to(m[:, o + ha:o + ha + 1], (c, LANE)),
                             jnp.broadcast_to(m[:, o + hb:o + hb + 1], (c, LANE)))

        for p in range(npair):
            ha, hb = 2 * p, 2 * p + 1
            decs = []
            for h in (ha, hb):
                seg = (jnp.broadcast_to(acum[:, o + h:o + h + 1], (c, c))
                       - jnp.broadcast_to(acum_t[o + h:o + h + 1, :], (c, c)))
                decs.append(jnp.where(causal, jnp.exp(jnp.where(causal, seg, 0.0)), 0.0))
            cdp = jnp.where(lo[0:1], jnp.broadcast_to(cd[:, o + ha:o + ha + 1], (1, LANE)),
                            jnp.broadcast_to(cd[:, o + hb:o + hb + 1], (1, LANE)))
            xdt = xs[:, p * LANE:(p + 1) * LANE] * colsel(dt, ha, hb)
            units.append(dict(
                d=d, p=p, g0=(p // 2 == 0), cdp=cdp, bfull=bfull.astype(BF16),
                cfull=cfull.astype(BF16), xdt=xdt.astype(BF16),
                xde=(xdt * colsel(de, ha, hb)).astype(BF16), eap=colsel(ea, ha, hb),
                sc_a=(cb[p // 2] * decs[0]).astype(BF16), sc_b=(cb[p // 2] * decs[1]).astype(BF16)))
    for u in units:
        u["st"] = st_ref[u["d"], u["p"]]
    yi_a = [_mm(u["sc_a"], u["xdt"]) for u in units]
    yi_b = [_mm(u["sc_b"], u["xdt"]) for u in units]
    yo = [_mm(u["cfull"], u["st"]) for u in units]
    cs = [_mm_tn(u["bfull"], u["xde"]) for u in units]
    ys = [jnp.where(lo, a, b) + o_ * u["eap"] for u, a, b, o_ in zip(units, yi_a, yi_b, yo)]
    for u, x in zip(units, cs):
        keep = grp_rows if u["g0"] else jnp.logical_not(grp_rows)
        st_ref[u["d"], u["p"]] = u["st"] * u["cdp"] + jnp.where(keep, x, 0.0)
    return [jnp.concatenate(ys[d * npair:(d + 1) * npair], axis=1) for d in range(2)]


def _ssd_kernel(xf, bcf, dtf, xb, bcb, dtb, arow_ref, s0_ref, yf_o, yb_o, sfin_o, st_ref):
    j = pl.program_id(1)

    @pl.when(j == 0)
    def _():
        st_ref[...] = s0_ref[0]

    yf, yb = _ssd_step([(xf[0], bcf[0], dtf[0], False), (xb[0], bcb[0], dtb[0], True)],
                       arow_ref[...], st_ref)
    yf_o[0] = yf
    yb_o[0] = yb

    @pl.when(j == pl.num_programs(1) - 1)
    def _():
        sfin_o[0] = st_ref[...]


def _ssd(xs, bc, dt, arow, s0):
    b, l, _ = xs.shape
    c = SSD_CHUNK
    nc = l // c

    def spec(w, rev):
        if rev:
            return pl.BlockSpec((1, c, w), lambda bi, j: (bi, nc - 1 - j, 0))
        return pl.BlockSpec((1, c, w), lambda bi, j: (bi, j, 0))

    st_spec = pl.BlockSpec((1,) + s0.shape[1:], lambda bi, j: (bi, 0, 0, 0, 0))
    return pl.pallas_call(
        _ssd_kernel,
        grid=(b, nc),
        in_specs=[spec(512, False), spec(256, False), spec(128, False),
                  spec(512, True), spec(256, True), spec(128, True),
                  _const_spec(arow.shape), st_spec],
        out_specs=[spec(512, False), spec(512, True), st_spec],
        out_shape=[jax.ShapeDtypeStruct((b, l, 512), F32), jax.ShapeDtypeStruct((b, l, 512), F32),
                   jax.ShapeDtypeStruct(s0.shape, F32)],
        scratch_shapes=[pltpu.VMEM(s0.shape[1:], F32)],
        compiler_params=_cparams(("parallel", "arbitrary")),
        name="ssd",
    )(xs, bc, dt, xs, bc, dt, arow, s0)


def _l1_post_kernel(h_ref, mod_ref, yf, yb, r, k, v, g, ydf, ydb, xs, zg, lnw, lnb, rk, dsum, gn,
                    w_ref, bd_ref, o_ref):
    bd = bd_ref[...]
    y = yf[0] + yb[0]
    mu = _mm_split(y, bd) * (1.0 / NC)
    dl = y - mu
    var = _mm_split(dl * dl, bd) * (1.0 / NC)
    y = dl * lax.rsqrt(var + WKV_LN_EPS) * lnw[...] + lnb[...]
    y = y + _mm_split(r[0] * k[0] * rk[...], bd) * v[0]
    oc = y * g[0]
    x = xs[0]
    yd = (ydf[0] + dsum[0:1] * x) + (ydb[0] + dsum[1:2] * x)
    od = _rms(yd * _silu(zg[0]), gn[...], D_INNER)
    m = _mm(oc, w_ref[0:512, :]) + _mm(od, w_ref[512:1024, :])
    o_ref[0] = h_ref[0] + mod_ref[0][2:3] * m


def _l1_post(h, mod, per_batch, acts, consts, tm):
    b, l, _ = h.shape
    mod_map = (lambda bi, i: (bi, 0, 0)) if per_batch else (lambda bi, i: (0, 0, 0))
    x_spec = pl.BlockSpec((1, tm, D_MODEL), lambda bi, i: (bi, i, 0))
    a_spec = pl.BlockSpec((1, tm, 512), lambda bi, i: (bi, i, 0))
    return pl.pallas_call(
        _l1_post_kernel,
        grid=(b, l // tm),
        in_specs=[x_spec, pl.BlockSpec((1, 6, D_MODEL), mod_map)] + [a_spec] * len(acts)
        + [_const_spec(c.shape) for c in consts],
        out_specs=x_spec,
        out_shape=jax.ShapeDtypeStruct(h.shape, F32),
        compiler_params=_cparams(("parallel", "parallel")),
        name="l1_post",
    )(h, mod, *acts, *consts)


def _rope_tables(length, dim, offset, width):
    quarter = dim // 4
    inv = np.float32(ROPE_BASE) ** (-np.arange(quarter, dtype=np.float32) / np.float32(quarter))
    pos = np.arange(length)
    row = (pos // GRID_W).astype(np.float32)
    col = (pos % GRID_W).astype(np.float32)
    ar, ac = row[:, None] * inv, col[:, None] * inv
    c = np.concatenate([np.cos(ar), np.cos(ar), np.cos(ac), np.cos(ac)], axis=-1)
    s = np.concatenate([-np.sin(ar), np.sin(ar), -np.sin(ac), np.sin(ac)], axis=-1)
    cw = np.ones((length, width), np.float32)
    sw = np.zeros((length, width), np.float32)
    cw[:, offset:offset + dim] = c
    sw[:, offset:offset + dim] = s
    return cw, sw


def _pad_cols(w, n):
    return jnp.pad(w, ((0, 0), (0, n - w.shape[1])))


def _prep_l0(w_in, q_norm, w_uq, kv_norm, w_ukv, w_out):
    a_cols = 3 * HA * 2 * DA
    wqd = _pad_cols(w_in[:, a_cols:a_cols + Q_LORA], 256)
    wckv = w_in[:, a_cols + Q_LORA:a_cols + Q_LORA + KV_LORA]
    wkpe = jnp.pad(w_in[:, a_cols + Q_LORA + KV_LORA:], ((0, 0), (NOPE, LANE - NOPE - ROPE_B)))
    w = jnp.concatenate([w_in[:, :a_cols], wqd, wckv, wkpe], axis=1).astype(BF16)
    qn = _pad_cols(q_norm.reshape(1, -1), 256)
    wuq = jnp.pad(w_uq.reshape(Q_LORA, HB, NOPE + ROPE_B),
                  ((0, 256 - Q_LORA), (0, 0), (0, LANE - NOPE - ROPE_B))).reshape(256, HB * LANE)
    wukv = w_ukv.reshape(KV_LORA, HB, NOPE + VB)
    wk = jnp.pad(wukv[:, :, :NOPE], ((0, 0), (0, 0), (0, LANE - NOPE))).reshape(KV_LORA, HB * LANE)
    wv = wukv[:, :, NOPE:].reshape(KV_LORA, HB * VB)
    return dict(w=w, qn=qn, wuq=wuq.astype(BF16), kvn=kv_norm.reshape(1, -1),
                wukv=jnp.concatenate([wk, wv], axis=1).astype(BF16), w_out=w_out.astype(BF16))


def _prep_l1(w_in, mu, w0, w2, a0, a2, g2, k_k, k_a, r_k, ln_w, ln_b, conv_w, conv_b, A_log,
             dt_bias, D, gnorm, w_out, norm1):
    c_cols = 3 * HC * NC + 4 * 64 + 128
    conv_ch = D_INNER + 2 * G_SSM * NS
    wdt = w_in[:, c_cols + D_INNER + conv_ch:]
    wdt = _pad_cols(jnp.concatenate([wdt, wdt], axis=1), LANE)
    w = jnp.concatenate([w_in[:, :c_cols + D_INNER + conv_ch], wdt], axis=1).astype(BF16)
    z = jnp.zeros((64, HC * NC), F32)
    w2p = jnp.stack([jnp.concatenate([w2[0], z], 0), jnp.concatenate([z, w2[1]], 0)]).astype(BF16)
    a2p = jnp.stack([jnp.concatenate([a2[0], z], 0), jnp.concatenate([z, a2[1]], 0)]).astype(BF16)
    hid = jnp.arange(HC * NC) // NC
    bd = (hid[:, None] == hid[None, :]).astype(BF16)
    dtb = _pad_cols(dt_bias.reshape(1, 2 * HD), LANE)
    arow = _pad_cols((-jnp.exp(A_log.astype(F32))).reshape(1, 2 * HD), LANE)
    return dict(norm1=norm1.reshape(1, -1), w_in=w, mu=mu.reshape(1, -1), k_k=k_k.reshape(1, -1),
                k_a=k_a.reshape(1, -1), w0=w0, w2=w2p, a0=a0, a2=a2p, g2=g2.astype(BF16),
                conv_w=conv_w, conv_b=conv_b.reshape(1, -1), dt_bias=dtb, bd=bd, arow=arow,
                ln_w=ln_w.reshape(1, -1), ln_b=ln_b.reshape(1, -1), r_k=r_k.reshape(1, -1),
                dsum=jnp.repeat(D, PD, axis=1), gnorm=gnorm.reshape(1, -1),
                w_out=w_out.astype(BF16))


def _wkv_state_in(s):
    b = s.shape[0]
    st = s.reshape(b, HC // 2, 2, NC, NC)
    z = jnp.zeros_like(st[:, :, 0])
    top = jnp.concatenate([st[:, :, 0], z], axis=-1)
    bot = jnp.concatenate([z, st[:, :, 1]], axis=-1)
    return jnp.concatenate([top, bot], axis=-2)


def _wkv_state_out(s):
    a = s[:, :, :NC, :NC]
    b_ = s[:, :, NC:, NC:]
    return jnp.stack([a, b_], axis=2).reshape(s.shape[0], HC, NC, NC)


def _ssd_state_in(s):
    b = s.shape[0]
    st = jnp.swapaxes(s, -1, -2).reshape(b, HD // 2, 2, NS, PD)
    pair = jnp.concatenate([st[:, :, 0], st[:, :, 1]], axis=-1)
    z = jnp.zeros_like(pair)
    g0 = jnp.concatenate([pair, z], axis=-2)
    g1 = jnp.concatenate([z, pair], axis=-2)
    is_g0 = (jnp.arange(HD // 2) < HD // 4)[None, :, None, None]
    return jnp.where(is_g0, g0, g1)


def _ssd_state_out(s):
    is_g0 = (jnp.arange(HD // 2) < HD // 4)[None, :, None, None]
    pair = jnp.where(is_g0, s[:, :, :NS, :], s[:, :, NS:, :])
    st = jnp.stack([pair[..., :PD], pair[..., PD:]], axis=2).reshape(s.shape[0], HD, NS, PD)
    return jnp.swapaxes(st, -1, -2)


def _layer0(h, mod, per_batch, p, norm1, ctx, tables, lam_vecs, subln, lambda_init, tm, tq):
    qa, ka, va, qb, ckv, kpe = _l0_proj(h, mod, per_batch, norm1, p["w"], p["qn"], p["wuq"],
                                        p["kvn"], tables, tm)
    own = (ka, va, ckv, kpe)
    if ctx is not None:
        ck, cv, cckv, ckpe = ctx
        ka = jnp.concatenate([ck.astype(BF16), ka], axis=2)
        va = jnp.concatenate([cv.astype(BF16), va], axis=2)
        ckv = jnp.concatenate([cckv, ckv], axis=1)
        kpe = jnp.concatenate([jnp.pad(ckpe, ((0, 0), (0, 0), (NOPE, LANE - NOPE - ROPE_B))), kpe],
                              axis=1)
    kb, vb = _kvup(ckv, kpe, p["wukv"], min(512, ckv.shape[1]))
    nsub = tq // 128
    oa = _attention(qa, ka, va, tq, nsub, True, lam_vecs, subln, lambda_init)
    ob = _attention(qb, kb, vb, tq, nsub, False)
    return _oproj(h, mod, per_batch, oa, ob, p["w_out"], min(2 * tm, h.shape[1])), own


def _layer1(h, mod, per_batch, p, wkv0, ssm0, tm):
    r, k, v, kk, g, wdad, zg, xs, bc, dt = _l1_proj(h, mod, per_batch, p, tm)
    yf, yb, s_wkv = _wkv(r, kk, v, k, wdad, p, wkv0)
    ydf, ydb, s_ssm = _ssd(xs, bc, dt, p["arow"], ssm0)
    consts = [p["ln_w"], p["ln_b"], p["r_k"], p["dsum"], p["gnorm"], p["w_out"], p["bd"]]
    h = _l1_post(h, mod, per_batch, [yf, yb, r, k, v, g, ydf, ydb, xs, zg], consts,
                 min(2 * tm, h.shape[1]))
    return h, s_wkv, s_ssm


def kernel(x_prompt, x_sample, cache_l0_k, cache_l0_v, cache_l0_ckv, cache_l0_kpe, state_l1_wkv_fwd, state_l1_wkv_bwd, state_l1_ssm_fwd, state_l1_ssm_bwd, c, c_ctx, ada_w_0, ada_b_0, norm1_0, norm2_0, ffn_up_0, ffn_conv_w_0, ffn_conv_b_0, ffn_down_0, l0_w_in, l0_lambda_q1, l0_lambda_k1, l0_lambda_q2, l0_lambda_k2, l0_subln, l0_q_norm, l0_w_uq, l0_kv_norm, l0_w_ukv, l0_w_out, ada_w_1, ada_b_1, norm1_1, norm2_1, ffn_up_1, ffn_conv_w_1, ffn_conv_b_1, ffn_down_1, l1_w_in, l1_mu, l1_w0, l1_w2, l1_a0, l1_a2, l1_g2, l1_k_k, l1_k_a, l1_r_k, l1_ln_w, l1_ln_b, l1_conv_w, l1_conv_b, l1_A_log, l1_dt_bias, l1_D, l1_gnorm, l1_w_out, norm_f):
    bc_, lc, _ = x_prompt.shape
    bl, ll, _ = x_sample.shape
    tm_c, tm_l = min(256, lc), min(256, ll)
    cond = jnp.concatenate([c_ctx[None, :], c], axis=0)
    cond8 = jnp.pad(cond, ((0, 8 - cond.shape[0]), (0, 0)))
    mods = [_ada(cond8, w, b).reshape(8, 6, D_MODEL) for w, b in ((ada_w_0, ada_b_0), (ada_w_1, ada_b_1))]
    p0 = _prep_l0(l0_w_in, l0_q_norm, l0_w_uq, l0_kv_norm, l0_w_ukv, l0_w_out)
    tables = _rope_tables(ll, DA, 0, DA)
    tables = tuple(np.concatenate([t, t], axis=1) for t in tables) + _rope_tables(ll, ROPE_B, NOPE, LANE)
    tables = tuple(jnp.asarray(t) for t in tables)
    lam_vecs = jnp.pad(jnp.stack([l0_lambda_q1, l0_lambda_k1, l0_lambda_q2, l0_lambda_k2]),
                       ((0, 4), (0, LANE - DA)))
    subln = l0_subln.reshape(1, -1)
    lambda_init = 0.8 - 0.6 * math.exp(-0.3 * 0)
    n1 = norm1_0.reshape(1, -1)
    h_ctx, own = _layer0(x_prompt, mods[0][0:1], False, p0, n1, None, None, lam_vecs, subln,
                         lambda_init, tm_c, min(256, lc))
    h_lat, _ = _layer0(x_sample, mods[0][1:1 + bl], True, p0, n1,
                       (cache_l0_k, cache_l0_v, cache_l0_ckv, cache_l0_kpe), tables, lam_vecs,
                       subln, lambda_init, tm_l, min(256, ll))
    ffn0 = (norm2_0.reshape(1, -1), ffn_up_0.astype(BF16), ffn_conv_w_0, ffn_conv_b_0.reshape(1, -1),
            ffn_down_0.astype(BF16))
    h_ctx = _ffn(h_ctx, mods[0][0:1], False, *ffn0, None, tm_c)
    h_lat = _ffn(h_lat, mods[0][1:1 + bl], True, *ffn0, None, min(512, ll))
    p1 = _prep_l1(l1_w_in, l1_mu, l1_w0, l1_w2, l1_a0, l1_a2, l1_g2, l1_k_k, l1_k_a, l1_r_k,
                  l1_ln_w, l1_ln_b, l1_conv_w, l1_conv_b, l1_A_log, l1_dt_bias, l1_D, l1_gnorm,
                  l1_w_out, norm1_1)
    zero_state = jnp.zeros((bc_, 2, HC // 2, LANE, LANE), F32)
    h_ctx, s_wkv, s_ssm = _layer1(h_ctx, mods[1][0:1], False, p1, zero_state, zero_state, tm_c)
    wkv0 = jnp.stack([_wkv_state_in(state_l1_wkv_fwd), _wkv_state_in(state_l1_wkv_bwd)], axis=1)
    ssm0 = jnp.stack([_ssd_state_in(state_l1_ssm_fwd), _ssd_state_in(state_l1_ssm_bwd)], axis=1)
    h_lat, _, _ = _layer1(h_lat, mods[1][1:1 + bl], True, p1, wkv0, ssm0, tm_l)
    ffn1 = (norm2_1.reshape(1, -1), ffn_up_1.astype(BF16), ffn_conv_w_1, ffn_conv_b_1.reshape(1, -1),
            ffn_down_1.astype(BF16))
    nf = norm_f.reshape(1, -1)
    y_prompt = _ffn(h_ctx, mods[1][0:1], False, *ffn1, nf, tm_c)
    y_sample = _ffn(h_lat, mods[1][1:1 + bl], True, *ffn1, nf, min(512, ll))
    ka, va, ckv, kpe = own
    return (y_prompt, y_sample, ka, va, ckv, kpe[:, :, NOPE:NOPE + ROPE_B],
            _wkv_state_out(s_wkv[:, 0]), _wkv_state_out(s_wkv[:, 1]),
            _ssd_state_out(s_ssm[:, 0]), _ssd_state_out(s_ssm[:, 1]))
```

```python
import functools
import math

import jax
import jax.numpy as jnp
import numpy as np
from jax import lax
from jax.experimental import pallas as pl
from jax.experimental.pallas import tpu as pltpu

F32 = jnp.float32
BF16 = jnp.bfloat16
HIGHEST = lax.Precision.HIGHEST

D_MODEL = 1024
GRID_W = 64
ROPE_BASE = 10000.0
NORM_EPS = 1e-6
HA, DA = 4, 64
HB, NOPE, ROPE_B, VB = 4, 64, 32, 128
Q_LORA, KV_LORA = 192, 128
HC, NC = 8, 64
WKV_LN_EPS = 64e-5
WKV_DECAY_SCALE = 0.606531
HD, PD, G_SSM, NS = 8, 64, 2, 64
D_INNER = HD * PD
F_FF = 2816
LANE = 128
HALO = 8
LOG2E = math.log2(math.e)
WKV_CHUNK = 64
SSD_CHUNK = 128
VMEM_LIMIT = 56 * 1024 * 1024


def _cparams(sem):
    return pltpu.CompilerParams(dimension_semantics=sem, vmem_limit_bytes=VMEM_LIMIT)


def _mm(a, b):
    return jnp.dot(a.astype(BF16), b.astype(BF16), preferred_element_type=F32)


def _mm_nt(a, b):
    return lax.dot_general(a.astype(BF16), b.astype(BF16), (((1,), (1,)), ((), ())),
                           preferred_element_type=F32)


def _mm_tn(a, b):
    return lax.dot_general(a.astype(BF16), b.astype(BF16), (((0,), (0,)), ((), ())),
                           preferred_element_type=F32)


def _mm_f32(a, b):
    return jnp.dot(a, b, precision=HIGHEST, preferred_element_type=F32)


def _mm_split(x, m):
    hi = x.astype(BF16)
    lo = (x - hi.astype(F32)).astype(BF16)
    return (jnp.dot(hi, m, preferred_element_type=F32) + jnp.dot(lo, m, preferred_element_type=F32))


def _rms(x, g, n):
    ms = jnp.sum(x * x, axis=-1, keepdims=True) * (1.0 / n)
    return x * lax.rsqrt(ms + NORM_EPS) * g


def _silu(x):
    return x * jax.nn.sigmoid(x)


def _lane(shape):
    return lax.broadcasted_iota(jnp.int32, shape, len(shape) - 1)


def _row(shape):
    return lax.broadcasted_iota(jnp.int32, shape, len(shape) - 2)


def _const_spec(shape, single=False):
    nd = len(shape)
    if single:
        return pl.BlockSpec(shape, lambda *_: (0,) * nd, pipeline_mode=pl.Buffered(1))
    return pl.BlockSpec(shape, lambda *_: (0,) * nd)


def _ada_kernel(c_ref, w_ref, b_ref, o_ref):
    o_ref[...] = _mm(_silu(c_ref[...]), w_ref[...]) + b_ref[...]


def _ada(cond8, w, b):
    n = w.shape[1]
    tn = 1536
    return pl.pallas_call(
        _ada_kernel,
        grid=(n // tn,),
        in_specs=[_const_spec((8, D_MODEL)),
                  pl.BlockSpec((D_MODEL, tn), lambda j: (0, j)),
                  pl.BlockSpec((1, tn), lambda j: (0, j))],
        out_specs=pl.BlockSpec((8, tn), lambda j: (0, j)),
        out_shape=jax.ShapeDtypeStruct((8, n), F32),
        compiler_params=_cparams(("arbitrary",)),
        name="ada",
    )(cond8, w, b.reshape(1, n))


def _rope(x, c, s, half):
    w = x.shape[-1]
    up = pltpu.roll(x, w - half, 1)
    dn = pltpu.roll(x, half, 1)
    first = (_lane(x.shape) & (2 * half - 1)) < half
    return x * c + jnp.where(first, up, dn) * s


def _l0_proj_kernel(*refs, latent):
    if latent:
        (x_ref, mod_ref, n1_ref, w_ref, qn_ref, wuq_ref, kvn_ref, ca_ref, sa_ref, cb_ref, sb_ref,
         qa_o, ka_o, va_o, qb_o, ckv_o, kpe_o) = refs
    else:
        (x_ref, mod_ref, n1_ref, w_ref, qn_ref, wuq_ref, kvn_ref,
         qa_o, ka_o, va_o, qb_o, ckv_o, kpe_o) = refs
    mod = mod_ref[0]
    hn = _rms(x_ref[0], n1_ref[...], D_MODEL) * (1.0 + mod[1:2]) + mod[0:1]
    proj = _mm(hn, w_ref[...])
    qd = _rms(proj[:, 1536:1792], qn_ref[...], Q_LORA)
    qb = _mm(qd, wuq_ref[...])
    ckv_o[0] = _rms(proj[:, 1792:1920], kvn_ref[...], KV_LORA)
    kpe = proj[:, 1920:2048]
    if latent:
        ca, sa, cb, sb = ca_ref[...], sa_ref[...], cb_ref[...], sb_ref[...]
        kpe = _rope(kpe, cb, sb, ROPE_B // 4)
    kpe_o[0] = kpe
    for h in range(HA):
        sl = slice(h * LANE, (h + 1) * LANE)
        q = proj[:, sl]
        k = proj[:, 512 + h * LANE:512 + (h + 1) * LANE]
        qbh = qb[:, sl]
        if latent:
            q = _rope(q, ca, sa, DA // 4)
            k = _rope(k, ca, sa, DA // 4)
            qbh = _rope(qbh, cb, sb, ROPE_B // 4)
        qa_o[0, h] = (q * (DA ** -0.5 * LOG2E)).astype(qa_o.dtype)
        ka_o[0, h] = k.astype(ka_o.dtype)
        va_o[0, h] = proj[:, 1024 + h * LANE:1024 + (h + 1) * LANE].astype(va_o.dtype)
        qb_o[0, h] = (qbh * ((NOPE + ROPE_B) ** -0.5 * LOG2E)).astype(qb_o.dtype)


def _l0_proj(x, mod, per_batch, n1, w, qn, wuq, kvn, tables, tm):
    b, l, _ = x.shape
    latent = tables is not None
    kv_dtype = BF16 if latent else F32
    mod_map = (lambda bi, i: (bi, 0, 0)) if per_batch else (lambda bi, i: (0, 0, 0))
    in_specs = [pl.BlockSpec((1, tm, D_MODEL), lambda bi, i: (bi, i, 0)),
                pl.BlockSpec((1, 6, D_MODEL), mod_map),
                _const_spec((1, D_MODEL)), _const_spec(w.shape), _const_spec(qn.shape),
                _const_spec(wuq.shape), _const_spec(kvn.shape)]
    args = [x, mod, n1, w, qn, wuq, kvn]
    if latent:
        in_specs += [pl.BlockSpec((tm, LANE), lambda bi, i: (i, 0))] * 4
        args += list(tables)
    head_spec = pl.BlockSpec((1, HA, tm, LANE), lambda bi, i: (bi, 0, i, 0))
    row_spec = pl.BlockSpec((1, tm, LANE), lambda bi, i: (bi, i, 0))
    return pl.pallas_call(
        functools.partial(_l0_proj_kernel, latent=latent),
        grid=(b, l // tm),
        in_specs=in_specs,
        out_specs=[head_spec, head_spec, head_spec, head_spec, row_spec, row_spec],
        out_shape=[jax.ShapeDtypeStruct((b, HA, l, LANE), BF16),
                   jax.ShapeDtypeStruct((b, HA, l, LANE), kv_dtype),
                   jax.ShapeDtypeStruct((b, HA, l, LANE), kv_dtype),
                   jax.ShapeDtypeStruct((b, HB, l, LANE), BF16),
                   jax.ShapeDtypeStruct((b, l, LANE), F32),
                   jax.ShapeDtypeStruct((b, l, LANE), F32)],
        compiler_params=_cparams(("parallel", "parallel")),
        name="l0_proj",
    )(*args)


def _kvup_kernel(ckv_ref, kpe_ref, w_ref, kb_o, vb_o):
    kv = _mm(ckv_ref[0], w_ref[...])
    kpe = kpe_ref[0]
    for h in range(HB):
        kb_o[0, h] = (kv[:, h * LANE:(h + 1) * LANE] + kpe).astype(BF16)
        vb_o[0, h] = kv[:, 512 + h * LANE:512 + (h + 1) * LANE].astype(BF16)


def _kvup(ckv, kpe, w, tk):
    b, lk, _ = ckv.shape
    row_spec = pl.BlockSpec((1, tk, LANE), lambda bi, i: (bi, i, 0))
    head_spec = pl.BlockSpec((1, HB, tk, LANE), lambda bi, i: (bi, 0, i, 0))
    return pl.pallas_call(
        _kvup_kernel,
        grid=(b, lk // tk),
        in_specs=[row_spec, row_spec, _const_spec(w.shape)],
        out_specs=[head_spec, head_spec],
        out_shape=[jax.ShapeDtypeStruct((b, HB, lk, LANE), BF16)] * 2,
        compiler_params=_cparams(("parallel", "parallel")),
        name="kvup",
    )(ckv, kpe, w)


def _attn_kernel(*refs, diff, lambda_init, nsub):
    if diff:
        q_ref, k_ref, v_ref, lam_ref, g_ref, o_ref = refs
    else:
        q_ref, k_ref, v_ref, o_ref = refs
    k = k_ref[0, 0]
    v = v_ref[0, 0]
    ts = q_ref.shape[2] // nsub
    qs = []
    for i in range(nsub):
        q = q_ref[0, 0, i * ts:(i + 1) * ts, :]
        if diff:
            lo = _lane(q.shape) < DA
            zero = jnp.zeros_like(q)
            q = jnp.concatenate([jnp.where(lo, q, zero), jnp.where(lo, zero, q)], axis=0)
        qs.append(q)
    s = [_mm_nt(q, k) for q in qs]
    m = [jnp.max(x, axis=-1, keepdims=True) for x in s]
    p = [jnp.exp2(x - y) for x, y in zip(s, m)]
    inv = [1.0 / jnp.sum(x, axis=-1, keepdims=True) for x in p]
    o = [_mm(x, v) * y for x, y in zip(p, inv)]
    if diff:
        lv = lam_ref[...]
        lam = (jnp.exp(jnp.sum(lv[0:1] * lv[1:2], axis=-1, keepdims=True))
               - jnp.exp(jnp.sum(lv[2:3] * lv[3:4], axis=-1, keepdims=True)) + lambda_init)
        o = [_rms(x[:ts] - lam * x[ts:], g_ref[...], 2 * DA) * (1.0 - lambda_init) for x in o]
    for i in range(nsub):
        o_ref[0, i * ts:(i + 1) * ts, :] = o[i].astype(o_ref.dtype)


def _attention(q, k, v, tq, nsub, diff, lam_vecs=None, subln=None, lambda_init=0.0):
    b, h, l, _ = q.shape
    lk = k.shape[2]
    in_specs = [pl.BlockSpec((1, 1, tq, LANE), lambda bi, hi, i: (bi, hi, i, 0)),
                pl.BlockSpec((1, 1, lk, LANE), lambda bi, hi, i: (bi, hi, 0, 0)),
                pl.BlockSpec((1, 1, lk, LANE), lambda bi, hi, i: (bi, hi, 0, 0))]
    args = [q, k, v]
    if diff:
        in_specs += [_const_spec(lam_vecs.shape), _const_spec(subln.shape)]
        args += [lam_vecs, subln]
    return pl.pallas_call(
        functools.partial(_attn_kernel, diff=diff, lambda_init=lambda_init, nsub=nsub),
        grid=(b, h, l // tq),
        in_specs=in_specs,
        out_specs=pl.BlockSpec((1, tq, LANE), lambda bi, hi, i: (bi, i, hi)),
        out_shape=jax.ShapeDtypeStruct((b, l, h * LANE), BF16),
        compiler_params=_cparams(("parallel", "parallel", "parallel")),
        name="attn_diff" if diff else "attn_mla",
    )(*args)


def _oproj_kernel(h_ref, mod_ref, a_ref, b_ref, w_ref, o_ref):
    half = a_ref.shape[-1]
    m = _mm(a_ref[0], w_ref[0:half, :]) + _mm(b_ref[0], w_ref[half:, :])
    o_ref[0] = h_ref[0] + mod_ref[0][2:3] * m


def _oproj(h, mod, per_batch, oa, ob, w, tm):
    b, l, _ = h.shape
    mod_map = (lambda bi, i: (bi, 0, 0)) if per_batch else (lambda bi, i: (0, 0, 0))
    x_spec = pl.BlockSpec((1, tm, D_MODEL), lambda bi, i: (bi, i, 0))
    half_spec = pl.BlockSpec((1, tm, oa.shape[-1]), lambda bi, i: (bi, i, 0))
    return pl.pallas_call(
        _oproj_kernel,
        grid=(b, l // tm),
        in_specs=[x_spec, pl.BlockSpec((1, 6, D_MODEL), mod_map), half_spec, half_spec,
                  _const_spec(w.shape)],
        out_specs=x_spec,
        out_shape=jax.ShapeDtypeStruct(h.shape, F32),
        compiler_params=_cparams(("parallel", "parallel")),
        name="oproj",
    )(h, mod, oa, ob, w)


def _halo_rows(x_ref, xp_ref, xn_ref):
    i = pl.program_id(1)
    last = pl.num_programs(1) - 1
    xc = jnp.concatenate([xp_ref[0], x_ref[0], xn_ref[0]], axis=0)
    tm = x_ref.shape[1]
    r = _row((tm + 2 * HALO, 1))
    valid = jnp.logical_and(jnp.logical_or(r >= HALO, i > 0),
                            jnp.logical_or(r < tm + HALO, i < last))
    return xc, valid


def _shift_rows(u, tm):
    n = u.shape[0]
    up = pltpu.roll(u, 1, 0)[HALO:HALO + tm]
    dn = pltpu.roll(u, n - 1, 0)[HALO:HALO + tm]
    return up, u[HALO:HALO + tm], dn


def _ffn_kernel(x_ref, xp_ref, xn_ref, mod_ref, n2_ref, wu_ref, cw_ref, cb_ref, wd_ref,
                *rest, final, cw):
    if final:
        nf_ref, o_ref = rest
    else:
        (o_ref,) = rest
    tm = x_ref.shape[1]
    mod = mod_ref[0]
    xc, valid = _halo_rows(x_ref, xp_ref, xn_ref)
    hn = _rms(xc, n2_ref[...], D_MODEL) * (1.0 + mod[4:5]) + mod[3:4]
    hn = jnp.where(valid, hn, 0.0).astype(BF16)
    acc = jnp.zeros((tm, D_MODEL), F32)
    for c in range(F_FF // cw):
        halves = []
        for off in (c * cw, F_FF + c * cw):
            u = jnp.dot(hn, wu_ref[:, off:off + cw], preferred_element_type=F32)
            up, mid, dn = _shift_rows(u, tm)
            w3 = cw_ref[:, off:off + cw]
            halves.append(up * w3[0:1] + mid * w3[1:2] + dn * w3[2:3] + cb_ref[:, off:off + cw])
        act = (_silu(halves[0]) * halves[1]).astype(BF16)
        acc = acc + jnp.dot(act, wd_ref[c * cw:(c + 1) * cw, :], preferred_element_type=F32)
    out = x_ref[0] + mod[5:6] * acc
    if final:
        out = _rms(out, nf_ref[...], D_MODEL)
    o_ref[0] = out


def _ffn(h, mod, per_batch, n2, wu, cw3, cb, wd, nf, tm):
    b, l, _ = h.shape
    nblk = l // tm
    nb = tm // HALO
    mod_map = (lambda bi, i: (bi, 0, 0)) if per_batch else (lambda bi, i: (0, 0, 0))
    x_spec = pl.BlockSpec((1, tm, D_MODEL), lambda bi, i: (bi, i, 0))
    in_specs = [x_spec,
                pl.BlockSpec((1, HALO, D_MODEL), lambda bi, i: (bi, jnp.maximum(i * nb - 1, 0), 0)),
                pl.BlockSpec((1, HALO, D_MODEL),
                             lambda bi, i: (bi, jnp.minimum((i + 1) * nb, nblk * nb - 1), 0)),
                pl.BlockSpec((1, 6, D_MODEL), mod_map),
                _const_spec(n2.shape), _const_spec(wu.shape, True), _const_spec(cw3.shape),
                _const_spec(cb.shape), _const_spec(wd.shape, True)]
    args = [h, h, h, mod, n2, wu, cw3, cb, wd]
    final = nf is not None
    if final:
        in_specs.append(_const_spec(nf.shape))
        args.append(nf)
    cw = 2816
    return pl.pallas_call(
        functools.partial(_ffn_kernel, final=final, cw=cw),
        grid=(b, nblk),
        in_specs=in_specs,
        out_specs=x_spec,
        out_shape=jax.ShapeDtypeStruct(h.shape, F32),
        compiler_params=_cparams(("parallel", "parallel")),
        name="ffn",
    )(*args)


def _softplus(x):
    return jnp.maximum(x, 0.0) + jnp.log1p(jnp.exp(-jnp.abs(x)))


def _l1_proj_kernel(x_ref, xp_ref, xn_ref, mod_ref, n1_ref, w_ref, mu_ref, kk_ref,
                    g2_ref, cw_ref, cb_ref, dtb_ref, bd_ref,
                    r_o, k_o, v_o, kk_o, g_o, wdad_o, zg_o, xs_o, bc_o, dt_o):
    tm = x_ref.shape[1]
    mod = mod_ref[0]
    xc, valid = _halo_rows(x_ref, xp_ref, xn_ref)
    hn = _rms(xc, n1_ref[...], D_MODEL) * (1.0 + mod[1:2]) + mod[0:1]
    hn = jnp.where(valid, hn, 0.0).astype(BF16)
    up, mid, dn = _shift_rows(jnp.dot(hn, w_ref[:, 0:1920], preferred_element_type=F32), tm)
    zc = mid + mu_ref[...] * (0.5 * (up + dn) - mid)
    r, k, v = zc[:, 0:512], zc[:, 512:1024], zc[:, 1024:1536]
    gd = zc[:, 1792:1920]
    r_o[0], k_o[0], v_o[0] = r, k, v
    wdad_o[0] = zc[:, 1536:1792]
    kkf = k * kk_ref[...]
    kk_o[0] = kkf * lax.rsqrt(_mm_split(kkf * kkf, bd_ref[...]) + 1e-12)
    g_o[0] = _mm(jax.nn.sigmoid(gd), g2_ref[...])
    zd = jnp.dot(hn, w_ref[:, 1920:3328], preferred_element_type=F32)
    zg_o[0] = zd[HALO:HALO + tm, 0:512]
    up, mid, dn = _shift_rows(zd[:, 512:1280], tm)
    cw3 = cw_ref[...]
    xbc = _silu(up * cw3[0:1] + mid * cw3[1:2] + dn * cw3[2:3] + cb_ref[...])
    xs_o[0] = xbc[:, 0:512]
    bc_o[0] = xbc[:, 512:768]
    dt_o[0] = _softplus(zd[HALO:HALO + tm, 1280:1408] + dtb_ref[...])


def _l1_proj(x, mod, per_batch, p, tm):
    b, l, _ = x.shape
    nblk = l // tm
    nb = tm // HALO
    mod_map = (lambda bi, i: (bi, 0, 0)) if per_batch else (lambda bi, i: (0, 0, 0))
    x_spec = pl.BlockSpec((1, tm, D_MODEL), lambda bi, i: (bi, i, 0))
    consts = [p["norm1"], p["w_in"], p["mu"], p["k_k"], p["g2"], p["conv_w"], p["conv_b"],
              p["dt_bias"], p["bd"]]
    in_specs = [x_spec,
                pl.BlockSpec((1, HALO, D_MODEL), lambda bi, i: (bi, jnp.maximum(i * nb - 1, 0), 0)),
                pl.BlockSpec((1, HALO, D_MODEL),
                             lambda bi, i: (bi, jnp.minimum((i + 1) * nb, nblk * nb - 1), 0)),
                pl.BlockSpec((1, 6, D_MODEL), mod_map)] + [_const_spec(c.shape) for c in consts]

    def o_spec(w):
        return pl.BlockSpec((1, tm, w), lambda bi, i: (bi, i, 0))

    widths = [512] * 5 + [256, 512, 512, 256, 128]
    return pl.pallas_call(
        _l1_proj_kernel,
        grid=(b, nblk),
        in_specs=in_specs,
        out_specs=[o_spec(w) for w in widths],
        out_shape=[jax.ShapeDtypeStruct((b, l, w), F32) for w in widths],
        compiler_params=_cparams(("parallel", "parallel")),
        name="l1_proj",
    )(x, x, x, mod, *consts)


def _tri(n, upper, strict=False):
    r, c = _row((n, n)), _lane((n, n))
    if upper:
        return (c > r) if strict else (c >= r)
    return (c < r) if strict else (c <= r)


def _wkv_step(dirs, st_ref):
    c = WKV_CHUNK
    npair = HC // 2
    lane2 = _lane((2 * c, LANE))
    row2 = _row((2 * c, LANE))
    lo2 = lane2 < NC
    top2 = row2 < c
    bd2 = lo2 == top2
    tmask = row2 - jnp.where(top2, 0, c)
    imask = lane2 - jnp.where(lo2, 0, NC)
    blk16 = jnp.right_shift(tmask, 4) == jnp.right_shift(imask, 4)
    eye = jnp.logical_and(bd2, tmask == imask).astype(F32)
    lo1 = _lane((c, LANE)) < NC
    units = []
    for d, (r, kk, v, lw, a, kx, bwd) in enumerate(dirs):
        cum = _mm_f32(_tri(c, bwd).astype(F32), lw)
        last = cum[0:1] if bwd else cum[c - 1:c]
        e_in, e_out = jnp.exp(cum - lw), jnp.exp(cum)
        e_neg, e_end = jnp.exp(-cum), jnp.exp(last - cum)
        at, rt = -kk * e_in, r * e_out
        bb = kk * a
        bt, kt = bb * e_neg, kx * e_neg
        bh, kh = bb * e_end, kx * e_end
        wc = jnp.exp(last)
        strict = (imask > tmask) if bwd else (imask < tmask)
        incl = ((imask >= tmask) if bwd else (imask <= tmask))[:c]
        for p in range(npair):
            sl = slice(p * LANE, (p + 1) * LANE)
            units.append(dict(
                d=d, p=p, strict=strict, incl=incl, v=v[:, sl], wc=wc[:, sl],
                l2=jnp.concatenate([at[:, sl], rt[:, sl]], axis=0).astype(BF16),
                bk=jnp.concatenate([bt[:, sl], kt[:, sl]], axis=0).astype(BF16),
                kb=jnp.concatenate([kt[:, sl], bt[:, sl]], axis=0).astype(BF16),
                hat=jnp.concatenate([bh[:, sl], kh[:, sl]], axis=0).astype(BF16)))
    zero = jnp.zeros((2 * c, LANE), BF16)
    for u in units:
        u["st"] = st_ref[u["d"], u["p"]]
    g_a = [_mm_nt(jnp.where(lo2, u["l2"], zero), u["bk"]) for u in units]
    g_b = [_mm_nt(jnp.where(lo2, zero, u["l2"]), u["kb"]) for u in units]
    ga = [jnp.where(u["strict"], jnp.concatenate([x[:c], y[:c]], axis=0), 0.0)
          for u, x, y in zip(units, g_a, g_b)]
    gr_a = [jnp.where(u["incl"], x[c:], 0.0).astype(BF16) for u, x in zip(units, g_a)]
    gr_b = [jnp.where(u["incl"], y[c:], 0.0).astype(BF16) for u, y in zip(units, g_b)]
    n_p = [jnp.where(bd2, x, 0.0) for x in ga]
    ak = [jnp.where(bd2, 0.0, x).astype(BF16) for x in ga]
    nd = [jnp.where(blk16, x, 0.0) for x in n_p]
    no = [(x - y).astype(BF16) for x, y in zip(n_p, nd)]
    ndb = [x.astype(BF16) for x in nd]
    n2 = [_mm(x, x).astype(BF16) for x in ndb]
    n4 = [_mm(x, x).astype(BF16) for x in n2]
    n8 = [_mm(x, x).astype(BF16) for x in n4]
    dinv = [eye + x for x in nd]
    dinv = [x + _mm(x, y) for x, y in zip(dinv, n2)]
    dinv = [x + _mm(x, y) for x, y in zip(dinv, n4)]
    dinv = [(x + _mm(x, y)).astype(BF16) for x, y in zip(dinv, n8)]
    m1 = [_mm(x, y) for x, y in zip(dinv, no)]
    m2 = [_mm(x, x) for x in m1]
    t_p = [eye + x for x in m1]
    t_p = [x + _mm(x, y) for x, y in zip(t_p, m2)]
    t_p = [_mm(x, y).astype(BF16) for x, y in zip(t_p, dinv)]
    akv = [_mm(x, jnp.concatenate([u["v"], u["v"]], axis=0)) for u, x in zip(units, ak)]
    base = [_mm_nt(u["l2"], u["st"]) for u in units]
    us = [_mm(t, jnp.concatenate([b[:c], b[:c]], axis=0) + x) for t, b, x in zip(t_p, base, akv)]
    uu = [jnp.where(lo1, x[:c], x[c:]) for x in us]
    uv = [jnp.concatenate([x, u["v"]], axis=0).astype(BF16) for u, x in zip(units, uu)]
    vu = [jnp.concatenate([u["v"], x], axis=0).astype(BF16) for u, x in zip(units, uu)]
    ys = [b[c:] + jnp.where(lo1, _mm(x, p), _mm(y, q))
          for b, x, y, p, q in zip(base, gr_a, gr_b, uv, vu)]
    upd = [_mm_tn(x, u["hat"]) for u, x in zip(units, uv)]
    for u, x in zip(units, upd):
        st_ref[u["d"], u["p"]] = u["st"] * u["wc"] + jnp.where(bd2, x, 0.0)
    return [jnp.concatenate(ys[d * npair:(d + 1) * npair], axis=1) for d in range(2)]


def _wkv_kernel(rf, kkf, vf, kf, wf, rb, kkb, vb, kb, wb, w0_ref, w2_ref, a0_ref, a2_ref, ka_ref,
                s0_ref, yf_o, yb_o, sfin_o, st_ref):
    j = pl.program_id(1)

    @pl.when(j == 0)
    def _():
        st_ref[...] = s0_ref[0]

    dirs = []
    for i, (r, kk, v, k, wdad) in enumerate(((rf, kkf, vf, kf, wf), (rb, kkb, vb, kb, wb))):
        wd, ad = wdad[0][:, 0:LANE], wdad[0][:, LANE:2 * LANE]
        lw = -WKV_DECAY_SCALE * jax.nn.sigmoid(w0_ref[i:i + 1] + _mm(jnp.tanh(wd), w2_ref[i]))
        a = jax.nn.sigmoid(a0_ref[i:i + 1] + _mm(ad, a2_ref[i]))
        kx = k[0] * (1.0 + (a - 1.0) * ka_ref[...])
        dirs.append((r[0], kk[0], v[0], lw, a, kx, i == 1))
    yf, yb = _wkv_step(dirs, st_ref)
    yf_o[0] = yf
    yb_o[0] = yb

    @pl.when(j == pl.num_programs(1) - 1)
    def _():
        sfin_o[0] = st_ref[...]


def _wkv(r, kk, v, k, wdad, p, s0):
    b, l, w = r.shape
    c = WKV_CHUNK
    nc = l // c
    fwd = pl.BlockSpec((1, c, w), lambda bi, j: (bi, j, 0))
    bwd = pl.BlockSpec((1, c, w), lambda bi, j: (bi, nc - 1 - j, 0))
    fwd2 = pl.BlockSpec((1, c, 2 * LANE), lambda bi, j: (bi, j, 0))
    bwd2 = pl.BlockSpec((1, c, 2 * LANE), lambda bi, j: (bi, nc - 1 - j, 0))
    st_spec = pl.BlockSpec((1,) + s0.shape[1:], lambda bi, j: (bi, 0, 0, 0, 0))
    consts = [p["w0"], p["w2"], p["a0"], p["a2"], p["k_a"]]
    return pl.pallas_call(
        _wkv_kernel,
        grid=(b, nc),
        in_specs=[fwd] * 4 + [fwd2] + [bwd] * 4 + [bwd2] + [_const_spec(x.shape) for x in consts]
        + [st_spec],
        out_specs=[fwd, bwd, st_spec],
        out_shape=[jax.ShapeDtypeStruct((b, l, w), F32), jax.ShapeDtypeStruct((b, l, w), F32),
                   jax.ShapeDtypeStruct(s0.shape, F32)],
        scratch_shapes=[pltpu.VMEM(s0.shape[1:], F32)],
        compiler_params=_cparams(("parallel", "arbitrary")),
        name="wkv",
    )(r, kk, v, k, wdad, r, kk, v, k, wdad, *consts, s0)


def _ssd_step(dirs, arow, st_ref):
    c = SSD_CHUNK
    npair = HD // 2
    lane = _lane((c, LANE))
    lo = lane < PD
    grp_rows = _row((c, LANE)) < NS
    zero = jnp.zeros((c, LANE), F32)
    units = []
    for d, (xs, bc, dt, bwd) in enumerate(dirs):
        o = HD * d
        acum = _mm_f32(_tri(c, bwd).astype(F32), dt * arow)
        acum_t = acum.T
        last = acum[0:1] if bwd else acum[c - 1:c]
        ea, de, cd = jnp.exp(acum), jnp.exp(last - acum), jnp.exp(last)
        bfull, cfull = bc[:, 0:LANE], bc[:, LANE:2 * LANE]
        causal = _tri(c, bwd)
        cb = [_mm_nt(jnp.where(lo, cfull, zero), bfull), _mm_nt(jnp.where(lo, zero, cfull), bfull)]

        def colsel(m, ha, hb, o=o):
            return jnp.where(lo, jnp.broadcast_to(m[:, o + ha:o + ha + 1], (c, LANE)),
                             jnp.broadcast_to(m[:, o + hb:o + hb + 1], (c, LANE)))

        for p in range(npair):
            ha, hb = 2 * p, 2 * p + 1
            decs = []
            for h in (ha, hb):
                seg = (jnp.broadcast_to(acum[:, o + h:o + h + 1], (c, c))
                       - jnp.broadcast_to(acum_t[o + h:o + h + 1, :], (c, c)))
                decs.append(jnp.where(causal, jnp.exp(jnp.where(causal, seg, 0.0)), 0.0))
            cdp = jnp.where(lo[0:1], jnp.broadcast_to(cd[:, o + ha:o + ha + 1], (1, LANE)),
                            jnp.broadcast_to(cd[:, o + hb:o + hb + 1], (1, LANE)))
            xdt = xs[:, p * LANE:(p + 1) * LANE] * colsel(dt, ha, hb)
            units.append(dict(
                d=d, p=p, g0=(p // 2 == 0), cdp=cdp, bfull=bfull.astype(BF16),
                cfull=cfull.astype(BF16), xdt=xdt.astype(BF16),
                xde=(xdt * colsel(de, ha, hb)).astype(BF16), eap=colsel(ea, ha, hb),
                sc_a=(cb[p // 2] * decs[0]).astype(BF16), sc_b=(cb[p // 2] * decs[1]).astype(BF16)))
    for u in units:
        u["st"] = st_ref[u["d"], u["p"]]
    yi_a = [_mm(u["sc_a"], u["xdt"]) for u in units]
    yi_b = [_mm(u["sc_b"], u["xdt"]) for u in units]
    yo = [_mm(u["cfull"], u["st"]) for u in units]
    cs = [_mm_tn(u["bfull"], u["xde"]) for u in units]
    ys = [jnp.where(lo, a, b) + o_ * u["eap"] for u, a, b, o_ in zip(units, yi_a, yi_b, yo)]
    for u, x in zip(units, cs):
        keep = grp_rows if u["g0"] else jnp.logical_not(grp_rows)
        st_ref[u["d"], u["p"]] = u["st"] * u["cdp"] + jnp.where(keep, x, 0.0)
    return [jnp.concatenate(ys[d * npair:(d + 1) * npair], axis=1) for d in range(2)]


def _ssd_kernel(xf, bcf, dtf, xb, bcb, dtb, arow_ref, s0_ref, yf_o, yb_o, sfin_o, st_ref):
    j = pl.program_id(1)

    @pl.when(j == 0)
    def _():
        st_ref[...] = s0_ref[0]

    yf, yb = _ssd_step([(xf[0], bcf[0], dtf[0], False), (xb[0], bcb[0], dtb[0], True)],
                       arow_ref[...], st_ref)
    yf_o[0] = yf
    yb_o[0] = yb

    @pl.when(j == pl.num_programs(1) - 1)
    def _():
        sfin_o[0] = st_ref[...]


def _ssd(xs, bc, dt, arow, s0):
    b, l, _ = xs.shape
    c = SSD_CHUNK
    nc = l // c

    def spec(w, rev):
        if rev:
            return pl.BlockSpec((1, c, w), lambda bi, j: (bi, nc - 1 - j, 0))
        return pl.BlockSpec((1, c, w), lambda bi, j: (bi, j, 0))

    st_spec = pl.BlockSpec((1,) + s0.shape[1:], lambda bi, j: (bi, 0, 0, 0, 0))
    return pl.pallas_call(
        _ssd_kernel,
        grid=(b, nc),
        in_specs=[spec(512, False), spec(256, False), spec(128, False),
                  spec(512, True), spec(256, True), spec(128, True),
                  _const_spec(arow.shape), st_spec],
        out_specs=[spec(512, False), spec(512, True), st_spec],
        out_shape=[jax.ShapeDtypeStruct((b, l, 512), F32), jax.ShapeDtypeStruct((b, l, 512), F32),
                   jax.ShapeDtypeStruct(s0.shape, F32)],
        scratch_shapes=[pltpu.VMEM(s0.shape[1:], F32)],
        compiler_params=_cparams(("parallel", "arbitrary")),
        name="ssd",
    )(xs, bc, dt, xs, bc, dt, arow, s0)


def _l1_post_kernel(h_ref, mod_ref, yf, yb, r, k, v, g, ydf, ydb, xs, zg, lnw, lnb, rk, dsum, gn,
                    w_ref, bd_ref, o_ref):
    bd = bd_ref[...]
    y = yf[0] + yb[0]
    mu = _mm_split(y, bd) * (1.0 / NC)
    dl = y - mu
    var = _mm_split(dl * dl, bd) * (1.0 / NC)
    y = dl * lax.rsqrt(var + WKV_LN_EPS) * lnw[...] + lnb[...]
    y = y + _mm_split(r[0] * k[0] * rk[...], bd) * v[0]
    oc = y * g[0]
    x = xs[0]
    yd = (ydf[0] + dsum[0:1] * x) + (ydb[0] + dsum[1:2] * x)
    od = _rms(yd * _silu(zg[0]), gn[...], D_INNER)
    m = _mm(oc, w_ref[0:512, :]) + _mm(od, w_ref[512:1024, :])
    o_ref[0] = h_ref[0] + mod_ref[0][2:3] * m


def _l1_post(h, mod, per_batch, acts, consts, tm):
    b, l, _ = h.shape
    mod_map = (lambda bi, i: (bi, 0, 0)) if per_batch else (lambda bi, i: (0, 0, 0))
    x_spec = pl.BlockSpec((1, tm, D_MODEL), lambda bi, i: (bi, i, 0))
    a_spec = pl.BlockSpec((1, tm, 512), lambda bi, i: (bi, i, 0))
    return pl.pallas_call(
        _l1_post_kernel,
        grid=(b, l // tm),
        in_specs=[x_spec, pl.BlockSpec((1, 6, D_MODEL), mod_map)] + [a_spec] * len(acts)
        + [_const_spec(c.shape) for c in consts],
        out_specs=x_spec,
        out_shape=jax.ShapeDtypeStruct(h.shape, F32),
        compiler_params=_cparams(("parallel", "parallel")),
        name="l1_post",
    )(h, mod, *acts, *consts)


def _rope_tables(length, dim, offset, width):
    quarter = dim // 4
    inv = np.float32(ROPE_BASE) ** (-np.arange(quarter, dtype=np.float32) / np.float32(quarter))
    pos = np.arange(length)
    row = (pos // GRID_W).astype(np.float32)
    col = (pos % GRID_W).astype(np.float32)
    ar, ac = row[:, None] * inv, col[:, None] * inv
    c = np.concatenate([np.cos(ar), np.cos(ar), np.cos(ac), np.cos(ac)], axis=-1)
    s = np.concatenate([-np.sin(ar), np.sin(ar), -np.sin(ac), np.sin(ac)], axis=-1)
    cw = np.ones((length, width), np.float32)
    sw = np.zeros((length, width), np.float32)
    cw[:, offset:offset + dim] = c
    sw[:, offset:offset + dim] = s
    return cw, sw


def _pad_cols(w, n):
    return jnp.pad(w, ((0, 0), (0, n - w.shape[1])))


def _prep_l0(w_in, q_norm, w_uq, kv_norm, w_ukv, w_out):
    a_cols = 3 * HA * 2 * DA
    wqd = _pad_cols(w_in[:, a_cols:a_cols + Q_LORA], 256)
    wckv = w_in[:, a_cols + Q_LORA:a_cols + Q_LORA + KV_LORA]
    wkpe = jnp.pad(w_in[:, a_cols + Q_LORA + KV_LORA:], ((0, 0), (NOPE, LANE - NOPE - ROPE_B)))
    w = jnp.concatenate([w_in[:, :a_cols], wqd, wckv, wkpe], axis=1).astype(BF16)
    qn = _pad_cols(q_norm.reshape(1, -1), 256)
    wuq = jnp.pad(w_uq.reshape(Q_LORA, HB, NOPE + ROPE_B),
                  ((0, 256 - Q_LORA), (0, 0), (0, LANE - NOPE - ROPE_B))).reshape(256, HB * LANE)
    wukv = w_ukv.reshape(KV_LORA, HB, NOPE + VB)
    wk = jnp.pad(wukv[:, :, :NOPE], ((0, 0), (0, 0), (0, LANE - NOPE))).reshape(KV_LORA, HB * LANE)
    wv = wukv[:, :, NOPE:].reshape(KV_LORA, HB * VB)
    return dict(w=w, qn=qn, wuq=wuq.astype(BF16), kvn=kv_norm.reshape(1, -1),
                wukv=jnp.concatenate([wk, wv], axis=1).astype(BF16), w_out=w_out.astype(BF16))


def _prep_l1(w_in, mu, w0, w2, a0, a2, g2, k_k, k_a, r_k, ln_w, ln_b, conv_w, conv_b, A_log,
             dt_bias, D, gnorm, w_out, norm1):
    c_cols = 3 * HC * NC + 4 * 64 + 128
    conv_ch = D_INNER + 2 * G_SSM * NS
    wdt = w_in[:, c_cols + D_INNER + conv_ch:]
    wdt = _pad_cols(jnp.concatenate([wdt, wdt], axis=1), LANE)
    w = jnp.concatenate([w_in[:, :c_cols + D_INNER + conv_ch], wdt], axis=1).astype(BF16)
    z = jnp.zeros((64, HC * NC), F32)
    w2p = jnp.stack([jnp.concatenate([w2[0], z], 0), jnp.concatenate([z, w2[1]], 0)]).astype(BF16)
    a2p = jnp.stack([jnp.concatenate([a2[0], z], 0), jnp.concatenate([z, a2[1]], 0)]).astype(BF16)
    hid = jnp.arange(HC * NC) // NC
    bd = (hid[:, None] == hid[None, :]).astype(BF16)
    dtb = _pad_cols(dt_bias.reshape(1, 2 * HD), LANE)
    arow = _pad_cols((-jnp.exp(A_log.astype(F32))).reshape(1, 2 * HD), LANE)
    return dict(norm1=norm1.reshape(1, -1), w_in=w, mu=mu.reshape(1, -1), k_k=k_k.reshape(1, -1),
                k_a=k_a.reshape(1, -1), w0=w0, w2=w2p, a0=a0, a2=a2p, g2=g2.astype(BF16),
                conv_w=conv_w, conv_b=conv_b.reshape(1, -1), dt_bias=dtb, bd=bd, arow=arow,
                ln_w=ln_w.reshape(1, -1), ln_b=ln_b.reshape(1, -1), r_k=r_k.reshape(1, -1),
                dsum=jnp.repeat(D, PD, axis=1), gnorm=gnorm.reshape(1, -1),
                w_out=w_out.astype(BF16))


def _wkv_state_in(s):
    b = s.shape[0]
    st = s.reshape(b, HC // 2, 2, NC, NC)
    z = jnp.zeros_like(st[:, :, 0])
    top = jnp.concatenate([st[:, :, 0], z], axis=-1)
    bot = jnp.concatenate([z, st[:, :, 1]], axis=-1)
    return jnp.concatenate([top, bot], axis=-2)


def _wkv_state_out(s):
    a = s[:, :, :NC, :NC]
    b_ = s[:, :, NC:, NC:]
    return jnp.stack([a, b_], axis=2).reshape(s.shape[0], HC, NC, NC)


def _ssd_state_in(s):
    b = s.shape[0]
    st = jnp.swapaxes(s, -1, -2).reshape(b, HD // 2, 2, NS, PD)
    pair = jnp.concatenate([st[:, :, 0], st[:, :, 1]], axis=-1)
    z = jnp.zeros_like(pair)
    g0 = jnp.concatenate([pair, z], axis=-2)
    g1 = jnp.concatenate([z, pair], axis=-2)
    is_g0 = (jnp.arange(HD // 2) < HD // 4)[None, :, None, None]
    return jnp.where(is_g0, g0, g1)


def _ssd_state_out(s):
    is_g0 = (jnp.arange(HD // 2) < HD // 4)[None, :, None, None]
    pair = jnp.where(is_g0, s[:, :, :NS, :], s[:, :, NS:, :])
    st = jnp.stack([pair[..., :PD], pair[..., PD:]], axis=2).reshape(s.shape[0], HD, NS, PD)
    return jnp.swapaxes(st, -1, -2)


def _layer0(h, mod, per_batch, p, norm1, ctx, tables, lam_vecs, subln, lambda_init, tm, tq):
    qa, ka, va, qb, ckv, kpe = _l0_proj(h, mod, per_batch, norm1, p["w"], p["qn"], p["wuq"],
                                        p["kvn"], tables, tm)
    own = (ka, va, ckv, kpe)
    if ctx is not None:
        ck, cv, cckv, ckpe = ctx
        ka = jnp.concatenate([ck.astype(BF16), ka], axis=2)
        va = jnp.concatenate([cv.astype(BF16), va], axis=2)
        ckv = jnp.concatenate([cckv, ckv], axis=1)
        kpe = jnp.concatenate([jnp.pad(ckpe, ((0, 0), (0, 0), (NOPE, LANE - NOPE - ROPE_B))), kpe],
                              axis=1)
    kb, vb = _kvup(ckv, kpe, p["wukv"], min(512, ckv.shape[1]))
    nsub = tq // 128
    oa = _attention(qa, ka, va, tq, nsub, True, lam_vecs, subln, lambda_init)
    tqb = min(2 * tq, qb.shape[2])
    ob = _attention(qb, kb, vb, tqb, tqb // 128, False)
    return _oproj(h, mod, per_batch, oa, ob, p["w_out"], min(2 * tm, h.shape[1])), own


def _layer1(h, mod, per_batch, p, wkv0, ssm0, tm):
    r, k, v, kk, g, wdad, zg, xs, bc, dt = _l1_proj(h, mod, per_batch, p, tm)
    yf, yb, s_wkv = _wkv(r, kk, v, k, wdad, p, wkv0)
    ydf, ydb, s_ssm = _ssd(xs, bc, dt, p["arow"], ssm0)
    consts = [p["ln_w"], p["ln_b"], p["r_k"], p["dsum"], p["gnorm"], p["w_out"], p["bd"]]
    h = _l1_post(h, mod, per_batch, [yf, yb, r, k, v, g, ydf, ydb, xs, zg], consts,
                 min(2 * tm, h.shape[1]))
    return h, s_wkv, s_ssm


def kernel(x_prompt, x_sample, cache_l0_k, cache_l0_v, cache_l0_ckv, cache_l0_kpe, state_l1_wkv_fwd, state_l1_wkv_bwd, state_l1_ssm_fwd, state_l1_ssm_bwd, c, c_ctx, ada_w_0, ada_b_0, norm1_0, norm2_0, ffn_up_0, ffn_conv_w_0, ffn_conv_b_0, ffn_down_0, l0_w_in, l0_lambda_q1, l0_lambda_k1, l0_lambda_q2, l0_lambda_k2, l0_subln, l0_q_norm, l0_w_uq, l0_kv_norm, l0_w_ukv, l0_w_out, ada_w_1, ada_b_1, norm1_1, norm2_1, ffn_up_1, ffn_conv_w_1, ffn_conv_b_1, ffn_down_1, l1_w_in, l1_mu, l1_w0, l1_w2, l1_a0, l1_a2, l1_g2, l1_k_k, l1_k_a, l1_r_k, l1_ln_w, l1_ln_b, l1_conv_w, l1_conv_b, l1_A_log, l1_dt_bias, l1_D, l1_gnorm, l1_w_out, norm_f):
    bc_, lc, _ = x_prompt.shape
    bl, ll, _ = x_sample.shape
    tm_c, tm_l = min(256, lc), min(256, ll)
    cond = jnp.concatenate([c_ctx[None, :], c], axis=0)
    cond8 = jnp.pad(cond, ((0, 8 - cond.shape[0]), (0, 0)))
    mods = [_ada(cond8, w, b).reshape(8, 6, D_MODEL) for w, b in ((ada_w_0, ada_b_0), (ada_w_1, ada_b_1))]
    p0 = _prep_l0(l0_w_in, l0_q_norm, l0_w_uq, l0_kv_norm, l0_w_ukv, l0_w_out)
    tables = _rope_tables(ll, DA, 0, DA)
    tables = tuple(np.concatenate([t, t], axis=1) for t in tables) + _rope_tables(ll, ROPE_B, NOPE, LANE)
    tables = tuple(jnp.asarray(t) for t in tables)
    lam_vecs = jnp.pad(jnp.stack([l0_lambda_q1, l0_lambda_k1, l0_lambda_q2, l0_lambda_k2]),
                       ((0, 4), (0, LANE - DA)))
    subln = l0_subln.reshape(1, -1)
    lambda_init = 0.8 - 0.6 * math.exp(-0.3 * 0)
    n1 = norm1_0.reshape(1, -1)
    h_ctx, own = _layer0(x_prompt, mods[0][0:1], False, p0, n1, None, None, lam_vecs, subln,
                         lambda_init, tm_c, min(256, lc))
    h_lat, _ = _layer0(x_sample, mods[0][1:1 + bl], True, p0, n1,
                       (cache_l0_k, cache_l0_v, cache_l0_ckv, cache_l0_kpe), tables, lam_vecs,
                       subln, lambda_init, tm_l, min(256, ll))
    ffn0 = (norm2_0.reshape(1, -1), ffn_up_0.astype(BF16), ffn_conv_w_0, ffn_conv_b_0.reshape(1, -1),
            ffn_down_0.astype(BF16))
    h_ctx = _ffn(h_ctx, mods[0][0:1], False, *ffn0, None, tm_c)
    h_lat = _ffn(h_lat, mods[0][1:1 + bl], True, *ffn0, None, min(512, ll))
    p1 = _prep_l1(l1_w_in, l1_mu, l1_w0, l1_w2, l1_a0, l1_a2, l1_g2, l1_k_k, l1_k_a, l1_r_k,
                  l1_ln_w, l1_ln_b, l1_conv_w, l1_conv_b, l1_A_log, l1_dt_bias, l1_D, l1_gnorm,
                  l1_w_out, norm1_1)
    zero_state = jnp.zeros((bc_, 2, HC // 2, LANE, LANE), F32)
    h_ctx, s_wkv, s_ssm = _layer1(h_ctx, mods[1][0:1], False, p1, zero_state, zero_state, tm_c)
    wkv0 = jnp.stack([_wkv_state_in(state_l1_wkv_fwd), _wkv_state_in(state_l1_wkv_bwd)], axis=1)
    ssm0 = jnp.stack([_ssd_state_in(state_l1_ssm_fwd), _ssd_state_in(state_l1_ssm_bwd)], axis=1)
    h_lat, _, _ = _layer1(h_lat, mods[1][1:1 + bl], True, p1, wkv0, ssm0, tm_l)
    ffn1 = (norm2_1.reshape(1, -1), ffn_up_1.astype(BF16), ffn_conv_w_1, ffn_conv_b_1.reshape(1, -1),
            ffn_down_1.astype(BF16))
    nf = norm_f.reshape(1, -1)
    y_prompt = _ffn(h_ctx, mods[1][0:1], False, *ffn1, nf, tm_c)
    y_sample = _ffn(h_lat, mods[1][1:1 + bl], True, *ffn1, nf, min(512, ll))
    ka, va, ckv, kpe = own
    return (y_prompt, y_sample, ka, va, ckv, kpe[:, :, NOPE:NOPE + ROPE_B],
            _wkv_state_out(s_wkv[:, 0]), _wkv_state_out(s_wkv[:, 1]),
            _ssd_state_out(s_ssm[:, 0]), _ssd_state_out(s_ssm[:, 1]))
```

```python
import functools
import math

import jax
import jax.numpy as jnp
import numpy as np
from jax import lax
from jax.experimental import pallas as pl
from jax.experimental.pallas import tpu as pltpu

F32 = jnp.float32
BF16 = jnp.bfloat16
HIGHEST = lax.Precision.HIGHEST

D_MODEL = 1024
GRID_W = 64
ROPE_BASE = 10000.0
NORM_EPS = 1e-6
HA, DA = 4, 64
HB, NOPE, ROPE_B, VB = 4, 64, 32, 128
Q_LORA, KV_LORA = 192, 128
HC, NC = 8, 64
WKV_LN_EPS = 64e-5
WKV_DECAY_SCALE = 0.606531
HD, PD, G_SSM, NS = 8, 64, 2, 64
D_INNER = HD * PD
F_FF = 2816
LANE = 128
HALO = 8
LOG2E = math.log2(math.e)
WKV_CHUNK = 64
SSD_CHUNK = 128
VMEM_LIMIT = 56 * 1024 * 1024


def _cparams(sem):
    return pltpu.CompilerParams(dimension_semantics=sem, vmem_limit_bytes=VMEM_LIMIT)


def _mm(a, b):
    return jnp.dot(a.astype(BF16), b.astype(BF16), preferred_element_type=F32)


def _mm_nt(a, b):
    return lax.dot_general(a.astype(BF16), b.astype(BF16), (((1,), (1,)), ((), ())),
                           preferred_element_type=F32)


def _mm_tn(a, b):
    return lax.dot_general(a.astype(BF16), b.astype(BF16), (((0,), (0,)), ((), ())),
                           preferred_element_type=F32)


def _mm_f32(a, b):
    return jnp.dot(a, b, precision=HIGHEST, preferred_element_type=F32)


def _mm_split(x, m):
    hi = x.astype(BF16)
    lo = (x - hi.astype(F32)).astype(BF16)
    return (jnp.dot(hi, m, preferred_element_type=F32) + jnp.dot(lo, m, preferred_element_type=F32))


def _rms(x, g, n):
    ms = jnp.sum(x * x, axis=-1, keepdims=True) * (1.0 / n)
    return x * lax.rsqrt(ms + NORM_EPS) * g


def _silu(x):
    return x * jax.nn.sigmoid(x)


def _lane(shape):
    return lax.broadcasted_iota(jnp.int32, shape, len(shape) - 1)


def _row(shape):
    return lax.broadcasted_iota(jnp.int32, shape, len(shape) - 2)


def _const_spec(shape, single=False):
    nd = len(shape)
    if single:
        return pl.BlockSpec(shape, lambda *_: (0,) * nd, pipeline_mode=pl.Buffered(1))
    return pl.BlockSpec(shape, lambda *_: (0,) * nd)


def _ada_kernel(c_ref, w_ref, b_ref, o_ref):
    o_ref[...] = _mm(_silu(c_ref[...]), w_ref[...]) + b_ref[...]


def _ada(cond8, w, b):
    n = w.shape[1]
    tn = 1536
    return pl.pallas_call(
        _ada_kernel,
        grid=(n // tn,),
        in_specs=[_const_spec((8, D_MODEL)),
                  pl.BlockSpec((D_MODEL, tn), lambda j: (0, j)),
                  pl.BlockSpec((1, tn), lambda j: (0, j))],
        out_specs=pl.BlockSpec((8, tn), lambda j: (0, j)),
        out_shape=jax.ShapeDtypeStruct((8, n), F32),
        compiler_params=_cparams(("arbitrary",)),
        name="ada",
    )(cond8, w, b.reshape(1, n))


def _rope(x, c, s, half):
    w = x.shape[-1]
    up = pltpu.roll(x, w - half, 1)
    dn = pltpu.roll(x, half, 1)
    first = (_lane(x.shape) & (2 * half - 1)) < half
    return x * c + jnp.where(first, up, dn) * s


def _l0_proj_kernel(*refs, latent):
    if latent:
        (x_ref, mod_ref, n1_ref, w_ref, qn_ref, wuq_ref, kvn_ref, ca_ref, sa_ref, cb_ref, sb_ref,
         qa_o, ka_o, va_o, qb_o, ckv_o, kpe_o) = refs
    else:
        (x_ref, mod_ref, n1_ref, w_ref, qn_ref, wuq_ref, kvn_ref,
         qa_o, ka_o, va_o, qb_o, ckv_o, kpe_o) = refs
    mod = mod_ref[0]
    hn = _rms(x_ref[0], n1_ref[...], D_MODEL) * (1.0 + mod[1:2]) + mod[0:1]
    proj = _mm(hn, w_ref[...])
    qd = _rms(proj[:, 1536:1792], qn_ref[...], Q_LORA)
    qb = _mm(qd, wuq_ref[...])
    ckv_o[0] = _rms(proj[:, 1792:1920], kvn_ref[...], KV_LORA)
    kpe = proj[:, 1920:2048]
    if latent:
        ca, sa, cb, sb = ca_ref[...], sa_ref[...], cb_ref[...], sb_ref[...]
        kpe = _rope(kpe, cb, sb, ROPE_B // 4)
    kpe_o[0] = kpe
    for h in range(HA):
        sl = slice(h * LANE, (h + 1) * LANE)
        q = proj[:, sl]
        k = proj[:, 512 + h * LANE:512 + (h + 1) * LANE]
        qbh = qb[:, sl]
        if latent:
            q = _rope(q, ca, sa, DA // 4)
            k = _rope(k, ca, sa, DA // 4)
            qbh = _rope(qbh, cb, sb, ROPE_B // 4)
        qa_o[0, h] = (q * (DA ** -0.5 * LOG2E)).astype(qa_o.dtype)
        ka_o[0, h] = k.astype(ka_o.dtype)
        va_o[0, h] = proj[:, 1024 + h * LANE:1024 + (h + 1) * LANE].astype(va_o.dtype)
        qb_o[0, h] = (qbh * ((NOPE + ROPE_B) ** -0.5 * LOG2E)).astype(qb_o.dtype)


def _l0_proj(x, mod, per_batch, n1, w, qn, wuq, kvn, tables, tm):
    b, l, _ = x.shape
    latent = tables is not None
    kv_dtype = BF16 if latent else F32
    mod_map = (lambda bi, i: (bi, 0, 0)) if per_batch else (lambda bi, i: (0, 0, 0))
    in_specs = [pl.BlockSpec((1, tm, D_MODEL), lambda bi, i: (bi, i, 0)),
                pl.BlockSpec((1, 6, D_MODEL), mod_map),
                _const_spec((1, D_MODEL)), _const_spec(w.shape), _const_spec(qn.shape),
                _const_spec(wuq.shape), _const_spec(kvn.shape)]
    args = [x, mod, n1, w, qn, wuq, kvn]
    if latent:
        in_specs += [pl.BlockSpec((tm, LANE), lambda bi, i: (i, 0))] * 4
        args += list(tables)
    head_spec = pl.BlockSpec((1, HA, tm, LANE), lambda bi, i: (bi, 0, i, 0))
    row_spec = pl.BlockSpec((1, tm, LANE), lambda bi, i: (bi, i, 0))
    return pl.pallas_call(
        functools.partial(_l0_proj_kernel, latent=latent),
        grid=(b, l // tm),
        in_specs=in_specs,
        out_specs=[head_spec, head_spec, head_spec, head_spec, row_spec, row_spec],
        out_shape=[jax.ShapeDtypeStruct((b, HA, l, LANE), BF16),
                   jax.ShapeDtypeStruct((b, HA, l, LANE), kv_dtype),
                   jax.ShapeDtypeStruct((b, HA, l, LANE), kv_dtype),
                   jax.ShapeDtypeStruct((b, HB, l, LANE), BF16),
                   jax.ShapeDtypeStruct((b, l, LANE), F32),
                   jax.ShapeDtypeStruct((b, l, LANE), F32)],
        compiler_params=_cparams(("parallel", "parallel")),
        name="l0_proj",
    )(*args)


def _kvup_kernel(ckv_ref, kpe_ref, w_ref, kb_o, vb_o):
    kv = _mm(ckv_ref[0], w_ref[...])
    kpe = kpe_ref[0]
    for h in range(HB):
        kb_o[0, h] = (kv[:, h * LANE:(h + 1) * LANE] + kpe).astype(BF16)
        vb_o[0, h] = kv[:, 512 + h * LANE:512 + (h + 1) * LANE].astype(BF16)


def _kvup(ckv, kpe, w, tk):
    b, lk, _ = ckv.shape
    row_spec = pl.BlockSpec((1, tk, LANE), lambda bi, i: (bi, i, 0))
    head_spec = pl.BlockSpec((1, HB, tk, LANE), lambda bi, i: (bi, 0, i, 0))
    return pl.pallas_call(
        _kvup_kernel,
        grid=(b, lk // tk),
        in_specs=[row_spec, row_spec, _const_spec(w.shape)],
        out_specs=[head_spec, head_spec],
        out_shape=[jax.ShapeDtypeStruct((b, HB, lk, LANE), BF16)] * 2,
        compiler_params=_cparams(("parallel", "parallel")),
        name="kvup",
    )(ckv, kpe, w)


def _attn_kernel(*refs, diff, lambda_init, nsub):
    if diff:
        q_ref, k_ref, v_ref, lam_ref, g_ref, o_ref = refs
    else:
        q_ref, k_ref, v_ref, o_ref = refs
    k = k_ref[0, 0]
    v = v_ref[0, 0]
    ts = q_ref.shape[2] // nsub
    qs = []
    for i in range(nsub):
        q = q_ref[0, 0, i * ts:(i + 1) * ts, :]
        if diff:
            lo = _lane(q.shape) < DA
            zero = jnp.zeros_like(q)
            q = jnp.concatenate([jnp.where(lo, q, zero), jnp.where(lo, zero, q)], axis=0)
        qs.append(q)
    s = [_mm_nt(q, k) for q in qs]
    m = [jnp.max(x, axis=-1, keepdims=True) for x in s]
    p = [jnp.exp2(x - y) for x, y in zip(s, m)]
    inv = [1.0 / jnp.sum(x, axis=-1, keepdims=True) for x in p]
    o = [_mm(x, v) * y for x, y in zip(p, inv)]
    if diff:
        lv = lam_ref[...]
        lam = (jnp.exp(jnp.sum(lv[0:1] * lv[1:2], axis=-1, keepdims=True))
               - jnp.exp(jnp.sum(lv[2:3] * lv[3:4], axis=-1, keepdims=True)) + lambda_init)
        o = [_rms(x[:ts] - lam * x[ts:], g_ref[...], 2 * DA) * (1.0 - lambda_init) for x in o]
    for i in range(nsub):
        o_ref[0, i * ts:(i + 1) * ts, :] = o[i].astype(o_ref.dtype)


def _attention(q, k, v, tq, nsub, diff, lam_vecs=None, subln=None, lambda_init=0.0):
    b, h, l, _ = q.shape
    lk = k.shape[2]
    in_specs = [pl.BlockSpec((1, 1, tq, LANE), lambda bi, hi, i: (bi, hi, i, 0)),
                pl.BlockSpec((1, 1, lk, LANE), lambda bi, hi, i: (bi, hi, 0, 0)),
                pl.BlockSpec((1, 1, lk, LANE), lambda bi, hi, i: (bi, hi, 0, 0))]
    args = [q, k, v]
    if diff:
        in_specs += [_const_spec(lam_vecs.shape), _const_spec(subln.shape)]
        args += [lam_vecs, subln]
    return pl.pallas_call(
        functools.partial(_attn_kernel, diff=diff, lambda_init=lambda_init, nsub=nsub),
        grid=(b, h, l // tq),
        in_specs=in_specs,
        out_specs=pl.BlockSpec((1, tq, LANE), lambda bi, hi, i: (bi, i, hi)),
        out_shape=jax.ShapeDtypeStruct((b, l, h * LANE), BF16),
        compiler_params=_cparams(("parallel", "parallel", "parallel")),
        name="attn_diff" if diff else "attn_mla",
    )(*args)


def _oproj_kernel(h_ref, mod_ref, a_ref, b_ref, w_ref, o_ref):
    half = a_ref.shape[-1]
    m = _mm(a_ref[0], w_ref[0:half, :]) + _mm(b_ref[0], w_ref[half:, :])
    o_ref[0] = h_ref[0] + mod_ref[0][2:3] * m


def _oproj(h, mod, per_batch, oa, ob, w, tm):
    b, l, _ = h.shape
    mod_map = (lambda bi, i: (bi, 0, 0)) if per_batch else (lambda bi, i: (0, 0, 0))
    x_spec = pl.BlockSpec((1, tm, D_MODEL), lambda bi, i: (bi, i, 0))
    half_spec = pl.BlockSpec((1, tm, oa.shape[-1]), lambda bi, i: (bi, i, 0))
    return pl.pallas_call(
        _oproj_kernel,
        grid=(b, l // tm),
        in_specs=[x_spec, pl.BlockSpec((1, 6, D_MODEL), mod_map), half_spec, half_spec,
                  _const_spec(w.shape)],
        out_specs=x_spec,
        out_shape=jax.ShapeDtypeStruct(h.shape, F32),
        compiler_params=_cparams(("parallel", "parallel")),
        name="oproj",
    )(h, mod, oa, ob, w)


def _halo_rows(x_ref, xp_ref, xn_ref):
    i = pl.program_id(1)
    last = pl.num_programs(1) - 1
    xc = jnp.concatenate([xp_ref[0], x_ref[0], xn_ref[0]], axis=0)
    tm = x_ref.shape[1]
    r = _row((tm + 2 * HALO, 1))
    valid = jnp.logical_and(jnp.logical_or(r >= HALO, i > 0),
                            jnp.logical_or(r < tm + HALO, i < last))
    return xc, valid


def _shift_rows(u, tm):
    n = u.shape[0]
    up = pltpu.roll(u, 1, 0)[HALO:HALO + tm]
    dn = pltpu.roll(u, n - 1, 0)[HALO:HALO + tm]
    return up, u[HALO:HALO + tm], dn


def _ffn_kernel(x_ref, xp_ref, xn_ref, mod_ref, n2_ref, wu_ref, cw_ref, cb_ref, wd_ref,
                *rest, final, cw):
    if final:
        nf_ref, o_ref = rest
    else:
        (o_ref,) = rest
    tm = x_ref.shape[1]
    mod = mod_ref[0]
    xc, valid = _halo_rows(x_ref, xp_ref, xn_ref)
    hn = _rms(xc, n2_ref[...], D_MODEL) * (1.0 + mod[4:5]) + mod[3:4]
    hn = jnp.where(valid, hn, 0.0).astype(BF16)
    acc = jnp.zeros((tm, D_MODEL), F32)
    for c in range(F_FF // cw):
        halves = []
        for off in (c * cw, F_FF + c * cw):
            u = jnp.dot(hn, wu_ref[:, off:off + cw], preferred_element_type=F32)
            up, mid, dn = _shift_rows(u, tm)
            w3 = cw_ref[:, off:off + cw]
            halves.append(up * w3[0:1] + mid * w3[1:2] + dn * w3[2:3] + cb_ref[:, off:off + cw])
        act = (_silu(halves[0]) * halves[1]).astype(BF16)
        acc = acc + jnp.dot(act, wd_ref[c * cw:(c + 1) * cw, :], preferred_element_type=F32)
    out = x_ref[0] + mod[5:6] * acc
    if final:
        out = _rms(out, nf_ref[...], D_MODEL)
    o_ref[0] = out


def _ffn(h, mod, per_batch, n2, wu, cw3, cb, wd, nf, tm):
    b, l, _ = h.shape
    nblk = l // tm
    nb = tm // HALO
    mod_map = (lambda bi, i: (bi, 0, 0)) if per_batch else (lambda bi, i: (0, 0, 0))
    x_spec = pl.BlockSpec((1, tm, D_MODEL), lambda bi, i: (bi, i, 0))
    in_specs = [x_spec,
                pl.BlockSpec((1, HALO, D_MODEL), lambda bi, i: (bi, jnp.maximum(i * nb - 1, 0), 0)),
                pl.BlockSpec((1, HALO, D_MODEL),
                             lambda bi, i: (bi, jnp.minimum((i + 1) * nb, nblk * nb - 1), 0)),
                pl.BlockSpec((1, 6, D_MODEL), mod_map),
                _const_spec(n2.shape), _const_spec(wu.shape, True), _const_spec(cw3.shape),
                _const_spec(cb.shape), _const_spec(wd.shape, True)]
    args = [h, h, h, mod, n2, wu, cw3, cb, wd]
    final = nf is not None
    if final:
        in_specs.append(_const_spec(nf.shape))
        args.append(nf)
    cw = 2816
    return pl.pallas_call(
        functools.partial(_ffn_kernel, final=final, cw=cw),
        grid=(b, nblk),
        in_specs=in_specs,
        out_specs=x_spec,
        out_shape=jax.ShapeDtypeStruct(h.shape, F32),
        compiler_params=_cparams(("parallel", "parallel")),
        name="ffn",
    )(*args)


def _softplus(x):
    return jnp.maximum(x, 0.0) + jnp.log1p(jnp.exp(-jnp.abs(x)))


def _l1_proj_kernel(x_ref, xp_ref, xn_ref, mod_ref, n1_ref, w_ref, mu_ref, kk_ref,
                    g2_ref, cw_ref, cb_ref, dtb_ref, bd_ref,
                    r_o, k_o, v_o, kk_o, g_o, wdad_o, zg_o, xs_o, bc_o, dt_o):
    tm = x_ref.shape[1]
    mod = mod_ref[0]
    xc, valid = _halo_rows(x_ref, xp_ref, xn_ref)
    hn = _rms(xc, n1_ref[...], D_MODEL) * (1.0 + mod[1:2]) + mod[0:1]
    hn = jnp.where(valid, hn, 0.0).astype(BF16)
    up, mid, dn = _shift_rows(jnp.dot(hn, w_ref[:, 0:1920], preferred_element_type=F32), tm)
    zc = mid + mu_ref[...] * (0.5 * (up + dn) - mid)
    r, k, v = zc[:, 0:512], zc[:, 512:1024], zc[:, 1024:1536]
    gd = zc[:, 1792:1920]
    r_o[0], k_o[0], v_o[0] = r, k, v
    wdad_o[0] = zc[:, 1536:1792]
    kkf = k * kk_ref[...]
    kk_o[0] = kkf * lax.rsqrt(_mm_split(kkf * kkf, bd_ref[...]) + 1e-12)
    g_o[0] = _mm(jax.nn.sigmoid(gd), g2_ref[...])
    zd = jnp.dot(hn, w_ref[:, 1920:3328], preferred_element_type=F32)
    zg_o[0] = zd[HALO:HALO + tm, 0:512]
    up, mid, dn = _shift_rows(zd[:, 512:1280], tm)
    cw3 = cw_ref[...]
    xbc = _silu(up * cw3[0:1] + mid * cw3[1:2] + dn * cw3[2:3] + cb_ref[...])
    xs_o[0] = xbc[:, 0:512]
    bc_o[0] = xbc[:, 512:768]
    dt_o[0] = _softplus(zd[HALO:HALO + tm, 1280:1408] + dtb_ref[...])


def _l1_proj(x, mod, per_batch, p, tm):
    b, l, _ = x.shape
    nblk = l // tm
    nb = tm // HALO
    mod_map = (lambda bi, i: (bi, 0, 0)) if per_batch else (lambda bi, i: (0, 0, 0))
    x_spec = pl.BlockSpec((1, tm, D_MODEL), lambda bi, i: (bi, i, 0))
    consts = [p["norm1"], p["w_in"], p["mu"], p["k_k"], p["g2"], p["conv_w"], p["conv_b"],
              p["dt_bias"], p["bd"]]
    in_specs = [x_spec,
                pl.BlockSpec((1, HALO, D_MODEL), lambda bi, i: (bi, jnp.maximum(i * nb - 1, 0), 0)),
                pl.BlockSpec((1, HALO, D_MODEL),
                             lambda bi, i: (bi, jnp.minimum((i + 1) * nb, nblk * nb - 1), 0)),
                pl.BlockSpec((1, 6, D_MODEL), mod_map)] + [_const_spec(c.shape) for c in consts]

    def o_spec(w):
        return pl.BlockSpec((1, tm, w), lambda bi, i: (bi, i, 0))

    widths = [512] * 5 + [256, 512, 512, 256, 128]
    return pl.pallas_call(
        _l1_proj_kernel,
        grid=(b, nblk),
        in_specs=in_specs,
        out_specs=[o_spec(w) for w in widths],
        out_shape=[jax.ShapeDtypeStruct((b, l, w), F32) for w in widths],
        compiler_params=_cparams(("parallel", "parallel")),
        name="l1_proj",
    )(x, x, x, mod, *consts)


def _tri(n, upper, strict=False):
    r, c = _row((n, n)), _lane((n, n))
    if upper:
        return (c > r) if strict else (c >= r)
    return (c < r) if strict else (c <= r)


def _wkv_step(dirs, st_ref):
    c = WKV_CHUNK
    npair = HC // 2
    lane2 = _lane((2 * c, LANE))
    row2 = _row((2 * c, LANE))
    lo2 = lane2 < NC
    top2 = row2 < c
    bd2 = lo2 == top2
    tmask = row2 - jnp.where(top2, 0, c)
    imask = lane2 - jnp.where(lo2, 0, NC)
    blk16 = jnp.right_shift(tmask, 4) == jnp.right_shift(imask, 4)
    eye = jnp.logical_and(bd2, tmask == imask).astype(F32)
    lo1 = _lane((c, LANE)) < NC
    units = []
    for d, (r, kk, v, lw, a, kx, bwd) in enumerate(dirs):
        cum = _mm_f32(_tri(c, bwd).astype(F32), lw)
        last = cum[0:1] if bwd else cum[c - 1:c]
        e_in, e_out = jnp.exp(cum - lw), jnp.exp(cum)
        e_neg, e_end = jnp.exp(-cum), jnp.exp(last - cum)
        at, rt = -kk * e_in, r * e_out
        bb = kk * a
        bt, kt = bb * e_neg, kx * e_neg
        bh, kh = bb * e_end, kx * e_end
        wc = jnp.exp(last)
        strict = (imask > tmask) if bwd else (imask < tmask)
        incl = ((imask >= tmask) if bwd else (imask <= tmask))[:c]
        for p in range(npair):
            sl = slice(p * LANE, (p + 1) * LANE)
            units.append(dict(
                d=d, p=p, strict=strict, incl=incl, v=v[:, sl], wc=wc[:, sl],
                l2=jnp.concatenate([at[:, sl], rt[:, sl]], axis=0).astype(BF16),
                bk=jnp.concatenate([bt[:, sl], kt[:, sl]], axis=0).astype(BF16),
                kb=jnp.concatenate([kt[:, sl], bt[:, sl]], axis=0).astype(BF16),
                hat=jnp.concatenate([bh[:, sl], kh[:, sl]], axis=0).astype(BF16)))
    zero = jnp.zeros((2 * c, LANE), BF16)
    for u in units:
        u["st"] = st_ref[u["d"], u["p"]]
    g_a = [_mm_nt(jnp.where(lo2, u["l2"], zero), u["bk"]) for u in units]
    g_b = [_mm_nt(jnp.where(lo2, zero, u["l2"]), u["kb"]) for u in units]
    ga = [jnp.where(u["strict"], jnp.concatenate([x[:c], y[:c]], axis=0), 0.0)
          for u, x, y in zip(units, g_a, g_b)]
    gr_a = [jnp.where(u["incl"], x[c:], 0.0).astype(BF16) for u, x in zip(units, g_a)]
    gr_b = [jnp.where(u["incl"], y[c:], 0.0).astype(BF16) for u, y in zip(units, g_b)]
    n_p = [jnp.where(bd2, x, 0.0) for x in ga]
    ak = [jnp.where(bd2, 0.0, x).astype(BF16) for x in ga]
    nd = [jnp.where(blk16, x, 0.0) for x in n_p]
    no = [(x - y).astype(BF16) for x, y in zip(n_p, nd)]
    ndb = [x.astype(BF16) for x in nd]
    n2 = [_mm(x, x).astype(BF16) for x in ndb]
    n4 = [_mm(x, x).astype(BF16) for x in n2]
    n8 = [_mm(x, x).astype(BF16) for x in n4]
    dinv = [eye + x for x in nd]
    dinv = [x + _mm(x, y) for x, y in zip(dinv, n2)]
    dinv = [x + _mm(x, y) for x, y in zip(dinv, n4)]
    dinv = [(x + _mm(x, y)).astype(BF16) for x, y in zip(dinv, n8)]
    m1 = [_mm(x, y) for x, y in zip(dinv, no)]
    m2 = [_mm(x, x) for x in m1]
    t_p = [eye + x for x in m1]
    t_p = [x + _mm(x, y) for x, y in zip(t_p, m2)]
    t_p = [_mm(x, y).astype(BF16) for x, y in zip(t_p, dinv)]
    akv = [_mm(x, jnp.concatenate([u["v"], u["v"]], axis=0)) for u, x in zip(units, ak)]
    base = [_mm_nt(u["l2"], u["st"]) for u in units]
    us = [_mm(t, jnp.concatenate([b[:c], b[:c]], axis=0) + x) for t, b, x in zip(t_p, base, akv)]
    uu = [jnp.where(lo1, x[:c], x[c:]) for x in us]
    uv = [jnp.concatenate([x, u["v"]], axis=0).astype(BF16) for u, x in zip(units, uu)]
    vu = [jnp.concatenate([u["v"], x], axis=0).astype(BF16) for u, x in zip(units, uu)]
    ys = [b[c:] + jnp.where(lo1, _mm(x, p), _mm(y, q))
          for b, x, y, p, q in zip(base, gr_a, gr_b, uv, vu)]
    upd = [_mm_tn(x, u["hat"]) for u, x in zip(units, uv)]
    for u, x in zip(units, upd):
        st_ref[u["d"], u["p"]] = u["st"] * u["wc"] + jnp.where(bd2, x, 0.0)
    return [jnp.concatenate(ys[d * npair:(d + 1) * npair], axis=1) for d in range(2)]


def _wkv_kernel(rf, kkf, vf, kf, wf, rb, kkb, vb, kb, wb, w0_ref, w2_ref, a0_ref, a2_ref, ka_ref,
                s0_ref, yf_o, yb_o, sfin_o, st_ref):
    j = pl.program_id(1)

    @pl.when(j == 0)
    def _():
        st_ref[...] = s0_ref[0]

    dirs = []
    for i, (r, kk, v, k, wdad) in enumerate(((rf, kkf, vf, kf, wf), (rb, kkb, vb, kb, wb))):
        wd, ad = wdad[0][:, 0:LANE], wdad[0][:, LANE:2 * LANE]
        lw = -WKV_DECAY_SCALE * jax.nn.sigmoid(w0_ref[i:i + 1] + _mm(jnp.tanh(wd), w2_ref[i]))
        a = jax.nn.sigmoid(a0_ref[i:i + 1] + _mm(ad, a2_ref[i]))
        kx = k[0] * (1.0 + (a - 1.0) * ka_ref[...])
        dirs.append((r[0], kk[0], v[0], lw, a, kx, i == 1))
    yf, yb = _wkv_step(dirs, st_ref)
    yf_o[0] = yf
    yb_o[0] = yb

    @pl.when(j == pl.num_programs(1) - 1)
    def _():
        sfin_o[0] = st_ref[...]


def _wkv(r, kk, v, k, wdad, p, s0):
    b, l, w = r.shape
    c = WKV_CHUNK
    nc = l // c
    fwd = pl.BlockSpec((1, c, w), lambda bi, j: (bi, j, 0))
    bwd = pl.BlockSpec((1, c, w), lambda bi, j: (bi, nc - 1 - j, 0))
    fwd2 = pl.BlockSpec((1, c, 2 * LANE), lambda bi, j: (bi, j, 0))
    bwd2 = pl.BlockSpec((1, c, 2 * LANE), lambda bi, j: (bi, nc - 1 - j, 0))
    st_spec = pl.BlockSpec((1,) + s0.shape[1:], lambda bi, j: (bi, 0, 0, 0, 0))
    consts = [p["w0"], p["w2"], p["a0"], p["a2"], p["k_a"]]
    return pl.pallas_call(
        _wkv_kernel,
        grid=(b, nc),
        in_specs=[fwd] * 4 + [fwd2] + [bwd] * 4 + [bwd2] + [_const_spec(x.shape) for x in consts]
        + [st_spec],
        out_specs=[fwd, bwd, st_spec],
        out_shape=[jax.ShapeDtypeStruct((b, l, w), F32), jax.ShapeDtypeStruct((b, l, w), F32),
                   jax.ShapeDtypeStruct(s0.shape, F32)],
        scratch_shapes=[pltpu.VMEM(s0.shape[1:], F32)],
        compiler_params=_cparams(("parallel", "arbitrary")),
        name="wkv",
    )(r, kk, v, k, wdad, r, kk, v, k, wdad, *consts, s0)


def _ssd_step(dirs, arow, st_ref):
    c = SSD_CHUNK
    npair = HD // 2
    lane = _lane((c, LANE))
    lo = lane < PD
    grp_rows = _row((c, LANE)) < NS
    zero = jnp.zeros((c, LANE), F32)
    units = []
    for d, (xs, bc, dt, bwd) in enumerate(dirs):
        o = HD * d
        acum = _mm_f32(_tri(c, bwd).astype(F32), dt * arow)
        acum_t = acum.T
        last = acum[0:1] if bwd else acum[c - 1:c]
        ea, de, cd = jnp.exp(acum), jnp.exp(last - acum), jnp.exp(last)
        bfull, cfull = bc[:, 0:LANE], bc[:, LANE:2 * LANE]
        causal = _tri(c, bwd)
        cb = [_mm_nt(jnp.where(lo, cfull, zero), bfull), _mm_nt(jnp.where(lo, zero, cfull), bfull)]

        def colsel(m, ha, hb, o=o):
            return jnp.where(lo, jnp.broadcast_to(m[:, o + ha:o + ha + 1], (c, LANE)),
                             jnp.broadcast_to(m[:, o + hb:o + hb + 1], (c, LANE)))

        for p in range(npair):
            ha, hb = 2 * p, 2 * p + 1
            decs = []
            for h in (ha, hb):
                seg = (jnp.broadcast_to(acum[:, o + h:o + h + 1], (c, c))
                       - jnp.broadcast_to(acum_t[o + h:o + h + 1, :], (c, c)))
                decs.append(jnp.where(causal, jnp.exp(jnp.where(causal, seg, 0.0)), 0.0))
            cdp = jnp.where(lo[0:1], jnp.broadcast_to(cd[:, o + ha:o + ha + 1], (1, LANE)),
                            jnp.broadcast_to(cd[:, o + hb:o + hb + 1], (1, LANE)))
            xdt = xs[:, p * LANE:(p + 1) * LANE] * colsel(dt, ha, hb)
            units.append(dict(
                d=d, p=p, g0=(p // 2 == 0), cdp=cdp, bfull=bfull.astype(BF16),
                cfull=cfull.astype(BF16), xdt=xdt.astype(BF16),
                xde=(xdt * colsel(de, ha, hb)).astype(BF16), eap=colsel(ea, ha, hb),
                sc_a=(cb[p // 2] * decs[0]).astype(BF16), sc_b=(cb[p // 2] * decs[1]).astype(BF16)))
    for u in units:
        u["st"] = st_ref[u["d"], u["p"]]
    yi_a = [_mm(u["sc_a"], u["xdt"]) for u in units]
    yi_b = [_mm(u["sc_b"], u["xdt"]) for u in units]
    yo = [_mm(u["cfull"], u["st"]) for u in units]
    cs = [_mm_tn(u["bfull"], u["xde"]) for u in units]
    ys = [jnp.where(lo, a, b) + o_ * u["eap"] for u, a, b, o_ in zip(units, yi_a, yi_b, yo)]
    for u, x in zip(units, cs):
        keep = grp_rows if u["g0"] else jnp.logical_not(grp_rows)
        st_ref[u["d"], u["p"]] = u["st"] * u["cdp"] + jnp.where(keep, x, 0.0)
    return [jnp.concatenate(ys[d * npair:(d + 1) * npair], axis=1) for d in range(2)]


def _ssd_kernel(xf, bcf, dtf, xb, bcb, dtb, arow_ref, s0_ref, yf_o, yb_o, sfin_o, st_ref):
    j = pl.program_id(1)

    @pl.when(j == 0)
    def _():
        st_ref[...] = s0_ref[0]

    yf, yb = _ssd_step([(xf[0], bcf[0], dtf[0], False), (xb[0], bcb[0], dtb[0], True)],
                       arow_ref[...], st_ref)
    yf_o[0] = yf
    yb_o[0] = yb

    @pl.when(j == pl.num_programs(1) - 1)
    def _():
        sfin_o[0] = st_ref[...]


def _ssd(xs, bc, dt, arow, s0):
    b, l, _ = xs.shape
    c = SSD_CHUNK
    nc = l // c

    def spec(w, rev):
        if rev:
            return pl.BlockSpec((1, c, w), lambda bi, j: (bi, nc - 1 - j, 0))
        return pl.BlockSpec((1, c, w), lambda bi, j: (bi, j, 0))

    st_spec = pl.BlockSpec((1,) + s0.shape[1:], lambda bi, j: (bi, 0, 0, 0, 0))
    return pl.pallas_call(
        _ssd_kernel,
        grid=(b, nc),
        in_specs=[spec(512, False), spec(256, False), spec(128, False),
                  spec(512, True), spec(256, True), spec(128, True),
                  _const_spec(arow.shape), st_spec],
        out_specs=[spec(512, False), spec(512, True), st_spec],
        out_shape=[jax.ShapeDtypeStruct((b, l, 512), F32), jax.ShapeDtypeStruct((b, l, 512), F32),
                   jax.ShapeDtypeStruct(s0.shape, F32)],
        scratch_shapes=[pltpu.VMEM(s0.shape[1:], F32)],
        compiler_params=_cparams(("parallel", "arbitrary")),
        name="ssd",
    )(xs, bc, dt, xs, bc, dt, arow, s0)


def _l1_post_kernel(h_ref, mod_ref, yf, yb, r, k, v, g, ydf, ydb, xs, zg, lnw, lnb, rk, dsum, gn,
                    w_ref, bd_ref, o_ref):
    bd = bd_ref[...]
    y = yf[0] + yb[0]
    mu = _mm_split(y, bd) * (1.0 / NC)
    dl = y - mu
    var = _mm_split(dl * dl, bd) * (1.0 / NC)
    y = dl * lax.rsqrt(var + WKV_LN_EPS) * lnw[...] + lnb[...]
    y = y + _mm_split(r[0] * k[0] * rk[...], bd) * v[0]
    oc = y * g[0]
    x = xs[0]
    yd = (ydf[0] + dsum[0:1] * x) + (ydb[0] + dsum[1:2] * x)
    od = _rms(yd * _silu(zg[0]), gn[...], D_INNER)
    m = _mm(oc, w_ref[0:512, :]) + _mm(od, w_ref[512:1024, :])
    o_ref[0] = h_ref[0] + mod_ref[0][2:3] * m


def _l1_post(h, mod, per_batch, acts, consts, tm):
    b, l, _ = h.shape
    mod_map = (lambda bi, i: (bi, 0, 0)) if per_batch else (lambda bi, i: (0, 0, 0))
    x_spec = pl.BlockSpec((1, tm, D_MODEL), lambda bi, i: (bi, i, 0))
    a_spec = pl.BlockSpec((1, tm, 512), lambda bi, i: (bi, i, 0))
    return pl.pallas_call(
        _l1_post_kernel,
        grid=(b, l // tm),
        in_specs=[x_spec, pl.BlockSpec((1, 6, D_MODEL), mod_map)] + [a_spec] * len(acts)
        + [_const_spec(c.shape) for c in consts],
        out_specs=x_spec,
        out_shape=jax.ShapeDtypeStruct(h.shape, F32),
        compiler_params=_cparams(("parallel", "parallel")),
        name="l1_post",
    )(h, mod, *acts, *consts)


def _rope_tables(length, dim, offset, width):
    quarter = dim // 4
    inv = np.float32(ROPE_BASE) ** (-np.arange(quarter, dtype=np.float32) / np.float32(quarter))
    pos = np.arange(length)
    row = (pos // GRID_W).astype(np.float32)
    col = (pos % GRID_W).astype(np.float32)
    ar, ac = row[:, None] * inv, col[:, None] * inv
    c = np.concatenate([np.cos(ar), np.cos(ar), np.cos(ac), np.cos(ac)], axis=-1)
    s = np.concatenate([-np.sin(ar), np.sin(ar), -np.sin(ac), np.sin(ac)], axis=-1)
    cw = np.ones((length, width), np.float32)
    sw = np.zeros((length, width), np.float32)
    cw[:, offset:offset + dim] = c
    sw[:, offset:offset + dim] = s
    return cw, sw


def _pad_cols(w, n):
    return jnp.pad(w, ((0, 0), (0, n - w.shape[1])))


def _prep_l0(w_in, q_norm, w_uq, kv_norm, w_ukv, w_out):
    a_cols = 3 * HA * 2 * DA
    wqd = _pad_cols(w_in[:, a_cols:a_cols + Q_LORA], 256)
    wckv = w_in[:, a_cols + Q_LORA:a_cols + Q_LORA + KV_LORA]
    wkpe = jnp.pad(w_in[:, a_cols + Q_LORA + KV_LORA:], ((0, 0), (NOPE, LANE - NOPE - ROPE_B)))
    w = jnp.concatenate([w_in[:, :a_cols], wqd, wckv, wkpe], axis=1).astype(BF16)
    qn = _pad_cols(q_norm.reshape(1, -1), 256)
    wuq = jnp.pad(w_uq.reshape(Q_LORA, HB, NOPE + ROPE_B),
                  ((0, 256 - Q_LORA), (0, 0), (0, LANE - NOPE - ROPE_B))).reshape(256, HB * LANE)
    wukv = w_ukv.reshape(KV_LORA, HB, NOPE + VB)
    wk = jnp.pad(wukv[:, :, :NOPE], ((0, 0), (0, 0), (0, LANE - NOPE))).reshape(KV_LORA, HB * LANE)
    wv = wukv[:, :, NOPE:].reshape(KV_LORA, HB * VB)
    return dict(w=w, qn=qn, wuq=wuq.astype(BF16), kvn=kv_norm.reshape(1, -1),
                wukv=jnp.concatenate([wk, wv], axis=1).astype(BF16), w_out=w_out.astype(BF16))


def _prep_l1(w_in, mu, w0, w2, a0, a2, g2, k_k, k_a, r_k, ln_w, ln_b, conv_w, conv_b, A_log,
             dt_bias, D, gnorm, w_out, norm1):
    c_cols = 3 * HC * NC + 4 * 64 + 128
    conv_ch = D_INNER + 2 * G_SSM * NS
    wdt = w_in[:, c_cols + D_INNER + conv_ch:]
    wdt = _pad_cols(jnp.concatenate([wdt, wdt], axis=1), LANE)
    w = jnp.concatenate([w_in[:, :c_cols + D_INNER + conv_ch], wdt], axis=1).astype(BF16)
    z = jnp.zeros((64, HC * NC), F32)
    w2p = jnp.stack([jnp.concatenate([w2[0], z], 0), jnp.concatenate([z, w2[1]], 0)]).astype(BF16)
    a2p = jnp.stack([jnp.concatenate([a2[0], z], 0), jnp.concatenate([z, a2[1]], 0)]).astype(BF16)
    hid = jnp.arange(HC * NC) // NC
    bd = (hid[:, None] == hid[None, :]).astype(BF16)
    dtb = _pad_cols(dt_bias.reshape(1, 2 * HD), LANE)
    arow = _pad_cols((-jnp.exp(A_log.astype(F32))).reshape(1, 2 * HD), LANE)
    return dict(norm1=norm1.reshape(1, -1), w_in=w, mu=mu.reshape(1, -1), k_k=k_k.reshape(1, -1),
                k_a=k_a.reshape(1, -1), w0=w0, w2=w2p, a0=a0, a2=a2p, g2=g2.astype(BF16),
                conv_w=conv_w, conv_b=conv_b.reshape(1, -1), dt_bias=dtb, bd=bd, arow=arow,
                ln_w=ln_w.reshape(1, -1), ln_b=ln_b.reshape(1, -1), r_k=r_k.reshape(1, -1),
                dsum=jnp.repeat(D, PD, axis=1), gnorm=gnorm.reshape(1, -1),
                w_out=w_out.astype(BF16))


def _wkv_state_in(s):
    b = s.shape[0]
    st = s.reshape(b, HC // 2, 2, NC, NC)
    z = jnp.zeros_like(st[:, :, 0])
    top = jnp.concatenate([st[:, :, 0], z], axis=-1)
    bot = jnp.concatenate([z, st[:, :, 1]], axis=-1)
    return jnp.concatenate([top, bot], axis=-2)


def _wkv_state_out(s):
    a = s[:, :, :NC, :NC]
    b_ = s[:, :, NC:, NC:]
    return jnp.stack([a, b_], axis=2).reshape(s.shape[0], HC, NC, NC)


def _ssd_state_in(s):
    b = s.shape[0]
    st = jnp.swapaxes(s, -1, -2).reshape(b, HD // 2, 2, NS, PD)
    pair = jnp.concatenate([st[:, :, 0], st[:, :, 1]], axis=-1)
    z = jnp.zeros_like(pair)
    g0 = jnp.concatenate([pair, z], axis=-2)
    g1 = jnp.concatenate([z, pair], axis=-2)
    is_g0 = (jnp.arange(HD // 2) < HD // 4)[None, :, None, None]
    return jnp.where(is_g0, g0, g1)


def _ssd_state_out(s):
    is_g0 = (jnp.arange(HD // 2) < HD // 4)[None, :, None, None]
    pair = jnp.where(is_g0, s[:, :, :NS, :], s[:, :, NS:, :])
    st = jnp.stack([pair[..., :PD], pair[..., PD:]], axis=2).reshape(s.shape[0], HD, NS, PD)
    return jnp.swapaxes(st, -1, -2)


def _layer0(h, mod, per_batch, p, norm1, ctx, tables, lam_vecs, subln, lambda_init, tm, tq):
    qa, ka, va, qb, ckv, kpe = _l0_proj(h, mod, per_batch, norm1, p["w"], p["qn"], p["wuq"],
                                        p["kvn"], tables, tm)
    own = (ka, va, ckv, kpe)
    if ctx is not None:
        ck, cv, cckv, ckpe = ctx
        ka = jnp.concatenate([ck.astype(BF16), ka], axis=2)
        va = jnp.concatenate([cv.astype(BF16), va], axis=2)
        ckv = jnp.concatenate([cckv, ckv], axis=1)
        kpe = jnp.concatenate([jnp.pad(ckpe, ((0, 0), (0, 0), (NOPE, LANE - NOPE - ROPE_B))), kpe],
                              axis=1)
    kb, vb = _kvup(ckv, kpe, p["wukv"], min(512, ckv.shape[1]))
    oa = _attention(qa, ka, va, tq, tq // 64, True, lam_vecs, subln, lambda_init)
    tqb = min(2 * tq, qb.shape[2])
    ob = _attention(qb, kb, vb, tqb, tqb // 128, False)
    return _oproj(h, mod, per_batch, oa, ob, p["w_out"], min(2 * tm, h.shape[1])), own


def _layer1(h, mod, per_batch, p, wkv0, ssm0, tm):
    r, k, v, kk, g, wdad, zg, xs, bc, dt = _l1_proj(h, mod, per_batch, p, tm)
    yf, yb, s_wkv = _wkv(r, kk, v, k, wdad, p, wkv0)
    ydf, ydb, s_ssm = _ssd(xs, bc, dt, p["arow"], ssm0)
    consts = [p["ln_w"], p["ln_b"], p["r_k"], p["dsum"], p["gnorm"], p["w_out"], p["bd"]]
    h = _l1_post(h, mod, per_batch, [yf, yb, r, k, v, g, ydf, ydb, xs, zg], consts,
                 min(2 * tm, h.shape[1]))
    return h, s_wkv, s_ssm


def kernel(x_prompt, x_sample, cache_l0_k, cache_l0_v, cache_l0_ckv, cache_l0_kpe, state_l1_wkv_fwd, state_l1_wkv_bwd, state_l1_ssm_fwd, state_l1_ssm_bwd, c, c_ctx, ada_w_0, ada_b_0, norm1_0, norm2_0, ffn_up_0, ffn_conv_w_0, ffn_conv_b_0, ffn_down_0, l0_w_in, l0_lambda_q1, l0_lambda_k1, l0_lambda_q2, l0_lambda_k2, l0_subln, l0_q_norm, l0_w_uq, l0_kv_norm, l0_w_ukv, l0_w_out, ada_w_1, ada_b_1, norm1_1, norm2_1, ffn_up_1, ffn_conv_w_1, ffn_conv_b_1, ffn_down_1, l1_w_in, l1_mu, l1_w0, l1_w2, l1_a0, l1_a2, l1_g2, l1_k_k, l1_k_a, l1_r_k, l1_ln_w, l1_ln_b, l1_conv_w, l1_conv_b, l1_A_log, l1_dt_bias, l1_D, l1_gnorm, l1_w_out, norm_f):
    bc_, lc, _ = x_prompt.shape
    bl, ll, _ = x_sample.shape
    tm_c, tm_l = min(256, lc), min(256, ll)
    cond = jnp.concatenate([c_ctx[None, :], c], axis=0)
    cond8 = jnp.pad(cond, ((0, 8 - cond.shape[0]), (0, 0)))
    mods = [_ada(cond8, w, b).reshape(8, 6, D_MODEL) for w, b in ((ada_w_0, ada_b_0), (ada_w_1, ada_b_1))]
    p0 = _prep_l0(l0_w_in, l0_q_norm, l0_w_uq, l0_kv_norm, l0_w_ukv, l0_w_out)
    tables = _rope_tables(ll, DA, 0, DA)
    tables = tuple(np.concatenate([t, t], axis=1) for t in tables) + _rope_tables(ll, ROPE_B, NOPE, LANE)
    tables = tuple(jnp.asarray(t) for t in tables)
    lam_vecs = jnp.pad(jnp.stack([l0_lambda_q1, l0_lambda_k1, l0_lambda_q2, l0_lambda_k2]),
                       ((0, 4), (0, LANE - DA)))
    subln = l0_subln.reshape(1, -1)
    lambda_init = 0.8 - 0.6 * math.exp(-0.3 * 0)
    n1 = norm1_0.reshape(1, -1)
    h_ctx, own = _layer0(x_prompt, mods[0][0:1], False, p0, n1, None, None, lam_vecs, subln,
                         lambda_init, tm_c, min(256, lc))
    h_lat, _ = _layer0(x_sample, mods[0][1:1 + bl], True, p0, n1,
                       (cache_l0_k, cache_l0_v, cache_l0_ckv, cache_l0_kpe), tables, lam_vecs,
                       subln, lambda_init, tm_l, min(256, ll))
    ffn0 = (norm2_0.reshape(1, -1), ffn_up_0.astype(BF16), ffn_conv_w_0, ffn_conv_b_0.reshape(1, -1),
            ffn_down_0.astype(BF16))
    h_ctx = _ffn(h_ctx, mods[0][0:1], False, *ffn0, None, tm_c)
    h_lat = _ffn(h_lat, mods[0][1:1 + bl], True, *ffn0, None, min(512, ll))
    p1 = _prep_l1(l1_w_in, l1_mu, l1_w0, l1_w2, l1_a0, l1_a2, l1_g2, l1_k_k, l1_k_a, l1_r_k,
                  l1_ln_w, l1_ln_b, l1_conv_w, l1_conv_b, l1_A_log, l1_dt_bias, l1_D, l1_gnorm,
                  l1_w_out, norm1_1)
    zero_state = jnp.zeros((bc_, 2, HC // 2, LANE, LANE), F32)
    h_ctx, s_wkv, s_ssm = _layer1(h_ctx, mods[1][0:1], False, p1, zero_state, zero_state, tm_c)
    wkv0 = jnp.stack([_wkv_state_in(state_l1_wkv_fwd), _wkv_state_in(state_l1_wkv_bwd)], axis=1)
    ssm0 = jnp.stack([_ssd_state_in(state_l1_ssm_fwd), _ssd_state_in(state_l1_ssm_bwd)], axis=1)
    h_lat, _, _ = _layer1(h_lat, mods[1][1:1 + bl], True, p1, wkv0, ssm0, tm_l)
    ffn1 = (norm2_1.reshape(1, -1), ffn_up_1.astype(BF16), ffn_conv_w_1, ffn_conv_b_1.reshape(1, -1),
            ffn_down_1.astype(BF16))
    nf = norm_f.reshape(1, -1)
    y_prompt = _ffn(h_ctx, mods[1][0:1], False, *ffn1, nf, tm_c)
    y_sample = _ffn(h_lat, mods[1][1:1 + bl], True, *ffn1, nf, min(512, ll))
    ka, va, ckv, kpe = own
    return (y_prompt, y_sample, ka, va, ckv, kpe[:, :, NOPE:NOPE + ROPE_B],
            _wkv_state_out(s_wkv[:, 0]), _wkv_state_out(s_wkv[:, 1]),
            _ssd_state_out(s_ssm[:, 0]), _ssd_state_out(s_ssm[:, 1]))
```
